```python
import math
import jax, jax.numpy as jnp
from jax import lax
import numpy as np

D_MODEL = 1024
BATCH = 4
SEQ = 8192
DEPTH = 2
DEC_BATCH = 32
DEC_SEQ = 1
PAST_LEN = 16384
PAGE_SIZE = 128

N_A_LAYERS = DEPTH // 2
N_B_LAYERS = DEPTH - N_A_LAYERS
POOL_WINDOWS = (2, 4, 8, 16)
N_POOL_GROUPS = len(POOL_WINDOWS)
POOL_GROUP_DIM = D_MODEL // N_POOL_GROUPS
POOL_BUF = max(POOL_WINDOWS) - 1
D_FF = ((8 * D_MODEL // 3 + 255) // 256) * 256
HEAD_DIM = 128
N_HEADS = D_MODEL // HEAD_DIM
DILATED_GROUPS = ((128, 1), (512, 4), (2048, 16))
N_DIL = len(DILATED_GROUPS)
WINDOW_MAX = max(w for w, _ in DILATED_GROUPS)
ATTN_BLOCK = 128
ROT_DIM = HEAD_DIM // 4
ROPE_THETA = 500000.0
RMS_EPS = 1e-6

kernel_name = "yoco_pool_dilated_swa_step"


def rms_norm(x, g):
    xf = x.astype(jnp.float32)
    y = xf * lax.rsqrt(jnp.mean(xf * xf, axis=-1, keepdims=True) + RMS_EPS)
    return (y * g.astype(jnp.float32)).astype(x.dtype)


def swiglu(x, w_in, w_out):
    gu = x @ w_in
    g, u = jnp.split(gu, 2, axis=-1)
    return (jax.nn.silu(g) * u) @ w_out


def partial_rope(x, pos):
    half = ROT_DIM // 2
    inv_freq = ROPE_THETA ** (-jnp.arange(0, ROT_DIM, 2, dtype=jnp.float32) / ROT_DIM)
    ang = pos.astype(jnp.float32)[:, None] * inv_freq[None, :]
    cos = jnp.cos(ang)[:, None, :]
    sin = jnp.sin(ang)[:, None, :]
    xr = x[..., :ROT_DIM].astype(jnp.float32)
    x1, x2 = xr[..., :half], xr[..., half:]
    rot = jnp.concatenate([x1 * cos - x2 * sin, x2 * cos + x1 * sin], axis=-1)
    return jnp.concatenate([rot.astype(x.dtype), x[..., ROT_DIM:]], axis=-1)


def pool_mixer(u_ext, pos0, w_group, scale, w_o):
    B, n_ext, D = u_ext.shape
    T = n_ext - POOL_BUF
    cs = jnp.cumsum(u_ext.astype(jnp.float32), axis=1)
    cs = jnp.pad(cs, ((0, 0), (1, 0), (0, 0)))
    hi = cs[:, POOL_BUF + 1:]
    cur = u_ext[:, POOL_BUF:].astype(jnp.float32)
    pos = pos0 + jnp.arange(T, dtype=jnp.int32)
    parts = []
    for g, w in enumerate(POOL_WINDOWS):
        c = slice(g * POOL_GROUP_DIM, (g + 1) * POOL_GROUP_DIM)
        lo = lax.slice_in_dim(cs, POOL_BUF + 1 - w, POOL_BUF + 1 - w + T, axis=1)[..., c]
        cnt = jnp.minimum(pos + 1, w).astype(jnp.float32)[None, :, None]
        parts.append((hi[..., c] - lo) / cnt - cur[..., c])
    p = jnp.stack(parts, axis=2).astype(u_ext.dtype)
    z = jnp.einsum('btgc,gcd->btgd', p, w_group).reshape(B, T, D) * scale
    return z @ w_o


def dilated_attn_prompt(q, k, v, window, dil):
    B, S, H, E = q.shape
    sw = window // dil
    L = S // dil
    nb = -(-L // ATTN_BLOCK)
    Lp = nb * ATTN_BLOCK

    def to_sub(a):
        a = a.reshape(B, L, dil, H, E).transpose(0, 2, 1, 3, 4)
        a = jnp.pad(a, ((0, 0), (0, 0), (0, Lp - L), (0, 0), (0, 0)))
        return a.reshape(B, dil, nb, ATTN_BLOCK, H, E)

    def with_prev(a):
        prev = jnp.pad(a, ((0, 0), (0, 0), (1, 0), (0, 0), (0, 0), (0, 0)))[:, :, :nb]
        return jnp.concatenate([prev, a], axis=3)

    qb = to_sub(q)
    kc = with_prev(to_sub(k))
    vc = with_prev(to_sub(v))
    s = jnp.einsum('brnqhe,brnkhe->brnhqk', qb, kc,
                   preferred_element_type=jnp.float32) * (HEAD_DIM ** -0.5)
    blk = jnp.arange(nb)[:, None, None]
    qi = blk * ATTN_BLOCK + jnp.arange(ATTN_BLOCK)[None, :, None]
    kj = (blk - 1) * ATTN_BLOCK + jnp.arange(2 * ATTN_BLOCK)[None, None, :]
    diff = qi - kj
    valid = (diff >= 0) & (diff <= sw) & (kj >= 0)
    s = jnp.where(valid[None, None, :, None], s, -jnp.inf)
    m = jnp.max(s, axis=-1, keepdims=True)
    p = jnp.exp(s - m)
    den = jnp.sum(p, axis=-1, keepdims=True)
    o = jnp.einsum('brnhqk,brnkhe->brnqhe', (p / den).astype(v.dtype), vc)
    lse = (m + jnp.log(den))[..., 0]
    o = o.reshape(B, dil, Lp, H, E)[:, :, :L].transpose(0, 2, 1, 3, 4).reshape(B, S, H, E)
    lse = lse.transpose(0, 1, 2, 4, 3).reshape(B, dil, Lp, H)[:, :, :L]
    lse = lse.transpose(0, 2, 1, 3).reshape(B, S, H)
    return o, lse


def dilated_attn_sample(q, k_ext, v_ext, window, dil):
    B, T, H, E = q.shape
    n_past = k_ext.shape[1] - T
    sw = window // dil
    idx = n_past + jnp.arange(T)[:, None] - dil * jnp.arange(sw + 1)[None, :]
    valid = idx >= 0
    idx = jnp.maximum(idx, 0)
    kg = k_ext[:, idx]
    vg = v_ext[:, idx]
    s = jnp.einsum('bthe,btkhe->bthk', q, kg,
                   preferred_element_type=jnp.float32) * (HEAD_DIM ** -0.5)
    s = jnp.where(valid[None, :, None, :], s, -jnp.inf)
    m = jnp.max(s, axis=-1, keepdims=True)
    p = jnp.exp(s - m)
    den = jnp.sum(p, axis=-1, keepdims=True)
    o = jnp.einsum('bthk,btkhe->bthe', (p / den).astype(v_ext.dtype), vg)
    lse = (m + jnp.log(den))[..., 0]
    return o, lse


def combine_by_denominator(outs, lses):
    a = jax.nn.softmax(jnp.stack(lses, axis=0), axis=0)
    o = jnp.stack(outs, axis=0).astype(jnp.float32)
    return jnp.sum(a[..., None] * o, axis=0)


def trunk(x, pos0, pool_prev, k_past, v_past, p):
    B, T, D = x.shape
    pos = pos0 + jnp.arange(T, dtype=jnp.int32)
    pool_new = []
    k_all = v_all = k_state = v_state = None
    for l in range(DEPTH):
        if l == N_A_LAYERS:
            kv = rms_norm(x, p['kv_norm']) @ p['w_kv']
            k, v = jnp.split(kv, 2, axis=-1)
            k = partial_rope(k.reshape(B, T, N_HEADS, HEAD_DIM), pos)
            v = v.reshape(B, T, N_HEADS, HEAD_DIM)
            if k_past is None:
                k_all, v_all = k, v
                n_keep = min(WINDOW_MAX, T)
            else:
                k_all = jnp.concatenate([k_past.astype(k.dtype), k], axis=1)
                v_all = jnp.concatenate([v_past.astype(v.dtype), v], axis=1)
                n_keep = k_past.shape[1]
            k_state = k_all[:, k_all.shape[1] - n_keep:]
            v_state = v_all[:, v_all.shape[1] - n_keep:]
        x = x + 0.5 * swiglu(rms_norm(x, p['ffn1_norm'][l]), p['ffn1_w_in'][l], p['ffn1_w_out'][l])
        h = rms_norm(x, p['mix_norm'][l])
        if l < N_A_LAYERS:
            prev = jnp.zeros((B, POOL_BUF, D), h.dtype) if pool_prev is None else pool_prev[l].astype(h.dtype)
            u_ext = jnp.concatenate([prev, h], axis=1)
            pool_new.append(u_ext[:, -POOL_BUF:])
            x = x + pool_mixer(u_ext, pos0, p['pool_w_group'][l], p['pool_scale'][l], p['pool_w_o'][l])
        else:
            b = l - N_A_LAYERS
            q = (h @ p['attn_w_q'][b]).reshape(B, T, N_DIL * N_HEADS, HEAD_DIM)
            q = partial_rope(q, pos).reshape(B, T, N_DIL, N_HEADS, HEAD_DIM)
            outs, lses = [], []
            for g, (w, d) in enumerate(DILATED_GROUPS):
                if k_past is None:
                    o, s = dilated_attn_prompt(q[:, :, g], k_all, v_all, w, d)
                else:
                    o, s = dilated_attn_sample(q[:, :, g], k_all, v_all, w, d)
                outs.append(o)
                lses.append(s)
            att = combine_by_denominator(outs, lses).reshape(B, T, N_HEADS * HEAD_DIM).astype(x.dtype)
            x = x + att @ p['attn_w_o'][b]
        x = x + 0.5 * swiglu(rms_norm(x, p['ffn2_norm'][l]), p['ffn2_w_in'][l], p['ffn2_w_out'][l])
    y = rms_norm(x, p['final_norm'])
    return y, jnp.stack(pool_new, axis=0), k_state, v_state


def setup_inputs(seed: int = 0) -> dict:
    key = jax.random.key(seed)
    ks = jax.random.split(key, 24)
    f32 = jnp.float32

    def nrm(k, shape, scale):
        return jax.random.normal(k, shape, f32) * scale

    def gain(k, shape):
        return 1.0 + 0.02 * jax.random.normal(k, shape, f32)

    wb = min(WINDOW_MAX, PAST_LEN)
    qw = N_DIL * N_HEADS * HEAD_DIM
    hw = N_HEADS * HEAD_DIM
    return {
        'x_prompt': nrm(ks[0], (BATCH, SEQ, D_MODEL), 1.0),
        'x_sample': nrm(ks[1], (DEC_BATCH, DEC_SEQ, D_MODEL), 1.0),
        'state_pool': nrm(ks[2], (N_A_LAYERS, DEC_BATCH, POOL_BUF, D_MODEL), 1.0),
        'cache_k': nrm(ks[3], (DEC_BATCH, wb, N_HEADS, HEAD_DIM), 1.0),
        'cache_v': nrm(ks[4], (DEC_BATCH, wb, N_HEADS, HEAD_DIM), 1.0),
        'ffn1_norm': gain(ks[5], (DEPTH, D_MODEL)),
        'ffn1_w_in': nrm(ks[6], (DEPTH, D_MODEL, 2 * D_FF), D_MODEL ** -0.5),
        'ffn1_w_out': nrm(ks[7], (DEPTH, D_FF, D_MODEL), D_FF ** -0.5),
        'mix_norm': gain(ks[8], (DEPTH, D_MODEL)),
        'ffn2_norm': gain(ks[9], (DEPTH, D_MODEL)),
        'ffn2_w_in': nrm(ks[10], (DEPTH, D_MODEL, 2 * D_FF), D_MODEL ** -0.5),
        'ffn2_w_out': nrm(ks[11], (DEPTH, D_FF, D_MODEL), D_FF ** -0.5),
        'pool_w_group': nrm(ks[12], (N_A_LAYERS, N_POOL_GROUPS, POOL_GROUP_DIM, POOL_GROUP_DIM), POOL_GROUP_DIM ** -0.5),
        'pool_scale': gain(ks[13], (N_A_LAYERS, D_MODEL)),
        'pool_w_o': nrm(ks[14], (N_A_LAYERS, D_MODEL, D_MODEL), D_MODEL ** -0.5),
        'kv_norm': gain(ks[15], (D_MODEL,)),
        'w_kv': nrm(ks[16], (D_MODEL, 2 * hw), D_MODEL ** -0.5),
        'attn_w_q': nrm(ks[17], (N_B_LAYERS, D_MODEL, qw), D_MODEL ** -0.5),
        'attn_w_o': nrm(ks[18], (N_B_LAYERS, hw, D_MODEL), hw ** -0.5),
        'final_norm': gain(ks[19], (D_MODEL,)),
    }


def reference(x_prompt, x_sample, state_pool, cache_k, cache_v,
              ffn1_norm, ffn1_w_in, ffn1_w_out, mix_norm, ffn2_norm, ffn2_w_in, ffn2_w_out,
              pool_w_group, pool_scale, pool_w_o, kv_norm, w_kv, attn_w_q, attn_w_o, final_norm):
    params = {
        'ffn1_norm': ffn1_norm, 'ffn1_w_in': ffn1_w_in, 'ffn1_w_out': ffn1_w_out,
        'mix_norm': mix_norm,
        'ffn2_norm': ffn2_norm, 'ffn2_w_in': ffn2_w_in, 'ffn2_w_out': ffn2_w_out,
        'pool_w_group': pool_w_group, 'pool_scale': pool_scale, 'pool_w_o': pool_w_o,
        'kv_norm': kv_norm, 'w_kv': w_kv, 'attn_w_q': attn_w_q, 'attn_w_o': attn_w_o,
        'final_norm': final_norm,
    }
    past_len = PAST_LEN
    y_prompt, pool_p, k_p, v_p = trunk(x_prompt, 0, None, None, None, params)
    y_sample, pool_s, k_s, v_s = trunk(x_sample, past_len, state_pool, cache_k, cache_v, params)
    return (y_prompt, y_sample, pool_p, pool_s, k_p, v_p, k_s, v_s)
```

```python
import functools
import math

import jax
import jax.numpy as jnp
from jax import lax
from jax.experimental import pallas as pl
from jax.experimental.pallas import tpu as pltpu

F32 = jnp.float32
BF16 = jnp.bfloat16

POOL_WINDOWS = (2, 4, 8, 16)
POOL_BUF = max(POOL_WINDOWS) - 1
HEAD_DIM = 128
DILATED_GROUPS = ((128, 1), (512, 4), (2048, 16))
WINDOW_MAX = max(w for w, _ in DILATED_GROUPS)
ATTN_BLOCK = 128
ROT_DIM = HEAD_DIM // 4
ROPE_THETA = 500000.0
RMS_EPS = 1e-6

V7X_VMEM_BYTES = 64 * 1024 * 1024
VMEM_LIMIT = (V7X_VMEM_BYTES * 13) // 16
LANES = 128
HALO = 16

ROW_TILE = 512
FF_CHUNK = 256
ATTN_ROWS = 512


def _const_spec(shape):
    n = len(shape)
    return pl.BlockSpec(shape, lambda *_: (0,) * n, pipeline_mode=pl.Buffered(1))


def _params(n_axes):
    return pltpu.CompilerParams(dimension_semantics=("arbitrary",) * n_axes,
                                vmem_limit_bytes=VMEM_LIMIT)


def _rms(x, g):
    return x * lax.rsqrt(jnp.mean(x * x, axis=-1, keepdims=True) + RMS_EPS) * g


def _swiglu_half(x, g_ref, win_ref, wout_ref, h_ref):
    d_ff = wout_ref.shape[0]
    xn = _rms(x, g_ref[...]).astype(BF16)
    for c in range(0, d_ff, FF_CHUNK):
        gate = jnp.dot(xn, win_ref[:, c:c + FF_CHUNK], preferred_element_type=F32)
        up = jnp.dot(xn, win_ref[:, d_ff + c:d_ff + c + FF_CHUNK], preferred_element_type=F32)
        h_ref[:, c:c + FF_CHUNK] = (gate * jax.nn.sigmoid(gate) * up).astype(BF16)
    return x + 0.5 * jnp.dot(h_ref[...], wout_ref[...], preferred_element_type=F32)


def _ffn_kernel(x_ref, g_ref, win_ref, wout_ref, o_ref, h_ref):
    o_ref[...] = _swiglu_half(x_ref[...], g_ref, win_ref, wout_ref, h_ref)


def _ffn_final_kernel(x_ref, g_ref, win_ref, wout_ref, gf_ref, o_ref, h_ref):
    o_ref[...] = _rms(_swiglu_half(x_ref[...], g_ref, win_ref, wout_ref, h_ref), gf_ref[...])


def _ffn(x, g, w_in, w_out, final_g=None):
    n, d = x.shape
    d_ff = w_out.shape[0]
    tm = min(ROW_TILE, n)
    row = pl.BlockSpec((tm, d), lambda i: (i, 0))
    in_specs = [row, _const_spec((1, d)), _const_spec(w_in.shape), _const_spec(w_out.shape)]
    args = [x, g, w_in, w_out]
    body = _ffn_kernel
    if final_g is not None:
        in_specs.append(_const_spec((1, d)))
        args.append(final_g)
        body = _ffn_final_kernel
    return pl.pallas_call(
        body, grid=(n // tm,), in_specs=in_specs, out_specs=row,
        out_shape=jax.ShapeDtypeStruct((n, d), F32),
        scratch_shapes=[pltpu.VMEM((tm, d_ff), BF16)],
        compiler_params=_params(1), name="ffn")(*args)


def _pool_project(p_groups, x, wg_ref, scale_ref, wo_ref):
    gd = wg_ref.shape[1]
    z = jnp.concatenate(
        [jnp.dot(p.astype(BF16), wg_ref[i], preferred_element_type=F32) for i, p in enumerate(p_groups)],
        axis=1)
    z = (z * scale_ref[...]).astype(BF16)
    del gd
    return x + jnp.dot(z, wo_ref[...], preferred_element_type=F32)


def _pool_prompt_kernel(x_ref, g_ref, wg_ref, scale_ref, wo_ref, o_ref, state_ref, ext_ref):
    j = pl.program_id(1)
    tm, d = x_ref.shape
    gd = d // len(POOL_WINDOWS)
    x = x_ref[...]
    h = _rms(x, g_ref[...])

    @pl.when(j == 0)
    def _():
        ext_ref[0:HALO, :] = jnp.zeros((HALO, d), F32)

    @pl.when(j > 0)
    def _():
        ext_ref[0:HALO, :] = ext_ref[tm:tm + HALO, :]

    ext_ref[HALO:HALO + tm, :] = h
    pos1 = (j * tm + 1 + lax.broadcasted_iota(jnp.int32, (tm, 1), 0)).astype(F32)
    parts = []
    for gi, w in enumerate(POOL_WINDOWS):
        c0, c1 = gi * gd, (gi + 1) * gd
        cur = h[:, c0:c1]
        tot = cur
        for s in range(1, w):
            tot = tot + ext_ref[HALO - s:HALO - s + tm, c0:c1]
        inv_cnt = 1.0 / jnp.minimum(pos1, float(w))
        parts.append(tot * inv_cnt - cur)
    o_ref[...] = _pool_project(parts, x, wg_ref, scale_ref, wo_ref)
    state_ref[...] = ext_ref[tm:tm + HALO, :]


def _pool_prompt(x, g, w_group, scale, w_o):
    b, s, d = x.shape
    tm = ROW_TILE
    row = pl.BlockSpec((None, tm, d), lambda i, j: (i, j, 0))
    return pl.pallas_call(
        _pool_prompt_kernel, grid=(b, s // tm),
        in_specs=[row, _const_spec((1, d)), _const_spec(w_group.shape), _const_spec((1, d)),
                  _const_spec(w_o.shape)],
        out_specs=[row, pl.BlockSpec((None, HALO, d), lambda i, j: (i, 0, 0))],
        out_shape=[jax.ShapeDtypeStruct((b, s, d), F32), jax.ShapeDtypeStruct((b, HALO, d), F32)],
        scratch_shapes=[pltpu.VMEM((tm + HALO, d), F32)],
        compiler_params=_params(2), name="pool_prompt")(x, g, w_group, scale, w_o)


def _pool_sample_kernel(x_ref, st_ref, g_ref, wg_ref, scale_ref, wo_ref, o_ref, new_ref):
    n, d = x_ref.shape
    gd = d // len(POOL_WINDOWS)
    x = x_ref[...]
    h = _rms(x, g_ref[...])
    parts = []
    for gi, w in enumerate(POOL_WINDOWS):
        c0, c1 = gi * gd, (gi + 1) * gd
        cur = h[:, c0:c1]
        tot = cur
        for s in range(1, w):
            tot = tot + st_ref[POOL_BUF - s, :, c0:c1]
        parts.append(tot * (1.0 / w) - cur)
    o_ref[...] = _pool_project(parts, x, wg_ref, scale_ref, wo_ref)
    for r in range(POOL_BUF - 1):
        new_ref[r] = st_ref[r + 1]
    new_ref[POOL_BUF - 1] = h


def _pool_sample(x, st, g, w_group, scale, w_o):
    n, d = x.shape
    full = lambda shape: pl.BlockSpec(shape, lambda i: (0,) * len(shape))
    return pl.pallas_call(
        _pool_sample_kernel, grid=(1,),
        in_specs=[full((n, d)), full(st.shape), full((1, d)), full(w_group.shape), full((1, d)),
                  full(w_o.shape)],
        out_specs=[full((n, d)), full(st.shape)],
        out_shape=[jax.ShapeDtypeStruct((n, d), F32), jax.ShapeDtypeStruct(st.shape, F32)],
        compiler_params=_params(1), name="pool_sample")(x, st, g, w_group, scale, w_o)


def _rope_tables(pos):
    half = ROT_DIM // 2
    inv_freq = ROPE_THETA ** (-jnp.arange(0, ROT_DIM, 2, dtype=F32) / ROT_DIM)
    ang = pos.astype(F32)[:, None] * inv_freq[None, :]
    cos, sin = jnp.cos(ang), jnp.sin(ang)
    t = pos.shape[0]
    c = jnp.concatenate([cos, cos, jnp.ones((t, HEAD_DIM - ROT_DIM), F32)], axis=1)
    a = jnp.concatenate([-sin, jnp.zeros((t, HEAD_DIM - half), F32)], axis=1)
    b = jnp.concatenate([jnp.zeros((t, half), F32), sin, jnp.zeros((t, HEAD_DIM - ROT_DIM), F32)], axis=1)
    return c, a, b


def _rope_heads(x, c, a, b, n_heads):
    half = ROT_DIM // 2
    outs = []
    for hh in range(n_heads):
        xh = x[:, hh * HEAD_DIM:(hh + 1) * HEAD_DIM]
        outs.append(xh * c + pltpu.roll(xh, HEAD_DIM - half, 1) * a + pltpu.roll(xh, half, 1) * b)
    return outs


def _kv_kernel(x_ref, g_ref, w_ref, c_ref, a_ref, b_ref, k_ref, v_ref, kf_ref, vf_ref):
    d = x_ref.shape[1]
    n_heads = d // HEAD_DIM
    xn = _rms(x_ref[...], g_ref[...]).astype(BF16)
    kv = jnp.dot(xn, w_ref[...], preferred_element_type=F32)
    c, a, b = c_ref[...], a_ref[...], b_ref[...]
    for hh, kh in enumerate(_rope_heads(kv[:, :d], c, a, b, n_heads)):
        sl = slice(hh * HEAD_DIM, (hh + 1) * HEAD_DIM)
        kf_ref[:, sl] = kh
        k_ref[:, sl] = kh.astype(BF16)
    v = kv[:, d:]
    vf_ref[...] = v
    v_ref[...] = v.astype(BF16)


def _kv_proj(x, g, w_kv, tables, seq, keep):
    n, d = x.shape
    tm = min(ROW_TILE, seq)
    per_seq, kept = seq // tm, keep // tm
    row = pl.BlockSpec((tm, d), lambda i: (i, 0))
    tab = pl.BlockSpec((tm, HEAD_DIM), lambda i: (i % per_seq, 0))
    kept_row = pl.BlockSpec(
        (tm, d), lambda i: ((i // per_seq) * kept + jnp.maximum(i % per_seq - (per_seq - kept), 0), 0))
    return pl.pallas_call(
        _kv_kernel, grid=(n // tm,),
        in_specs=[row, _const_spec((1, d)), _const_spec(w_kv.shape), tab, tab, tab],
        out_specs=[row, row, kept_row, kept_row],
        out_shape=[jax.ShapeDtypeStruct((n, d), BF16), jax.ShapeDtypeStruct((n, d), BF16),
                   jax.ShapeDtypeStruct((n // seq * keep, d), F32),
                   jax.ShapeDtypeStruct((n // seq * keep, d), F32)],
        compiler_params=_params(1), name="kv_proj")(x, g, w_kv, *tables)


def _q_kernel(x_ref, g_ref, w_ref, c_ref, a_ref, b_ref, q_ref):
    n_slots = w_ref.shape[1] // HEAD_DIM
    xn = _rms(x_ref[...], g_ref[...]).astype(BF16)
    q = jnp.dot(xn, w_ref[...], preferred_element_type=F32)
    for hh, qh in enumerate(_rope_heads(q, c_ref[...], a_ref[...], b_ref[...], n_slots)):
        q_ref[:, hh * HEAD_DIM:(hh + 1) * HEAD_DIM] = qh.astype(q_ref.dtype)


def _q_proj(x, g, w_q, tables, seq, out_dtype):
    n, d = x.shape
    qw = w_q.shape[1]
    tm = min(ROW_TILE, seq)
    per_seq = seq // tm
    tab = pl.BlockSpec((tm, HEAD_DIM), lambda i: (i % per_seq, 0))
    return pl.pallas_call(
        _q_kernel, grid=(n // tm,),
        in_specs=[pl.BlockSpec((tm, d), lambda i: (i, 0)), _const_spec((1, d)), _const_spec(w_q.shape),
                  tab, tab, tab],
        out_specs=pl.BlockSpec((tm, qw), lambda i: (i, 0)),
        out_shape=jax.ShapeDtypeStruct((n, qw), out_dtype),
        compiler_params=_params(1), name="q_proj")(x, g, w_q, *tables)


def _attn_prompt_kernel(q_ref, kc_ref, kp_ref, vc_ref, vp_ref, o_ref, lse_ref, kk_ref, vv_ref):
    n = pl.program_id(2)
    tq, d = q_ref.shape
    blk = ATTN_BLOCK
    n_heads = d // HEAD_DIM
    kk_ref[0:blk, :] = kp_ref[...]
    kk_ref[blk:, :] = kc_ref[...]
    vv_ref[0:blk, :] = vp_ref[...]
    vv_ref[blk:, :] = vc_ref[...]
    qi = lax.broadcasted_iota(jnp.int32, (blk, 2 * blk), 0)
    kj = lax.broadcasted_iota(jnp.int32, (blk, 2 * blk), 1)
    band = (kj >= qi) & (kj <= qi + blk)
    lane = lax.broadcasted_iota(jnp.int32, (blk, LANES), 1)
    scale = HEAD_DIM ** -0.5
    for u in range(tq // blk):
        valid = band
        if u == 0:
            valid = band & ((kj >= blk) | (n > 0))
        lse_tile = jnp.zeros((blk, LANES), F32)
        for hh in range(n_heads):
            sl = slice(hh * HEAD_DIM, (hh + 1) * HEAD_DIM)
            qh = q_ref[u * blk:(u + 1) * blk, sl]
            kh = kk_ref[u * blk:(u + 2) * blk, sl]
            vh = vv_ref[u * blk:(u + 2) * blk, sl]
            s = lax.dot_general(qh, kh, (((1,), (1,)), ((), ())), preferred_element_type=F32) * scale
            s = jnp.where(valid, s, -jnp.inf)
            m = jnp.max(s, axis=1, keepdims=True)
            p = jnp.exp(s - m)
            den = jnp.sum(p, axis=1, keepdims=True)
            o = jnp.dot(p.astype(BF16), vh, preferred_element_type=F32)
            o_ref[u * blk:(u + 1) * blk, sl] = o / den
            lse_tile = jnp.where(lane == hh, m + jnp.log(den), lse_tile)
        lse_ref[u * blk:(u + 1) * blk, :] = lse_tile


def _attn_prompt(q, k, v, group, dil):
    b, s, qw = q.shape
    d = k.shape[2]
    n_groups = qw // d
    sub = s // dil
    tq = min(ATTN_ROWS, sub)
    blocks_per_step = tq // ATTN_BLOCK
    qv = q.reshape(b, sub, dil * qw)
    kv_ = k.reshape(b, sub, dil * d)
    vv_ = v.reshape(b, sub, dil * d)
    cur = pl.BlockSpec((None, tq, d), lambda i, r, n: (i, n, r))
    prev = pl.BlockSpec((None, ATTN_BLOCK, d),
                        lambda i, r, n: (i, jnp.maximum(n * blocks_per_step - 1, 0), r))
    o, lse = pl.pallas_call(
        _attn_prompt_kernel, grid=(b, dil, sub // tq),
        in_specs=[pl.BlockSpec((None, tq, d), lambda i, r, n: (i, n, r * n_groups + group)),
                  cur, prev, cur, prev],
        out_specs=[cur, pl.BlockSpec((None, tq, LANES), lambda i, r, n: (i, n, r))],
        out_shape=[jax.ShapeDtypeStruct((b, sub, dil * d), F32),
                   jax.ShapeDtypeStruct((b, sub, dil * LANES), F32)],
        scratch_shapes=[pltpu.VMEM((tq + ATTN_BLOCK, d), BF16), pltpu.VMEM((tq + ATTN_BLOCK, d), BF16)],
        compiler_params=_params(3), name=f"attn_prompt_d{dil}")(qv, kv_, kv_, vv_, vv_)
    return o.reshape(b, s, d), lse.reshape(b, s, LANES)


def _combine_kernel(*refs):
    n_g = (len(refs) - 3) // 2
    o_refs, lse_refs = refs[:n_g], refs[n_g:2 * n_g]
    x_ref, wo_ref, out_ref = refs[2 * n_g:]
    d = x_ref.shape[1]
    lses = [r[...] for r in lse_refs]
    top = functools.reduce(jnp.maximum, lses)
    es = [jnp.exp(l - top) for l in lses]
    inv = 1.0 / functools.reduce(jnp.add, es)
    ws = [e * inv for e in es]
    cols = []
    for hh in range(d // HEAD_DIM):
        sl = slice(hh * HEAD_DIM, (hh + 1) * HEAD_DIM)
        cols.append(functools.reduce(
            jnp.add, [w[:, hh:hh + 1] * o_ref[:, sl] for w, o_ref in zip(ws, o_refs)]))
    att = jnp.concatenate(cols, axis=1).astype(BF16)
    out_ref[...] = x_ref[...] + jnp.dot(att, wo_ref[...], preferred_element_type=F32)


def _combine(outs, lses, x, w_o):
    n, d = x.shape
    tm = ROW_TILE
    row = pl.BlockSpec((tm, d), lambda i: (i, 0))
    lrow = pl.BlockSpec((tm, LANES), lambda i: (i, 0))
    n_g = len(outs)
    return pl.pallas_call(
        _combine_kernel, grid=(n // tm,),
        in_specs=[row] * n_g + [lrow] * n_g + [row, _const_spec(w_o.shape)],
        out_specs=row, out_shape=jax.ShapeDtypeStruct((n, d), F32),
        compiler_params=_params(1), name="attn_combine")(*outs, *lses, x, w_o)


def _attn_sample_kernel(*refs):
    n_g = len(DILATED_GROUPS)
    q_ref, kn_ref, vn_ref = refs[:3]
    kc_refs = refs[3:3 + n_g]
    vc_refs = refs[3 + n_g:3 + 2 * n_g]
    att_ref = refs[3 + 2 * n_g]
    d = kn_ref.shape[1]
    scale = HEAD_DIM ** -0.5
    kn, vn = kn_ref[...], vn_ref[...]
    for hh in range(d // HEAD_DIM):
        sl = slice(hh * HEAD_DIM, (hh + 1) * HEAD_DIM)
        outs, lses = [], []
        for g in range(n_g):
            qh = q_ref[:, g * d + hh * HEAD_DIM:g * d + (hh + 1) * HEAD_DIM]
            s_past = jnp.sum(kc_refs[g][:, sl] * qh, axis=1, keepdims=True) * scale
            s_new = jnp.sum(kn[:, sl] * qh, axis=1, keepdims=True) * scale
            m = jnp.maximum(jnp.max(s_past, axis=0, keepdims=True), s_new)
            p_past = jnp.exp(s_past - m)
            p_new = jnp.exp(s_new - m)
            den = jnp.sum(p_past, axis=0, keepdims=True) + p_new
            o = jnp.sum(p_past * vc_refs[g][:, sl], axis=0, keepdims=True) + p_new * vn[:, sl]
            outs.append(o / den)
            lses.append(m + jnp.log(den))
        top = functools.reduce(jnp.maximum, lses)
        es = [jnp.exp(l - top) for l in lses]
        inv = 1.0 / functools.reduce(jnp.add, es)
        att_ref[:, sl] = functools.reduce(jnp.add, [e * inv * o for e, o in zip(es, outs)])


def _attn_sample(q, k_new, v_new, cache_k, cache_v):
    b, d = k_new.shape
    w = cache_k.shape[1]
    qw = q.shape[1]
    one = lambda width: pl.BlockSpec((None, 1, width), lambda i: (i, 0, 0))
    cache_args, cache_specs = [], []
    for cache in (cache_k, cache_v):
        for win, dil in DILATED_GROUPS:
            n_keys = win // dil
            last = w // dil // n_keys - 1
            cache_args.append(cache.reshape(b, w // dil, dil * d))
            cache_specs.append(pl.BlockSpec((None, n_keys, d), lambda i, last=last: (i, last, 0)))
    att = pl.pallas_call(
        _attn_sample_kernel, grid=(b,),
        in_specs=[one(qw), one(d), one(d)] + cache_specs,
        out_specs=one(d), out_shape=jax.ShapeDtypeStruct((b, 1, d), F32),
        compiler_params=_params(1), name="attn_sample")(
            q.reshape(b, 1, qw), k_new.reshape(b, 1, d), v_new.reshape(b, 1, d), *cache_args)
    return att.reshape(b, d)


def _dense_residual_kernel(a_ref, x_ref, w_ref, o_ref):
    o_ref[...] = x_ref[...] + jnp.dot(a_ref[...].astype(BF16), w_ref[...], preferred_element_type=F32)


def _dense_residual(a, x, w):
    n, d = x.shape
    full = lambda shape: pl.BlockSpec(shape, lambda i: (0,) * len(shape))
    return pl.pallas_call(
        _dense_residual_kernel, grid=(1,),
        in_specs=[full(a.shape), full((n, d)), full(w.shape)], out_specs=full((n, d)),
        out_shape=jax.ShapeDtypeStruct((n, d), F32),
        compiler_params=_params(1), name="dense_residual")(a, x, w)


CACHE_COPY_CHUNKS = 4


def _cache_roll_kernel(ck_ref, cv_ref, kn_ref, vn_ref, ko_ref, vo_ref, sem):
    b, w = ck_ref.shape[0], ck_ref.shape[1]
    step = b // CACHE_COPY_CHUNKS
    copies = []
    for src, new, dst in ((ck_ref, kn_ref, ko_ref), (cv_ref, vn_ref, vo_ref)):
        for c in range(CACHE_COPY_CHUNKS):
            bs = pl.ds(c * step, step)
            copies.append(pltpu.make_async_copy(
                src.at[bs, pl.ds(1, w - 1)], dst.at[bs, pl.ds(0, w - 1)], sem.at[len(copies)]))
        copies.append(pltpu.make_async_copy(new, dst.at[:, pl.ds(w - 1, 1)], sem.at[len(copies)]))
    for cp in copies:
        cp.start()
    for cp in copies:
        cp.wait()


def _cache_roll(cache_k, cache_v, k_new, v_new):
    hbm = pl.BlockSpec(memory_space=pl.ANY)
    shape = jax.ShapeDtypeStruct(cache_k.shape, cache_k.dtype)
    return pl.pallas_call(
        _cache_roll_kernel, in_specs=[hbm] * 4, out_specs=[hbm, hbm], out_shape=[shape, shape],
        scratch_shapes=[pltpu.SemaphoreType.DMA((2 * (CACHE_COPY_CHUNKS + 1),))],
        name="cache_roll")(cache_k, cache_v, k_new, v_new)


def kernel(x_prompt, x_sample, state_pool, cache_k, cache_v, ffn1_norm, ffn1_w_in, ffn1_w_out, mix_norm,
           ffn2_norm, ffn2_w_in, ffn2_w_out, pool_w_group, pool_scale, pool_w_o, kv_norm, w_kv, attn_w_q,
           attn_w_o, final_norm):
    b, s, d = x_prompt.shape
    bs = x_sample.shape[0]
    assert x_sample.shape[1] == 1
    n_heads = d // HEAD_DIM
    w_cache = cache_k.shape[1]
    past_len = 16384
    assert w_cache == min(WINDOW_MAX, past_len)
    keep = min(WINDOW_MAX, s)

    bf = lambda a: a.astype(BF16)
    f1_in, f1_out, f2_in, f2_out = bf(ffn1_w_in), bf(ffn1_w_out), bf(ffn2_w_in), bf(ffn2_w_out)
    wg, wpo, wkv, wq, wo = bf(pool_w_group[0]), bf(pool_w_o[0]), bf(w_kv), bf(attn_w_q[0]), bf(attn_w_o[0])
    vec = lambda a: a.reshape(1, d)

    tab_p = _rope_tables(jnp.arange(s, dtype=jnp.int32))
    tab_s = _rope_tables(jnp.full((bs,), past_len, dtype=jnp.int32))

    x = x_prompt.reshape(b * s, d)
    x = _ffn(x, vec(ffn1_norm[0]), f1_in[0], f1_out[0])
    x, pool_hist = _pool_prompt(x.reshape(b, s, d), vec(mix_norm[0]), wg, vec(pool_scale[0]), wpo)
    pool_p = pool_hist[:, HALO - POOL_BUF:][None]
    x = _ffn(x.reshape(b * s, d), vec(ffn2_norm[0]), f2_in[0], f2_out[0])
    k, v, k_keep, v_keep = _kv_proj(x, vec(kv_norm), wkv, tab_p, s, keep)
    x = _ffn(x, vec(ffn1_norm[1]), f1_in[1], f1_out[1])
    q = _q_proj(x, vec(mix_norm[1]), wq, tab_p, s, BF16)
    outs, lses = [], []
    for g, (win, dil) in enumerate(DILATED_GROUPS):
        assert win // dil == ATTN_BLOCK
        o, lse = _attn_prompt(q.reshape(b, s, -1), k.reshape(b, s, d), v.reshape(b, s, d), g, dil)
        outs.append(o.reshape(b * s, d))
        lses.append(lse.reshape(b * s, LANES))
    x = _combine(outs, lses, x, wo)
    y_prompt = _ffn(x, vec(ffn2_norm[1]), f2_in[1], f2_out[1], vec(final_norm)).reshape(b, s, d)
    k_p = k_keep.reshape(b, keep, n_heads, HEAD_DIM)
    v_p = v_keep.reshape(b, keep, n_heads, HEAD_DIM)

    xs = x_sample.reshape(bs, d)
    xs = _ffn(xs, vec(ffn1_norm[0]), f1_in[0], f1_out[0])
    hist = jnp.swapaxes(state_pool[0], 0, 1)
    xs, hist_new = _pool_sample(xs, hist, vec(mix_norm[0]), wg, vec(pool_scale[0]), wpo)
    pool_s = jnp.swapaxes(hist_new, 0, 1)[None]
    xs = _ffn(xs, vec(ffn2_norm[0]), f2_in[0], f2_out[0])
    _, _, k_new, v_new = _kv_proj(xs, vec(kv_norm), wkv, tab_s, bs, bs)
    xs = _ffn(xs, vec(ffn1_norm[1]), f1_in[1], f1_out[1])
    qs = _q_proj(xs, vec(mix_norm[1]), wq, tab_s, bs, F32)
    att = _attn_sample(qs, k_new, v_new, cache_k.reshape(bs, w_cache, d), cache_v.reshape(bs, w_cache, d))
    xs = _dense_residual(att, xs, wo)
    y_sample = _ffn(xs, vec(ffn2_norm[1]), f2_in[1], f2_out[1], vec(final_norm)).reshape(bs, 1, d)
    k_s, v_s = _cache_roll(cache_k, cache_v, k_new.reshape(bs, 1, n_heads, HEAD_DIM),
                           v_new.reshape(bs, 1, n_heads, HEAD_DIM))

    return (y_prompt, y_sample, pool_p, pool_s, k_p, v_p, k_s, v_s)
```

```python
import functools

import jax
import jax.numpy as jnp
from jax import lax
from jax.experimental import pallas as pl
from jax.experimental.pallas import tpu as pltpu

F32 = jnp.float32
BF16 = jnp.bfloat16

POOL_WINDOWS = (2, 4, 8, 16)
POOL_BUF = max(POOL_WINDOWS) - 1
HEAD_DIM = 128
DILATED_GROUPS = ((128, 1), (512, 4), (2048, 16))
N_DIL = len(DILATED_GROUPS)
WINDOW_MAX = max(w for w, _ in DILATED_GROUPS)
STRIDE = max(d for _, d in DILATED_GROUPS)
ATTN_BLOCK = 128
ROT_DIM = HEAD_DIM // 4
ROPE_THETA = 500000.0
RMS_EPS = 1e-6
PAST_LEN = 16384

V7X_VMEM_BYTES = 64 * 1024 * 1024
VMEM_LIMIT = (V7X_VMEM_BYTES * 13) // 16
LANES = 128
HALO = 16

ROW_TILE = 512
FF_CHUNK = 256
ATTN_QUERIES = 512
CACHE_TILE = 1024


def _const_spec(shape):
    n = len(shape)
    return pl.BlockSpec(shape, lambda *_: (0,) * n, pipeline_mode=pl.Buffered(1))


def _params(n_axes):
    return pltpu.CompilerParams(dimension_semantics=("arbitrary",) * n_axes,
                                vmem_limit_bytes=VMEM_LIMIT)


def _rms(x, g):
    return x * lax.rsqrt(jnp.mean(x * x, axis=-1, keepdims=True) + RMS_EPS) * g


def _swiglu_half(x, g_ref, win_ref, wout_ref, h_ref):
    d_ff = wout_ref.shape[0]
    xn = _rms(x, g_ref[...]).astype(BF16)
    for c in range(0, d_ff, FF_CHUNK):
        gate = jnp.dot(xn, win_ref[:, c:c + FF_CHUNK], preferred_element_type=F32)
        up = jnp.dot(xn, win_ref[:, d_ff + c:d_ff + c + FF_CHUNK], preferred_element_type=F32)
        h_ref[:, c:c + FF_CHUNK] = (gate * jax.nn.sigmoid(gate) * up).astype(BF16)
    return x + 0.5 * jnp.dot(h_ref[...], wout_ref[...], preferred_element_type=F32)


def _ffn_kernel(x_ref, g_ref, win_ref, wout_ref, o_ref, h_ref):
    o_ref[...] = _swiglu_half(x_ref[...], g_ref, win_ref, wout_ref, h_ref)


def _ffn_final_kernel(x_ref, g_ref, win_ref, wout_ref, gf_ref, o_ref, h_ref):
    o_ref[...] = _rms(_swiglu_half(x_ref[...], g_ref, win_ref, wout_ref, h_ref), gf_ref[...])


def _ffn(x, g, w_in, w_out, final_g=None):
    n, d = x.shape
    d_ff = w_out.shape[0]
    tm = min(ROW_TILE, n)
    row = pl.BlockSpec((tm, d), lambda i: (i, 0))
    in_specs = [row, _const_spec((1, d)), _const_spec(w_in.shape), _const_spec(w_out.shape)]
    args = [x, g, w_in, w_out]
    body = _ffn_kernel
    if final_g is not None:
        in_specs.append(_const_spec((1, d)))
        args.append(final_g)
        body = _ffn_final_kernel
    return pl.pallas_call(
        body, grid=(n // tm,), in_specs=in_specs, out_specs=row,
        out_shape=jax.ShapeDtypeStruct((n, d), F32),
        scratch_shapes=[pltpu.VMEM((tm, d_ff), BF16)],
        compiler_params=_params(1), name="ffn")(*args)


def _pool_project(p_groups, x, wg_ref, scale_ref, wo_ref):
    z = jnp.concatenate(
        [jnp.dot(p.astype(BF16), wg_ref[i], preferred_element_type=F32) for i, p in enumerate(p_groups)],
        axis=1)
    z = (z * scale_ref[...]).astype(BF16)
    return x + jnp.dot(z, wo_ref[...], preferred_element_type=F32)


def _pool_prompt_kernel(x_ref, g_ref, wg_ref, scale_ref, wo_ref, o_ref, state_ref, ext_ref):
    j = pl.program_id(1)
    tm, d = x_ref.shape
    gd = d // len(POOL_WINDOWS)
    x = x_ref[...]
    h = _rms(x, g_ref[...])

    @pl.when(j == 0)
    def _():
        ext_ref[0:HALO, :] = jnp.zeros((HALO, d), F32)

    @pl.when(j > 0)
    def _():
        ext_ref[0:HALO, :] = ext_ref[tm:tm + HALO, :]

    ext_ref[HALO:HALO + tm, :] = h
    pos1 = (j * tm + 1 + lax.broadcasted_iota(jnp.int32, (tm, 1), 0)).astype(F32)
    parts = []
    for gi, w in enumerate(POOL_WINDOWS):
        c0, c1 = gi * gd, (gi + 1) * gd
        cur = h[:, c0:c1]
        tot = cur
        for s in range(1, w):
            tot = tot + ext_ref[HALO - s:HALO - s + tm, c0:c1]
        inv_cnt = 1.0 / jnp.minimum(pos1, float(w))
        parts.append(tot * inv_cnt - cur)
    o_ref[...] = _pool_project(parts, x, wg_ref, scale_ref, wo_ref)
    state_ref[...] = ext_ref[tm:tm + HALO, :]


def _pool_prompt(x, g, w_group, scale, w_o):
    b, s, d = x.shape
    tm = ROW_TILE
    row = pl.BlockSpec((None, tm, d), lambda i, j: (i, j, 0))
    return pl.pallas_call(
        _pool_prompt_kernel, grid=(b, s // tm),
        in_specs=[row, _const_spec((1, d)), _const_spec(w_group.shape), _const_spec((1, d)),
                  _const_spec(w_o.shape)],
        out_specs=[row, pl.BlockSpec((None, HALO, d), lambda i, j: (i, 0, 0))],
        out_shape=[jax.ShapeDtypeStruct((b, s, d), F32), jax.ShapeDtypeStruct((b, HALO, d), F32)],
        scratch_shapes=[pltpu.VMEM((tm + HALO, d), F32)],
        compiler_params=_params(2), name="pool_prompt")(x, g, w_group, scale, w_o)


def _pool_sample_kernel(x_ref, st_ref, g_ref, wg_ref, scale_ref, wo_ref, o_ref, new_ref):
    n, d = x_ref.shape
    gd = d // len(POOL_WINDOWS)
    x = x_ref[...]
    h = _rms(x, g_ref[...])
    parts = []
    for gi, w in enumerate(POOL_WINDOWS):
        c0, c1 = gi * gd, (gi + 1) * gd
        cur = h[:, c0:c1]
        tot = cur
        for s in range(1, w):
            tot = tot + st_ref[POOL_BUF - s, :, c0:c1]
        parts.append(tot * (1.0 / w) - cur)
    o_ref[...] = _pool_project(parts, x, wg_ref, scale_ref, wo_ref)
    for r in range(POOL_BUF - 1):
        new_ref[r] = st_ref[r + 1]
    new_ref[POOL_BUF - 1] = h


def _full_spec(shape):
    return pl.BlockSpec(shape, lambda i: (0,) * len(shape))


def _pool_sample(x, st, g, w_group, scale, w_o):
    n, d = x.shape
    return pl.pallas_call(
        _pool_sample_kernel, grid=(1,),
        in_specs=[_full_spec((n, d)), _full_spec(st.shape), _full_spec((1, d)), _full_spec(w_group.shape),
                  _full_spec((1, d)), _full_spec(w_o.shape)],
        out_specs=[_full_spec((n, d)), _full_spec(st.shape)],
        out_shape=[jax.ShapeDtypeStruct((n, d), F32), jax.ShapeDtypeStruct(st.shape, F32)],
        compiler_params=_params(1), name="pool_sample")(x, st, g, w_group, scale, w_o)


def _rope_tables(pos):
    half = ROT_DIM // 2
    inv_freq = ROPE_THETA ** (-jnp.arange(0, ROT_DIM, 2, dtype=F32) / ROT_DIM)
    ang = pos.astype(F32)[:, None] * inv_freq[None, :]
    cos, sin = jnp.cos(ang), jnp.sin(ang)
    t = pos.shape[0]
    c = jnp.concatenate([cos, cos, jnp.ones((t, HEAD_DIM - ROT_DIM), F32)], axis=1)
    a = jnp.concatenate([-sin, jnp.zeros((t, HEAD_DIM - half), F32)], axis=1)
    b = jnp.concatenate([jnp.zeros((t, half), F32), sin, jnp.zeros((t, HEAD_DIM - ROT_DIM), F32)], axis=1)
    return c, a, b


def _rope_heads(x, c, a, b, n_heads):
    half = ROT_DIM // 2
    outs = []
    for hh in range(n_heads):
        xh = x[:, hh * HEAD_DIM:(hh + 1) * HEAD_DIM]
        outs.append(xh * c + pltpu.roll(xh, HEAD_DIM - half, 1) * a + pltpu.roll(xh, half, 1) * b)
    return outs


def _project_rope(x_ref, g_ref, w_ref, c_ref, a_ref, b_ref, n_rope_heads, dst_ref):
    xn = _rms(x_ref[...], g_ref[...]).astype(BF16)
    y = jnp.dot(xn, w_ref[...], preferred_element_type=F32)
    roped = _rope_heads(y, c_ref[...], a_ref[...], b_ref[...], n_rope_heads)
    for hh in range(y.shape[1] // HEAD_DIM):
        dst_ref[hh] = roped[hh] if hh < n_rope_heads else y[:, hh * HEAD_DIM:(hh + 1) * HEAD_DIM]


def _store_natural(src_ref, slot0, dst_ref):
    for hh in range(dst_ref.shape[1] // HEAD_DIM):
        dst_ref[:, hh * HEAD_DIM:(hh + 1) * HEAD_DIM] = src_ref[slot0 + hh].astype(dst_ref.dtype)


def _store_strided(src_ref, slot0, dst_ref):
    per = src_ref.shape[1] // STRIDE
    for hh in range(dst_ref.shape[2] // HEAD_DIM):
        for r in range(STRIDE):
            dst_ref[r, :, hh * HEAD_DIM:(hh + 1) * HEAD_DIM] = (
                src_ref[slot0 + hh, pl.ds(r, per, stride=STRIDE), :].astype(dst_ref.dtype))


def _kv_prompt_kernel(x_ref, g_ref, w_ref, c_ref, a_ref, b_ref,
                      k_ref, v_ref, ks_ref, vs_ref, kf_ref, vf_ref, y_ref):
    n_heads = x_ref.shape[1] // HEAD_DIM
    _project_rope(x_ref, g_ref, w_ref, c_ref, a_ref, b_ref, n_heads, y_ref)
    _store_natural(y_ref, 0, k_ref)
    _store_natural(y_ref, n_heads, v_ref)
    _store_strided(y_ref, 0, ks_ref)
    _store_strided(y_ref, n_heads, vs_ref)
    for hh in range(n_heads):
        kf_ref[:, hh, :] = y_ref[hh]
        vf_ref[:, hh, :] = y_ref[n_heads + hh]


def _kv_prompt(x, g, w_kv, tables, b, s, keep):
    n, d = x.shape
    n_heads = d // HEAD_DIM
    tm = ROW_TILE
    per_seq, kept = s // tm, keep // tm
    row = pl.BlockSpec((tm, d), lambda i: (i, 0))
    tab = pl.BlockSpec((tm, HEAD_DIM), lambda i: (i % per_seq, 0))
    strided = pl.BlockSpec((None, STRIDE, tm // STRIDE, d), lambda i: (i // per_seq, 0, i % per_seq, 0))
    kept_row = pl.BlockSpec(
        (tm, n_heads, HEAD_DIM),
        lambda i: ((i // per_seq) * kept + jnp.maximum(i % per_seq - (per_seq - kept), 0), 0, 0))
    nat_shape = jax.ShapeDtypeStruct((n, d), BF16)
    str_shape = jax.ShapeDtypeStruct((b, STRIDE, s // STRIDE, d), BF16)
    kept_shape = jax.ShapeDtypeStruct((b * keep, n_heads, HEAD_DIM), F32)
    return pl.pallas_call(
        _kv_prompt_kernel, grid=(n // tm,),
        in_specs=[row, _const_spec((1, d)), _const_spec(w_kv.shape), tab, tab, tab],
        out_specs=[row, row, strided, strided, kept_row, kept_row],
        out_shape=[nat_shape, nat_shape, str_shape, str_shape, kept_shape, kept_shape],
        scratch_shapes=[pltpu.VMEM((2 * n_heads, tm, HEAD_DIM), F32)],
        compiler_params=_params(1), name="kv_prompt")(x, g, w_kv, *tables)


def _q_prompt_kernel(x_ref, g_ref, w_ref, c_ref, a_ref, b_ref, *refs):
    out_refs, y_ref = refs[:-1], refs[-1]
    n_heads = x_ref.shape[1] // HEAD_DIM
    _project_rope(x_ref, g_ref, w_ref, c_ref, a_ref, b_ref, y_ref.shape[0], y_ref)
    for g, ((_, dil), o_ref) in enumerate(zip(DILATED_GROUPS, out_refs)):
        (_store_natural if dil == 1 else _store_strided)(y_ref, g * n_heads, o_ref)


def _q_prompt(x, g, w_q, tables, b, s):
    n, d = x.shape
    qw = w_q.shape[1]
    tm = ROW_TILE
    per_seq = s // tm
    tab = pl.BlockSpec((tm, HEAD_DIM), lambda i: (i % per_seq, 0))
    row = pl.BlockSpec((tm, d), lambda i: (i, 0))
    strided = pl.BlockSpec((None, STRIDE, tm // STRIDE, d), lambda i: (i // per_seq, 0, i % per_seq, 0))
    out_specs, out_shape = [], []
    for _, dil in DILATED_GROUPS:
        out_specs.append(row if dil == 1 else strided)
        out_shape.append(jax.ShapeDtypeStruct((n, d) if dil == 1 else (b, STRIDE, s // STRIDE, d), BF16))
    return pl.pallas_call(
        _q_prompt_kernel, grid=(n // tm,),
        in_specs=[row, _const_spec((1, d)), _const_spec(w_q.shape), tab, tab, tab],
        out_specs=out_specs, out_shape=out_shape,
        scratch_shapes=[pltpu.VMEM((qw // HEAD_DIM, tm, HEAD_DIM), F32)],
        compiler_params=_params(1), name="q_prompt")(x, g, w_q, *tables)


def _proj_sample_kernel(x_ref, g_ref, w_ref, c_ref, a_ref, b_ref, y_ref, *, n_rope_heads):
    _project_rope(x_ref, g_ref, w_ref, c_ref, a_ref, b_ref, n_rope_heads, y_ref)


def _proj_sample(x, g, w, tables, n_rope_heads):
    n, d = x.shape
    n_slots = w.shape[1] // HEAD_DIM
    return pl.pallas_call(
        functools.partial(_proj_sample_kernel, n_rope_heads=n_rope_heads), grid=(1,),
        in_specs=[_full_spec((n, d)), _full_spec((1, d)), _full_spec(w.shape)]
        + [_full_spec((n, HEAD_DIM))] * 3,
        out_specs=_full_spec((n_slots, n, HEAD_DIM)),
        out_shape=jax.ShapeDtypeStruct((n_slots, n, HEAD_DIM), F32),
        compiler_params=_params(1), name="proj_sample")(x, g, w, *tables)


def _attn_prompt_kernel(q_ref, kc_ref, kp_ref, vc_ref, vp_ref, o_ref, lse_ref, kk_ref, vv_ref):
    n = pl.program_id(2)
    nc, rows, d = q_ref.shape
    blk = ATTN_BLOCK
    sb = blk // nc
    n_heads = d // HEAD_DIM
    for c in range(nc):
        kk_ref[c, 0:sb, :] = kp_ref[c]
        kk_ref[c, sb:, :] = kc_ref[c]
        vv_ref[c, 0:sb, :] = vp_ref[c]
        vv_ref[c, sb:, :] = vc_ref[c]
    qrow = lax.broadcasted_iota(jnp.int32, (blk, 2 * blk), 0)
    kcol = lax.broadcasted_iota(jnp.int32, (blk, 2 * blk), 1)
    dist = nc * (qrow % sb - kcol % (2 * sb) + sb) + (qrow // sb - kcol // (2 * sb))
    band = (dist >= 0) & (dist <= blk)
    lane = lax.broadcasted_iota(jnp.int32, (blk, LANES), 1)
    scale = HEAD_DIM ** -0.5

    def gather(ref, lo, hi, sl):
        parts = [ref[c, lo:hi, sl] for c in range(nc)]
        return parts[0] if nc == 1 else jnp.concatenate(parts, axis=0)

    for u in range(rows // sb):
        valid = band
        if u == 0:
            valid = band & ((kcol % (2 * sb) >= sb) | (n > 0))
        lse_tile = jnp.zeros((blk, LANES), F32)
        for hh in range(n_heads):
            sl = slice(hh * HEAD_DIM, (hh + 1) * HEAD_DIM)
            qh = gather(q_ref, u * sb, (u + 1) * sb, sl)
            kh = gather(kk_ref, u * sb, (u + 2) * sb, sl)
            vh = gather(vv_ref, u * sb, (u + 2) * sb, sl)
            s = lax.dot_general(qh, kh, (((1,), (1,)), ((), ())), preferred_element_type=F32) * scale
            s = jnp.where(valid, s, -jnp.inf)
            m = jnp.max(s, axis=1, keepdims=True)
            p = jnp.exp(s - m)
            den = jnp.sum(p, axis=1, keepdims=True)
            o = jnp.dot(p.astype(BF16), vh, preferred_element_type=F32) / den
            for c in range(nc):
                o_ref[c, u * sb:(u + 1) * sb, sl] = o[c * sb:(c + 1) * sb]
            lse_tile = jnp.where(lane == hh, m + jnp.log(den), lse_tile)
        for c in range(nc):
            lse_ref[c, u * sb:(u + 1) * sb, :] = lse_tile[c * sb:(c + 1) * sb]


def _attn_prompt(q, k, v, dil):
    b, r_all, length, d = q.shape
    nc = r_all // dil
    sb = ATTN_BLOCK // nc
    rows = ATTN_QUERIES // nc
    view = lambda a: a.reshape(b, nc, dil, length, a.shape[-1])
    cur = pl.BlockSpec((None, nc, None, rows, d), lambda i, r, n: (i, 0, r, n, 0))
    prev = pl.BlockSpec((None, nc, None, sb, d),
                        lambda i, r, n: (i, 0, r, jnp.maximum(n * (rows // sb) - 1, 0), 0))
    o, lse = pl.pallas_call(
        _attn_prompt_kernel, grid=(b, dil, length // rows),
        in_specs=[cur, cur, prev, cur, prev],
        out_specs=[cur, pl.BlockSpec((None, nc, None, rows, LANES), lambda i, r, n: (i, 0, r, n, 0))],
        out_shape=[jax.ShapeDtypeStruct((b, nc, dil, length, d), F32),
                   jax.ShapeDtypeStruct((b, nc, dil, length, LANES), F32)],
        scratch_shapes=[pltpu.VMEM((nc, rows + sb, d), BF16), pltpu.VMEM((nc, rows + sb, d), BF16)],
        compiler_params=_params(3), name=f"attn_prompt_d{dil}")(view(q), view(k), view(k), view(v), view(v))
    return o.reshape(b, r_all, length, d), lse.reshape(b, r_all, length, LANES)


def _combine_kernel(*refs):
    o_refs, lse_refs = refs[:N_DIL], refs[N_DIL:2 * N_DIL]
    x_ref, wo_ref, out_ref, o_nat, lse_nat = refs[2 * N_DIL:]
    tm, d = x_ref.shape
    per = tm // STRIDE
    n_heads = d // HEAD_DIM
    outs, lses = [], []
    slot = 0
    for (_, dil), o_ref, lse_ref in zip(DILATED_GROUPS, o_refs, lse_refs):
        if dil == 1:
            outs.append(functools.partial(
                lambda hh, ref: ref[:, hh * HEAD_DIM:(hh + 1) * HEAD_DIM], ref=o_ref))
            lses.append(lse_ref[...])
        else:
            for r in range(STRIDE):
                lse_nat[slot, pl.ds(r, per, stride=STRIDE), :] = lse_ref[r]
                for hh in range(n_heads):
                    o_nat[slot * n_heads + hh, pl.ds(r, per, stride=STRIDE), :] = (
                        o_ref[r, :, hh * HEAD_DIM:(hh + 1) * HEAD_DIM])
            outs.append(functools.partial(lambda hh, s0: o_nat[s0 + hh], s0=slot * n_heads))
            lses.append(lse_nat[slot])
            slot += 1
    top = functools.reduce(jnp.maximum, lses)
    es = [jnp.exp(l - top) for l in lses]
    inv = 1.0 / functools.reduce(jnp.add, es)
    ws = [e * inv for e in es]
    cols = []
    for hh in range(n_heads):
        cols.append(functools.reduce(
            jnp.add, [w[:, hh:hh + 1] * head_of(hh) for w, head_of in zip(ws, outs)]))
    att = jnp.concatenate(cols, axis=1).astype(BF16)
    out_ref[...] = x_ref[...] + jnp.dot(att, wo_ref[...], preferred_element_type=F32)


def _combine(outs, lses, x, w_o, b, s):
    n, d = x.shape
    tm = ROW_TILE
    per_seq = s // tm

    def specs(width):
        row = pl.BlockSpec((tm, width), lambda i: (i, 0))
        strided = pl.BlockSpec((None, STRIDE, tm // STRIDE, width),
                               lambda i: (i // per_seq, 0, i % per_seq, 0))
        return [row if dil == 1 else strided for _, dil in DILATED_GROUPS]

    n_strided = sum(dil > 1 for _, dil in DILATED_GROUPS)
    row = pl.BlockSpec((tm, d), lambda i: (i, 0))
    return pl.pallas_call(
        _combine_kernel, grid=(n // tm,),
        in_specs=specs(d) + specs(LANES) + [row, _const_spec(w_o.shape)],
        out_specs=row, out_shape=jax.ShapeDtypeStruct((n, d), F32),
        scratch_shapes=[pltpu.VMEM((n_strided * (d // HEAD_DIM), tm, HEAD_DIM), F32),
                        pltpu.VMEM((n_strided, tm, LANES), F32)],
        compiler_params=_params(1), name="attn_combine")(*outs, *lses, x, w_o)


def _attn_sample_kernel(*refs):
    q_ref, kn_ref, vn_ref = refs[:3]
    kc_refs = refs[3:3 + N_DIL]
    vc_refs = refs[3 + N_DIL:3 + 2 * N_DIL]
    att_ref = refs[3 + 2 * N_DIL]
    scale = HEAD_DIM ** -0.5
    kn, vn = kn_ref[...], vn_ref[...]
    outs, lses = [], []
    for g in range(N_DIL):
        qg = q_ref[g]
        s_past = jnp.sum(kc_refs[g][...] * qg[None], axis=2, keepdims=True) * scale
        s_new = jnp.sum(kn * qg, axis=1, keepdims=True) * scale
        m = jnp.maximum(jnp.max(s_past, axis=0), s_new)
        p_past = jnp.exp(s_past - m[None])
        p_new = jnp.exp(s_new - m)
        den = jnp.sum(p_past, axis=0) + p_new
        o = jnp.sum(p_past * vc_refs[g][...], axis=0) + p_new * vn
        outs.append(o / den)
        lses.append(m + jnp.log(den))
    top = functools.reduce(jnp.maximum, lses)
    es = [jnp.exp(l - top) for l in lses]
    inv = 1.0 / functools.reduce(jnp.add, es)
    att_ref[...] = functools.reduce(jnp.add, [e * inv * o for e, o in zip(es, outs)])


def _attn_sample(q, k_new, v_new, cache_k, cache_v):
    b, w, h, e = cache_k.shape
    cache_args, cache_specs = [], []
    for cache in (cache_k, cache_v):
        for win, dil in DILATED_GROUPS:
            n_keys = win // dil
            last = w // dil // n_keys - 1
            cache_args.append(cache.reshape(b, w // dil, dil, h, e))
            cache_specs.append(pl.BlockSpec((None, n_keys, None, h, e),
                                            lambda i, last=last: (i, last, 0, 0, 0)))
    head = pl.BlockSpec((None, h, e), lambda i: (i, 0, 0))
    return pl.pallas_call(
        _attn_sample_kernel, grid=(b,),
        in_specs=[pl.BlockSpec((None, N_DIL, h, e), lambda i: (i, 0, 0, 0)), head, head] + cache_specs,
        out_specs=head, out_shape=jax.ShapeDtypeStruct((b, h, e), F32),
        compiler_params=_params(1), name="attn_sample")(q, k_new, v_new, *cache_args)


def _dense_residual_kernel(a_ref, x_ref, w_ref, o_ref):
    o_ref[...] = x_ref[...] + jnp.dot(a_ref[...].astype(BF16), w_ref[...], preferred_element_type=F32)


def _dense_residual(a, x, w):
    n, d = x.shape
    return pl.pallas_call(
        _dense_residual_kernel, grid=(1,),
        in_specs=[_full_spec(a.shape), _full_spec((n, d)), _full_spec(w.shape)], out_specs=_full_spec((n, d)),
        out_shape=jax.ShapeDtypeStruct((n, d), F32),
        compiler_params=_params(1), name="dense_residual")(a, x, w)


def _cache_roll_kernel(ck_ref, ck_next_ref, kn_ref, cv_ref, cv_next_ref, vn_ref, ko_ref, vo_ref):
    j = pl.program_id(1)
    is_last = j == pl.num_programs(1) - 1
    t = ck_ref.shape[0]
    for src, nxt, new, dst in ((ck_ref, ck_next_ref, kn_ref, ko_ref), (cv_ref, cv_next_ref, vn_ref, vo_ref)):
        dst[0:t - 1] = src[1:t]
        dst[t - 1] = jnp.where(is_last, new[0], nxt[0])


def _cache_roll(cache_k, cache_v, k_new, v_new):
    b, w, h, e = cache_k.shape
    t = CACHE_TILE
    blk = pl.BlockSpec((None, t, h, e), lambda i, j: (i, j, 0, 0))
    nxt = pl.BlockSpec((None, 1, h, e), lambda i, j: (i, jnp.minimum((j + 1) * t, w - 1), 0, 0))
    new = pl.BlockSpec((None, 1, h, e), lambda i, j: (i, 0, 0, 0))
    shape = jax.ShapeDtypeStruct(cache_k.shape, cache_k.dtype)
    return pl.pallas_call(
        _cache_roll_kernel, grid=(b, w // t),
        in_specs=[blk, nxt, new, blk, nxt, new], out_specs=[blk, blk], out_shape=[shape, shape],
        compiler_params=_params(2), name="cache_roll")(cache_k, cache_k, k_new, cache_v, cache_v, v_new)


def kernel(x_prompt, x_sample, state_pool, cache_k, cache_v, ffn1_norm, ffn1_w_in, ffn1_w_out, mix_norm,
           ffn2_norm, ffn2_w_in, ffn2_w_out, pool_w_group, pool_scale, pool_w_o, kv_norm, w_kv, attn_w_q,
           attn_w_o, final_norm):
    b, s, d = x_prompt.shape
    bs = x_sample.shape[0]
    assert x_sample.shape[1] == 1
    n_heads = d // HEAD_DIM
    w_cache = cache_k.shape[1]
    assert w_cache == min(WINDOW_MAX, PAST_LEN)
    keep = min(WINDOW_MAX, s)

    bf = lambda a: a.astype(BF16)
    f1_in, f1_out, f2_in, f2_out = bf(ffn1_w_in), bf(ffn1_w_out), bf(ffn2_w_in), bf(ffn2_w_out)
    wg, wpo, wkv, wq, wo = bf(pool_w_group[0]), bf(pool_w_o[0]), bf(w_kv), bf(attn_w_q[0]), bf(attn_w_o[0])
    vec = lambda a: a.reshape(1, d)

    tab_p = _rope_tables(jnp.arange(s, dtype=jnp.int32))
    tab_s = _rope_tables(jnp.full((bs,), PAST_LEN, dtype=jnp.int32))

    x = x_prompt.reshape(b * s, d)
    x = _ffn(x, vec(ffn1_norm[0]), f1_in[0], f1_out[0])
    x, pool_hist = _pool_prompt(x.reshape(b, s, d), vec(mix_norm[0]), wg, vec(pool_scale[0]), wpo)
    pool_p = pool_hist[:, HALO - POOL_BUF:][None]
    x = _ffn(x.reshape(b * s, d), vec(ffn2_norm[0]), f2_in[0], f2_out[0])
    k_nat, v_nat, k_str, v_str, k_keep, v_keep = _kv_prompt(x, vec(kv_norm), wkv, tab_p, b, s, keep)
    x = _ffn(x, vec(ffn1_norm[1]), f1_in[1], f1_out[1])
    qs_by_group = _q_prompt(x, vec(mix_norm[1]), wq, tab_p, b, s)
    outs, lses = [], []
    for (win, dil), q in zip(DILATED_GROUPS, qs_by_group):
        assert win // dil == ATTN_BLOCK
        if dil == 1:
            nat = lambda a: a.reshape(b, 1, s, d)
            o, lse = _attn_prompt(nat(q), nat(k_nat), nat(v_nat), 1)
            outs.append(o.reshape(b * s, d))
            lses.append(lse.reshape(b * s, LANES))
        else:
            o, lse = _attn_prompt(q, k_str, v_str, dil)
            outs.append(o)
            lses.append(lse)
    x = _combine(outs, lses, x, wo, b, s)
    y_prompt = _ffn(x, vec(ffn2_norm[1]), f2_in[1], f2_out[1], vec(final_norm)).reshape(b, s, d)
    k_p = k_keep.reshape(b, keep, n_heads, HEAD_DIM)
    v_p = v_keep.reshape(b, keep, n_heads, HEAD_DIM)

    xs = x_sample.reshape(bs, d)
    xs = _ffn(xs, vec(ffn1_norm[0]), f1_in[0], f1_out[0])
    hist = jnp.swapaxes(state_pool[0], 0, 1)
    xs, hist_new = _pool_sample(xs, hist, vec(mix_norm[0]), wg, vec(pool_scale[0]), wpo)
    pool_s = jnp.swapaxes(hist_new, 0, 1)[None]
    xs = _ffn(xs, vec(ffn2_norm[0]), f2_in[0], f2_out[0])
    kv_new = jnp.swapaxes(_proj_sample(xs, vec(kv_norm), wkv, tab_s, n_heads), 0, 1)
    k_new, v_new = kv_new[:, :n_heads], kv_new[:, n_heads:]
    xs = _ffn(xs, vec(ffn1_norm[1]), f1_in[1], f1_out[1])
    q_new = jnp.swapaxes(_proj_sample(xs, vec(mix_norm[1]), wq, tab_s, N_DIL * n_heads), 0, 1)
    att = _attn_sample(q_new.reshape(bs, N_DIL, n_heads, HEAD_DIM), k_new, v_new, cache_k, cache_v)
    xs = _dense_residual(att.reshape(bs, d), xs, wo)
    y_sample = _ffn(xs, vec(ffn2_norm[1]), f2_in[1], f2_out[1], vec(final_norm)).reshape(bs, 1, d)
    k_s, v_s = _cache_roll(cache_k, cache_v, k_new[:, None], v_new[:, None])

    return (y_prompt, y_sample, pool_p, pool_s, k_p, v_p, k_s, v_s)
```

```python
import functools

import jax
import jax.numpy as jnp
from jax import lax
from jax.experimental import pallas as pl
from jax.experimental.pallas import tpu as pltpu

F32 = jnp.float32
BF16 = jnp.bfloat16

POOL_WINDOWS = (2, 4, 8, 16)
POOL_BUF = max(POOL_WINDOWS) - 1
HEAD_DIM = 128
DILATED_GROUPS = ((128, 1), (512, 4), (2048, 16))
N_DIL = len(DILATED_GROUPS)
WINDOW_MAX = max(w for w, _ in DILATED_GROUPS)
STRIDE = max(d for _, d in DILATED_GROUPS)
ATTN_BLOCK = 128
ROT_DIM = HEAD_DIM // 4
ROPE_THETA = 500000.0
RMS_EPS = 1e-6
PAST_LEN = 16384
LOG2_E = 1.4426950408889634

V7X_VMEM_BYTES = 64 * 1024 * 1024
VMEM_LIMIT = (V7X_VMEM_BYTES * 13) // 16
LANES = 128
HALO = 16

ROW_TILE = 512
FF_CHUNK = 256
ATTN_QUERIES = 512


def _const_spec(shape):
    n = len(shape)
    return pl.BlockSpec(shape, lambda *_: (0,) * n, pipeline_mode=pl.Buffered(1))


def _params(n_axes):
    return pltpu.CompilerParams(dimension_semantics=("arbitrary",) * n_axes,
                                vmem_limit_bytes=VMEM_LIMIT)


def _rms(x, g):
    return x * lax.rsqrt(jnp.mean(x * x, axis=-1, keepdims=True) + RMS_EPS) * g


def _swiglu_half(x, g_ref, win_ref, wout_ref, h_ref):
    d_ff = wout_ref.shape[0]
    xn = _rms(x, g_ref[...]).astype(BF16)
    for c in range(0, d_ff, FF_CHUNK):
        gate = jnp.dot(xn, win_ref[:, c:c + FF_CHUNK], preferred_element_type=F32)
        up = jnp.dot(xn, win_ref[:, d_ff + c:d_ff + c + FF_CHUNK], preferred_element_type=F32)
        h_ref[:, c:c + FF_CHUNK] = (gate * jax.nn.sigmoid(gate) * up).astype(BF16)
    return x + 0.5 * jnp.dot(h_ref[...], wout_ref[...], preferred_element_type=F32)


def _ffn_kernel(x_ref, g_ref, win_ref, wout_ref, o_ref, h_ref):
    o_ref[...] = _swiglu_half(x_ref[...], g_ref, win_ref, wout_ref, h_ref)


def _ffn_final_kernel(x_ref, g_ref, win_ref, wout_ref, gf_ref, o_ref, h_ref):
    o_ref[...] = _rms(_swiglu_half(x_ref[...], g_ref, win_ref, wout_ref, h_ref), gf_ref[...])


def _ffn_roll_kernel(x_ref, g_ref, win_ref, wout_ref, src_ref, nxt_ref, new_ref, o_ref, dst_ref, h_ref,
                     *, tiles_per_seq):
    o_ref[...] = _swiglu_half(x_ref[...], g_ref, win_ref, wout_ref, h_ref)
    t = src_ref.shape[0]
    seq_end = pl.program_id(0) % tiles_per_seq == tiles_per_seq - 1
    dst_ref[0:t - 1] = src_ref[1:t]
    dst_ref[t - 1] = jnp.where(seq_end, new_ref[0], nxt_ref[0])


def _ffn(x, g, w_in, w_out, final_g=None, roll=None):
    n, d = x.shape
    d_ff = w_out.shape[0]
    tm = min(ROW_TILE, n)
    steps = n // tm
    row = pl.BlockSpec((tm, d), lambda i: (i, 0))
    in_specs = [row, _const_spec((1, d)), _const_spec(w_in.shape), _const_spec(w_out.shape)]
    args = [x, g, w_in, w_out]
    out_specs, out_shape = row, jax.ShapeDtypeStruct((n, d), F32)
    body = _ffn_kernel
    if final_g is not None:
        in_specs.append(_const_spec((1, d)))
        args.append(final_g)
        body = _ffn_final_kernel
    elif roll is not None:
        cache, new = roll
        b, w, h, e = cache.shape
        per = steps // b
        t = w // per
        blk = pl.BlockSpec((None, t, h, e), lambda i: (i // per, i % per, 0, 0))
        one = lambda row_of: pl.BlockSpec((None, 1, h, e), lambda i: (i // per, row_of(i), 0, 0))
        in_specs += [blk, one(lambda i: jnp.minimum((i % per + 1) * t, w - 1)), one(lambda i: 0)]
        args += [cache, cache, new]
        out_specs = [row, blk]
        out_shape = [out_shape, jax.ShapeDtypeStruct(cache.shape, cache.dtype)]
        body = functools.partial(_ffn_roll_kernel, tiles_per_seq=per)
    return pl.pallas_call(
        body, grid=(steps,), in_specs=in_specs, out_specs=out_specs, out_shape=out_shape,
        scratch_shapes=[pltpu.VMEM((tm, d_ff), BF16)],
        compiler_params=_params(1), name="ffn")(*args)


def _pool_project(p_groups, x, wg_ref, scale_ref, wo_ref):
    z = jnp.concatenate(
        [jnp.dot(p.astype(BF16), wg_ref[i], preferred_element_type=F32) for i, p in enumerate(p_groups)],
        axis=1)
    z = (z * scale_ref[...]).astype(BF16)
    return x + jnp.dot(z, wo_ref[...], preferred_element_type=F32)


def _pool_prompt_kernel(x_ref, g_ref, wg_ref, scale_ref, wo_ref, o_ref, state_ref, ext_ref):
    j = pl.program_id(1)
    tm, d = x_ref.shape
    gd = d // len(POOL_WINDOWS)
    x = x_ref[...]
    h = _rms(x, g_ref[...])

    @pl.when(j == 0)
    def _():
        ext_ref[0:HALO, :] = jnp.zeros((HALO, d), F32)

    @pl.when(j > 0)
    def _():
        ext_ref[0:HALO, :] = ext_ref[tm:tm + HALO, :]

    ext_ref[HALO:HALO + tm, :] = h
    pos1 = (j * tm + 1 + lax.broadcasted_iota(jnp.int32, (tm, 1), 0)).astype(F32)
    parts = []
    for gi, w in enumerate(POOL_WINDOWS):
        c0, c1 = gi * gd, (gi + 1) * gd
        cur = h[:, c0:c1]
        tot = cur
        for s in range(1, w):
            tot = tot + ext_ref[HALO - s:HALO - s + tm, c0:c1]
        inv_cnt = 1.0 / jnp.minimum(pos1, float(w))
        parts.append(tot * inv_cnt - cur)
    o_ref[...] = _pool_project(parts, x, wg_ref, scale_ref, wo_ref)
    state_ref[...] = ext_ref[tm:tm + HALO, :]


def _pool_prompt(x, g, w_group, scale, w_o):
    b, s, d = x.shape
    tm = ROW_TILE
    row = pl.BlockSpec((None, tm, d), lambda i, j: (i, j, 0))
    return pl.pallas_call(
        _pool_prompt_kernel, grid=(b, s // tm),
        in_specs=[row, _const_spec((1, d)), _const_spec(w_group.shape), _const_spec((1, d)),
                  _const_spec(w_o.shape)],
        out_specs=[row, pl.BlockSpec((None, HALO, d), lambda i, j: (i, 0, 0))],
        out_shape=[jax.ShapeDtypeStruct((b, s, d), F32), jax.ShapeDtypeStruct((b, HALO, d), F32)],
        scratch_shapes=[pltpu.VMEM((tm + HALO, d), F32)],
        compiler_params=_params(2), name="pool_prompt")(x, g, w_group, scale, w_o)


def _pool_sample_kernel(x_ref, st_ref, g_ref, wg_ref, scale_ref, wo_ref, o_ref, new_ref):
    n, d = x_ref.shape
    gd = d // len(POOL_WINDOWS)
    x = x_ref[...]
    h = _rms(x, g_ref[...])
    parts = []
    for gi, w in enumerate(POOL_WINDOWS):
        c0, c1 = gi * gd, (gi + 1) * gd
        cur = h[:, c0:c1]
        tot = cur
        for s in range(1, w):
            tot = tot + st_ref[POOL_BUF - s, :, c0:c1]
        parts.append(tot * (1.0 / w) - cur)
    o_ref[...] = _pool_project(parts, x, wg_ref, scale_ref, wo_ref)
    for r in range(POOL_BUF - 1):
        new_ref[r] = st_ref[r + 1]
    new_ref[POOL_BUF - 1] = h


def _full_spec(shape):
    return pl.BlockSpec(shape, lambda i: (0,) * len(shape))


def _pool_sample(x, st, g, w_group, scale, w_o):
    n, d = x.shape
    return pl.pallas_call(
        _pool_sample_kernel, grid=(1,),
        in_specs=[_full_spec((n, d)), _full_spec(st.shape), _full_spec((1, d)), _full_spec(w_group.shape),
                  _full_spec((1, d)), _full_spec(w_o.shape)],
        out_specs=[_full_spec((n, d)), _full_spec(st.shape)],
        out_shape=[jax.ShapeDtypeStruct((n, d), F32), jax.ShapeDtypeStruct(st.shape, F32)],
        compiler_params=_params(1), name="pool_sample")(x, st, g, w_group, scale, w_o)


def _rope_tables(pos):
    half = ROT_DIM // 2
    inv_freq = ROPE_THETA ** (-jnp.arange(0, ROT_DIM, 2, dtype=F32) / ROT_DIM)
    ang = pos.astype(F32)[:, None] * inv_freq[None, :]
    cos, sin = jnp.cos(ang), jnp.sin(ang)
    t = pos.shape[0]
    c = jnp.concatenate([cos, cos, jnp.ones((t, HEAD_DIM - ROT_DIM), F32)], axis=1)
    a = jnp.concatenate([-sin, jnp.zeros((t, HEAD_DIM - half), F32)], axis=1)
    b = jnp.concatenate([jnp.zeros((t, half), F32), sin, jnp.zeros((t, HEAD_DIM - ROT_DIM), F32)], axis=1)
    return c, a, b


def _rope_heads(x, c, a, b, n_heads):
    half = ROT_DIM // 2
    outs = []
    for hh in range(n_heads):
        xh = x[:, hh * HEAD_DIM:(hh + 1) * HEAD_DIM]
        outs.append(xh * c + pltpu.roll(xh, HEAD_DIM - half, 1) * a + pltpu.roll(xh, half, 1) * b)
    return outs


def _project_rope(x_ref, g_ref, w_ref, c_ref, a_ref, b_ref, n_rope_heads, dst_ref):
    xn = _rms(x_ref[...], g_ref[...]).astype(BF16)
    y = jnp.dot(xn, w_ref[...], preferred_element_type=F32)
    roped = _rope_heads(y, c_ref[...], a_ref[...], b_ref[...], n_rope_heads)
    for hh in range(y.shape[1] // HEAD_DIM):
        dst_ref[hh] = roped[hh] if hh < n_rope_heads else y[:, hh * HEAD_DIM:(hh + 1) * HEAD_DIM]


def _store_natural(src_ref, slot0, dst_ref):
    for hh in range(dst_ref.shape[1] // HEAD_DIM):
        dst_ref[:, hh * HEAD_DIM:(hh + 1) * HEAD_DIM] = src_ref[slot0 + hh].astype(dst_ref.dtype)


def _store_strided(src_ref, slot0, dst_ref):
    per = src_ref.shape[1] // STRIDE
    for hh in range(dst_ref.shape[2] // HEAD_DIM):
        for r in range(STRIDE):
            dst_ref[r, :, hh * HEAD_DIM:(hh + 1) * HEAD_DIM] = (
                src_ref[slot0 + hh, pl.ds(r, per, stride=STRIDE), :].astype(dst_ref.dtype))


def _kv_prompt_kernel(x_ref, g_ref, w_ref, c_ref, a_ref, b_ref,
                      k_ref, v_ref, ks_ref, vs_ref, kf_ref, vf_ref, y_ref, *, tiles_per_seq, kept_tiles):
    n_heads = x_ref.shape[1] // HEAD_DIM
    _project_rope(x_ref, g_ref, w_ref, c_ref, a_ref, b_ref, n_heads, y_ref)
    _store_natural(y_ref, 0, k_ref)
    _store_natural(y_ref, n_heads, v_ref)
    _store_strided(y_ref, 0, ks_ref)
    _store_strided(y_ref, n_heads, vs_ref)

    @pl.when(pl.program_id(0) % tiles_per_seq >= tiles_per_seq - kept_tiles)
    def _():
        for hh in range(n_heads):
            kf_ref[:, hh, :] = y_ref[hh]
            vf_ref[:, hh, :] = y_ref[n_heads + hh]


def _kv_prompt(x, g, w_kv, tables, b, s, keep):
    n, d = x.shape
    n_heads = d // HEAD_DIM
    tm = ROW_TILE
    per_seq, kept = s // tm, keep // tm
    row = pl.BlockSpec((tm, d), lambda i: (i, 0))
    tab = pl.BlockSpec((tm, HEAD_DIM), lambda i: (i % per_seq, 0))
    strided = pl.BlockSpec((None, STRIDE, tm // STRIDE, d), lambda i: (i // per_seq, 0, i % per_seq, 0))
    kept_row = pl.BlockSpec(
        (tm, n_heads, HEAD_DIM),
        lambda i: ((i // per_seq) * kept + jnp.maximum(i % per_seq - (per_seq - kept), 0), 0, 0))
    nat_shape = jax.ShapeDtypeStruct((n, d), BF16)
    str_shape = jax.ShapeDtypeStruct((b, STRIDE, s // STRIDE, d), BF16)
    kept_shape = jax.ShapeDtypeStruct((b * keep, n_heads, HEAD_DIM), F32)
    return pl.pallas_call(
        functools.partial(_kv_prompt_kernel, tiles_per_seq=per_seq, kept_tiles=kept), grid=(n // tm,),
        in_specs=[row, _const_spec((1, d)), _const_spec(w_kv.shape), tab, tab, tab],
        out_specs=[row, row, strided, strided, kept_row, kept_row],
        out_shape=[nat_shape, nat_shape, str_shape, str_shape, kept_shape, kept_shape],
        scratch_shapes=[pltpu.VMEM((2 * n_heads, tm, HEAD_DIM), F32)],
        compiler_params=_params(1), name="kv_prompt")(x, g, w_kv, *tables)


def _q_prompt_kernel(x_ref, g_ref, w_ref, c_ref, a_ref, b_ref, *refs):
    out_refs, y_ref = refs[:-1], refs[-1]
    n_heads = x_ref.shape[1] // HEAD_DIM
    _project_rope(x_ref, g_ref, w_ref, c_ref, a_ref, b_ref, y_ref.shape[0], y_ref)
    for g, ((_, dil), o_ref) in enumerate(zip(DILATED_GROUPS, out_refs)):
        (_store_natural if dil == 1 else _store_strided)(y_ref, g * n_heads, o_ref)


def _q_prompt(x, g, w_q, tables, b, s):
    n, d = x.shape
    qw = w_q.shape[1]
    tm = ROW_TILE
    per_seq = s // tm
    tab = pl.BlockSpec((tm, HEAD_DIM), lambda i: (i % per_seq, 0))
    row = pl.BlockSpec((tm, d), lambda i: (i, 0))
    strided = pl.BlockSpec((None, STRIDE, tm // STRIDE, d), lambda i: (i // per_seq, 0, i % per_seq, 0))
    out_specs, out_shape = [], []
    for _, dil in DILATED_GROUPS:
        out_specs.append(row if dil == 1 else strided)
        out_shape.append(jax.ShapeDtypeStruct((n, d) if dil == 1 else (b, STRIDE, s // STRIDE, d), BF16))
    return pl.pallas_call(
        _q_prompt_kernel, grid=(n // tm,),
        in_specs=[row, _const_spec((1, d)), _const_spec(w_q.shape), tab, tab, tab],
        out_specs=out_specs, out_shape=out_shape,
        scratch_shapes=[pltpu.VMEM((qw // HEAD_DIM, tm, HEAD_DIM), F32)],
        compiler_params=_params(1), name="q_prompt")(x, g, w_q, *tables)


def _proj_sample_kernel(x_ref, g_ref, w_ref, c_ref, a_ref, b_ref, y_ref, *, n_rope_heads):
    _project_rope(x_ref, g_ref, w_ref, c_ref, a_ref, b_ref, n_rope_heads, y_ref)


def _proj_sample(x, g, w, tables, n_rope_heads):
    n, d = x.shape
    n_slots = w.shape[1] // HEAD_DIM
    return pl.pallas_call(
        functools.partial(_proj_sample_kernel, n_rope_heads=n_rope_heads), grid=(1,),
        in_specs=[_full_spec((n, d)), _full_spec((1, d)), _full_spec(w.shape)]
        + [_full_spec((n, HEAD_DIM))] * 3,
        out_specs=_full_spec((n_slots, n, HEAD_DIM)),
        out_shape=jax.ShapeDtypeStruct((n_slots, n, HEAD_DIM), F32),
        compiler_params=_params(1), name="proj_sample")(x, g, w, *tables)


def _attn_prompt_kernel(q_ref, kc_ref, kp_ref, vc_ref, vp_ref, o_ref, lse_ref, kk_ref, vv_ref):
    n = pl.program_id(2)
    nc, rows, d = q_ref.shape
    blk = ATTN_BLOCK
    sb = blk // nc
    n_heads = d // HEAD_DIM
    for c in range(nc):
        kk_ref[c, 0:sb, :] = kp_ref[c]
        kk_ref[c, sb:, :] = kc_ref[c]
        vv_ref[c, 0:sb, :] = vp_ref[c]
        vv_ref[c, sb:, :] = vc_ref[c]
    qrow = lax.broadcasted_iota(jnp.int32, (blk, 2 * blk), 0)
    kcol = lax.broadcasted_iota(jnp.int32, (blk, 2 * blk), 1)
    dist = nc * (qrow % sb - kcol % (2 * sb) + sb) + (qrow // sb - kcol // (2 * sb))
    band = (dist >= 0) & (dist <= blk)
    lane = lax.broadcasted_iota(jnp.int32, (blk, LANES), 1)
    scale = HEAD_DIM ** -0.5

    def gather(ref, lo, hi, sl):
        parts = [ref[c, lo:hi, sl] for c in range(nc)]
        return parts[0] if nc == 1 else jnp.concatenate(parts, axis=0)

    for u in range(rows // sb):
        valid = band
        if u == 0:
            valid = band & ((kcol % (2 * sb) >= sb) | (n > 0))
        lse_tile = jnp.zeros((blk, LANES), F32)
        for hh in range(n_heads):
            sl = slice(hh * HEAD_DIM, (hh + 1) * HEAD_DIM)
            qh = gather(q_ref, u * sb, (u + 1) * sb, sl)
            kh = gather(kk_ref, u * sb, (u + 2) * sb, sl)
            vh = gather(vv_ref, u * sb, (u + 2) * sb, sl)
            s = lax.dot_general(qh, kh, (((1,), (1,)), ((), ())), preferred_element_type=F32)
            s = jnp.where(valid, s, -jnp.inf)
            m = jnp.max(s, axis=1, keepdims=True)
            p = jnp.exp2((s - m) * (scale * LOG2_E))
            den = jnp.sum(p, axis=1, keepdims=True)
            o = jnp.dot(p.astype(BF16), vh, preferred_element_type=F32) / den
            for c in range(nc):
                o_ref[c, u * sb:(u + 1) * sb, sl] = o[c * sb:(c + 1) * sb]
            lse_tile = jnp.where(lane == hh, m * scale + jnp.log(den), lse_tile)
        for c in range(nc):
            lse_ref[c, u * sb:(u + 1) * sb, :] = lse_tile[c * sb:(c + 1) * sb]


def _attn_prompt(q, k, v, dil):
    b, r_all, length, d = q.shape
    nc = r_all // dil
    sb = ATTN_BLOCK // nc
    rows = ATTN_QUERIES // nc
    view = lambda a: a.reshape(b, nc, dil, length, a.shape[-1])
    cur = pl.BlockSpec((None, nc, None, rows, d), lambda i, r, n: (i, 0, r, n, 0))
    prev = pl.BlockSpec((None, nc, None, sb, d),
                        lambda i, r, n: (i, 0, r, jnp.maximum(n * (rows // sb) - 1, 0), 0))
    o, lse = pl.pallas_call(
        _attn_prompt_kernel, grid=(b, dil, length // rows),
        in_specs=[cur, cur, prev, cur, prev],
        out_specs=[cur, pl.BlockSpec((None, nc, None, rows, LANES), lambda i, r, n: (i, 0, r, n, 0))],
        out_shape=[jax.ShapeDtypeStruct((b, nc, dil, length, d), F32),
                   jax.ShapeDtypeStruct((b, nc, dil, length, LANES), F32)],
        scratch_shapes=[pltpu.VMEM((nc, rows + sb, d), BF16), pltpu.VMEM((nc, rows + sb, d), BF16)],
        compiler_params=_params(3), name=f"attn_prompt_d{dil}")(view(q), view(k), view(k), view(v), view(v))
    return o.reshape(b, r_all, length, d), lse.reshape(b, r_all, length, LANES)


def _combine_kernel(*refs):
    o_refs, lse_refs = refs[:N_DIL], refs[N_DIL:2 * N_DIL]
    x_ref, wo_ref, out_ref, o_nat, lse_nat = refs[2 * N_DIL:]
    tm, d = x_ref.shape
    per = tm // STRIDE
    n_heads = d // HEAD_DIM
    outs, lses = [], []
    slot = 0
    for (_, dil), o_ref, lse_ref in zip(DILATED_GROUPS, o_refs, lse_refs):
        if dil == 1:
            outs.append(functools.partial(
                lambda hh, ref: ref[:, hh * HEAD_DIM:(hh + 1) * HEAD_DIM], ref=o_ref))
            lses.append(lse_ref[...])
        else:
            for r in range(STRIDE):
                lse_nat[slot, pl.ds(r, per, stride=STRIDE), :] = lse_ref[r]
                for hh in range(n_heads):
                    o_nat[slot * n_heads + hh, pl.ds(r, per, stride=STRIDE), :] = (
                        o_ref[r, :, hh * HEAD_DIM:(hh + 1) * HEAD_DIM])
            outs.append(functools.partial(lambda hh, s0: o_nat[s0 + hh], s0=slot * n_heads))
            lses.append(lse_nat[slot])
            slot += 1
    top = functools.reduce(jnp.maximum, lses)
    es = [jnp.exp(l - top) for l in lses]
    inv = 1.0 / functools.reduce(jnp.add, es)
    ws = [e * inv for e in es]
    cols = []
    for hh in range(n_heads):
        cols.append(functools.reduce(
            jnp.add, [w[:, hh:hh + 1] * head_of(hh) for w, head_of in zip(ws, outs)]))
    att = jnp.concatenate(cols, axis=1).astype(BF16)
    out_ref[...] = x_ref[...] + jnp.dot(att, wo_ref[...], preferred_element_type=F32)


def _combine(outs, lses, x, w_o, b, s):
    n, d = x.shape
    tm = ROW_TILE
    per_seq = s // tm

    def specs(width):
        row = pl.BlockSpec((tm, width), lambda i: (i, 0))
        strided = pl.BlockSpec((None, STRIDE, tm // STRIDE, width),
                               lambda i: (i // per_seq, 0, i % per_seq, 0))
        return [row if dil == 1 else strided for _, dil in DILATED_GROUPS]

    n_strided = sum(dil > 1 for _, dil in DILATED_GROUPS)
    row = pl.BlockSpec((tm, d), lambda i: (i, 0))
    return pl.pallas_call(
        _combine_kernel, grid=(n // tm,),
        in_specs=specs(d) + specs(LANES) + [row, _const_spec(w_o.shape)],
        out_specs=row, out_shape=jax.ShapeDtypeStruct((n, d), F32),
        scratch_shapes=[pltpu.VMEM((n_strided * (d // HEAD_DIM), tm, HEAD_DIM), F32),
                        pltpu.VMEM((n_strided, tm, LANES), F32)],
        compiler_params=_params(1), name="attn_combine")(*outs, *lses, x, w_o)


def _attn_sample_kernel(*refs):
    q_ref, kn_ref, vn_ref = refs[:3]
    kc_refs = refs[3:3 + N_DIL]
    vc_refs = refs[3 + N_DIL:3 + 2 * N_DIL]
    att_ref = refs[3 + 2 * N_DIL]
    scale = HEAD_DIM ** -0.5
    kn, vn = kn_ref[...], vn_ref[...]
    outs, lses = [], []
    for g in range(N_DIL):
        qg = q_ref[g]
        s_past = jnp.sum(kc_refs[g][...] * qg[None], axis=2, keepdims=True) * scale
        s_new = jnp.sum(kn * qg, axis=1, keepdims=True) * scale
        m = jnp.maximum(jnp.max(s_past, axis=0), s_new)
        p_past = jnp.exp(s_past - m[None])
        p_new = jnp.exp(s_new - m)
        den = jnp.sum(p_past, axis=0) + p_new
        o = jnp.sum(p_past * vc_refs[g][...], axis=0) + p_new * vn
        outs.append(o / den)
        lses.append(m + jnp.log(den))
    top = functools.reduce(jnp.maximum, lses)
    es = [jnp.exp(l - top) for l in lses]
    inv = 1.0 / functools.reduce(jnp.add, es)
    att_ref[...] = functools.reduce(jnp.add, [e * inv * o for e, o in zip(es, outs)])


def _attn_sample(q, k_new, v_new, cache_k, cache_v):
    b, w, h, e = cache_k.shape
    cache_args, cache_specs = [], []
    for cache in (cache_k, cache_v):
        for win, dil in DILATED_GROUPS:
            n_keys = win // dil
            last = w // dil // n_keys - 1
            cache_args.append(cache.reshape(b, w // dil, dil, h, e))
            cache_specs.append(pl.BlockSpec((None, n_keys, None, h, e),
                                            lambda i, last=last: (i, last, 0, 0, 0)))
    head = pl.BlockSpec((None, h, e), lambda i: (i, 0, 0))
    return pl.pallas_call(
        _attn_sample_kernel, grid=(b,),
        in_specs=[pl.BlockSpec((None, N_DIL, h, e), lambda i: (i, 0, 0, 0)), head, head] + cache_specs,
        out_specs=head, out_shape=jax.ShapeDtypeStruct((b, h, e), F32),
        compiler_params=_params(1), name="attn_sample")(q, k_new, v_new, *cache_args)


def _dense_residual_kernel(a_ref, x_ref, w_ref, o_ref):
    o_ref[...] = x_ref[...] + jnp.dot(a_ref[...].astype(BF16), w_ref[...], preferred_element_type=F32)


def _dense_residual(a, x, w):
    n, d = x.shape
    return pl.pallas_call(
        _dense_residual_kernel, grid=(1,),
        in_specs=[_full_spec(a.shape), _full_spec((n, d)), _full_spec(w.shape)], out_specs=_full_spec((n, d)),
        out_shape=jax.ShapeDtypeStruct((n, d), F32),
        compiler_params=_params(1), name="dense_residual")(a, x, w)


def kernel(x_prompt, x_sample, state_pool, cache_k, cache_v, ffn1_norm, ffn1_w_in, ffn1_w_out, mix_norm,
           ffn2_norm, ffn2_w_in, ffn2_w_out, pool_w_group, pool_scale, pool_w_o, kv_norm, w_kv, attn_w_q,
           attn_w_o, final_norm):
    b, s, d = x_prompt.shape
    bs = x_sample.shape[0]
    assert x_sample.shape[1] == 1
    n_heads = d // HEAD_DIM
    w_cache = cache_k.shape[1]
    assert w_cache == min(WINDOW_MAX, PAST_LEN)
    keep = min(WINDOW_MAX, s)

    bf = lambda a: a.astype(BF16)
    f1_in, f1_out, f2_in, f2_out = bf(ffn1_w_in), bf(ffn1_w_out), bf(ffn2_w_in), bf(ffn2_w_out)
    wg, wpo, wkv, wq, wo = bf(pool_w_group[0]), bf(pool_w_o[0]), bf(w_kv), bf(attn_w_q[0]), bf(attn_w_o[0])
    vec = lambda a: a.reshape(1, d)

    tab_p = _rope_tables(jnp.arange(s, dtype=jnp.int32))
    tab_s = _rope_tables(jnp.full((bs,), PAST_LEN, dtype=jnp.int32))

    xs = x_sample.reshape(bs, d)
    xs = _ffn(xs, vec(ffn1_norm[0]), f1_in[0], f1_out[0])
    hist = jnp.swapaxes(state_pool[0], 0, 1)
    xs, hist_new = _pool_sample(xs, hist, vec(mix_norm[0]), wg, vec(pool_scale[0]), wpo)
    pool_s = jnp.swapaxes(hist_new, 0, 1)[None]
    xs = _ffn(xs, vec(ffn2_norm[0]), f2_in[0], f2_out[0])
    kv_new = jnp.swapaxes(_proj_sample(xs, vec(kv_norm), wkv, tab_s, n_heads), 0, 1)
    k_new, v_new = kv_new[:, :n_heads], kv_new[:, n_heads:]
    xs = _ffn(xs, vec(ffn1_norm[1]), f1_in[1], f1_out[1])
    q_new = jnp.swapaxes(_proj_sample(xs, vec(mix_norm[1]), wq, tab_s, N_DIL * n_heads), 0, 1)
    att = _attn_sample(q_new.reshape(bs, N_DIL, n_heads, HEAD_DIM), k_new, v_new, cache_k, cache_v)
    xs = _dense_residual(att.reshape(bs, d), xs, wo)
    y_sample = _ffn(xs, vec(ffn2_norm[1]), f2_in[1], f2_out[1], vec(final_norm)).reshape(bs, 1, d)

    x = x_prompt.reshape(b * s, d)
    x, k_s = _ffn(x, vec(ffn1_norm[0]), f1_in[0], f1_out[0], roll=(cache_k, k_new[:, None]))
    x, pool_hist = _pool_prompt(x.reshape(b, s, d), vec(mix_norm[0]), wg, vec(pool_scale[0]), wpo)
    pool_p = pool_hist[:, HALO - POOL_BUF:][None]
    x, v_s = _ffn(x.reshape(b * s, d), vec(ffn2_norm[0]), f2_in[0], f2_out[0], roll=(cache_v, v_new[:, None]))
    k_nat, v_nat, k_str, v_str, k_keep, v_keep = _kv_prompt(x, vec(kv_norm), wkv, tab_p, b, s, keep)
    x = _ffn(x, vec(ffn1_norm[1]), f1_in[1], f1_out[1])
    qs_by_group = _q_prompt(x, vec(mix_norm[1]), wq, tab_p, b, s)
    outs, lses = [], []
    for (win, dil), q in zip(DILATED_GROUPS, qs_by_group):
        assert win // dil == ATTN_BLOCK
        if dil == 1:
            nat = lambda a: a.reshape(b, 1, s, d)
            o, lse = _attn_prompt(nat(q), nat(k_nat), nat(v_nat), 1)
            outs.append(o.reshape(b * s, d))
            lses.append(lse.reshape(b * s, LANES))
        else:
            o, lse = _attn_prompt(q, k_str, v_str, dil)
            outs.append(o)
            lses.append(lse)
    x = _combine(outs, lses, x, wo, b, s)
    y_prompt = _ffn(x, vec(ffn2_norm[1]), f2_in[1], f2_out[1], vec(final_norm)).reshape(b, s, d)
    k_p = k_keep.reshape(b, keep, n_heads, HEAD_DIM)
    v_p = v_keep.reshape(b, keep, n_heads, HEAD_DIM)

    return (y_prompt, y_sample, pool_p, pool_s, k_p, v_p, k_s, v_s)
```

```python
import functools

import jax
import jax.numpy as jnp
from jax import lax
from jax.experimental import pallas as pl
from jax.experimental.pallas import tpu as pltpu

F32 = jnp.float32
BF16 = jnp.bfloat16

POOL_WINDOWS = (2, 4, 8, 16)
POOL_BUF = max(POOL_WINDOWS) - 1
HEAD_DIM = 128
DILATED_GROUPS = ((128, 1), (512, 4), (2048, 16))
N_DIL = len(DILATED_GROUPS)
WINDOW_MAX = max(w for w, _ in DILATED_GROUPS)
STRIDE = max(d for _, d in DILATED_GROUPS)
ATTN_BLOCK = 128
ROT_DIM = HEAD_DIM // 4
ROPE_THETA = 500000.0
RMS_EPS = 1e-6
PAST_LEN = 16384
LOG2_E = 1.4426950408889634

V7X_VMEM_BYTES = 64 * 1024 * 1024
VMEM_LIMIT = (V7X_VMEM_BYTES * 7) // 8
LANES = 128
MXU_WIDTH = 256
HALO = 16

ROW_TILE = 256
ATTN_QUERIES = 512


def _const_spec(shape):
    n = len(shape)
    return pl.BlockSpec(shape, lambda *_: (0,) * n, pipeline_mode=pl.Buffered(1))


def _full_spec(shape):
    return pl.BlockSpec(shape, lambda i: (0,) * len(shape))


def _params(n_axes):
    return pltpu.CompilerParams(dimension_semantics=("arbitrary",) * n_axes,
                                vmem_limit_bytes=VMEM_LIMIT)


def _rms(x, g):
    return x * lax.rsqrt(jnp.mean(x * x, axis=-1, keepdims=True) + RMS_EPS) * g


def _col_chunks(width):
    return [(c, c + MXU_WIDTH) for c in range(0, width, MXU_WIDTH)]


def _swiglu_half(src_ref, g_ref, win_ref, wout_ref, h_ref, xn_ref, emit, side_jobs=()):
    d_ff, d = wout_ref.shape
    ff_chunks, out_chunks = _col_chunks(d_ff), _col_chunks(d)
    n_seg = len(ff_chunks) + len(out_chunks)
    jobs = list(side_jobs)
    done = [0, 0]

    def end_segment():
        done[0] += 1
        while done[1] < len(jobs) and done[1] * n_seg < done[0] * len(jobs):
            jobs[done[1]]()
            done[1] += 1

    xn_ref[...] = _rms(src_ref[...], g_ref[...]).astype(BF16)
    for c0, c1 in ff_chunks:
        gate = jnp.dot(xn_ref[...], win_ref[:, c0:c1], preferred_element_type=F32)
        up = jnp.dot(xn_ref[...], win_ref[:, d_ff + c0:d_ff + c1], preferred_element_type=F32)
        h_ref[:, c0:c1] = (gate * jax.nn.sigmoid(gate) * up).astype(BF16)
        end_segment()
    for c0, c1 in out_chunks:
        emit(c0, c1, src_ref[:, c0:c1] + 0.5 * jnp.dot(h_ref[...], wout_ref[:, c0:c1],
                                                        preferred_element_type=F32))
        end_segment()


def _swiglu_scratch(rows, d, d_ff):
    return [pltpu.VMEM((rows, d_ff), BF16), pltpu.VMEM((rows, d), BF16)]


def _pool_trailing(ext_ref, row0, rows, j, gi, w, gd):
    c0, c1 = gi * gd, (gi + 1) * gd
    cur = ext_ref[row0:row0 + rows, c0:c1]
    tot = cur
    for s in range(1, w):
        tot = tot + ext_ref[row0 - s:row0 - s + rows, c0:c1]
    if j is None:
        return tot * (1.0 / w) - cur
    pos1 = (j * rows + 1 + lax.broadcasted_iota(jnp.int32, (rows, 1), 0)).astype(F32)
    return tot * (1.0 / jnp.minimum(pos1, float(w))) - cur


def _pool_jobs(src_ref, j, g_ref, wg_ref, scale_ref, wo_ref, ext_ref, z_ref, emit, state_ref):
    tm, d = src_ref.shape
    gd = d // len(POOL_WINDOWS)

    def normalise():
        h = _rms(src_ref[...], g_ref[...])
        ext_ref[0:HALO, :] = jnp.where(j == 0, 0.0, ext_ref[tm:tm + HALO, :])
        ext_ref[HALO:HALO + tm, :] = h
        state_ref[...] = h[tm - HALO:, :]

    def group(gi, w):
        p = _pool_trailing(ext_ref, HALO, tm, j, gi, w, gd).astype(BF16)
        z = jnp.dot(p, wg_ref[gi], preferred_element_type=F32) * scale_ref[:, gi * gd:(gi + 1) * gd]
        z_ref[:, gi * gd:(gi + 1) * gd] = z.astype(BF16)

    def project(c0, c1):
        emit(c0, c1, src_ref[:, c0:c1] + jnp.dot(z_ref[...], wo_ref[:, c0:c1], preferred_element_type=F32))

    return ([normalise] + [functools.partial(group, gi, w) for gi, w in enumerate(POOL_WINDOWS)]
            + [functools.partial(project, c0, c1) for c0, c1 in _col_chunks(d)])


def _rope_tables(pos):
    half = ROT_DIM // 2
    inv_freq = ROPE_THETA ** (-jnp.arange(0, ROT_DIM, 2, dtype=F32) / ROT_DIM)
    ang = pos.astype(F32)[:, None] * inv_freq[None, :]
    cos, sin = jnp.cos(ang), jnp.sin(ang)
    t = pos.shape[0]
    c = jnp.concatenate([cos, cos, jnp.ones((t, HEAD_DIM - ROT_DIM), F32)], axis=1)
    a = jnp.concatenate([-sin, jnp.zeros((t, HEAD_DIM - half), F32)], axis=1)
    b = jnp.concatenate([jnp.zeros((t, half), F32), sin, jnp.zeros((t, HEAD_DIM - ROT_DIM), F32)], axis=1)
    return c, a, b


def _projection_jobs(xn_ref, w_ref, col0, table_refs, y_ref, store_head):
    n_slots = y_ref.shape[0]
    half = ROT_DIM // 2
    per_dot = MXU_WIDTH // HEAD_DIM

    def project(h0):
        cols = slice(col0 + h0 * HEAD_DIM, col0 + (h0 + per_dot) * HEAD_DIM)
        y = jnp.dot(xn_ref[...], w_ref[:, cols], preferred_element_type=F32)
        for k in range(per_dot):
            y_ref[h0 + k] = y[:, k * HEAD_DIM:(k + 1) * HEAD_DIM]

    def finish(hh):
        if table_refs is not None:
            c, a, b = (t[...] for t in table_refs)
            yh = y_ref[hh]
            y_ref[hh] = yh * c + pltpu.roll(yh, HEAD_DIM - half, 1) * a + pltpu.roll(yh, half, 1) * b
        store_head(hh)

    jobs = []
    for h0 in range(0, n_slots, per_dot):
        jobs.append(functools.partial(project, h0))
        jobs += [functools.partial(finish, h0 + k) for k in range(per_dot)]
    return jobs


def _store_head_natural(src_ref, hh, dst_ref):
    dst_ref[:, hh * HEAD_DIM:(hh + 1) * HEAD_DIM] = src_ref[hh].astype(dst_ref.dtype)


def _store_head_strided(src_ref, hh, dst_ref):
    per = src_ref.shape[1] // STRIDE
    for r in range(STRIDE):
        dst_ref[r, :, hh * HEAD_DIM:(hh + 1) * HEAD_DIM] = (
            src_ref[hh, pl.ds(r, per, stride=STRIDE), :].astype(dst_ref.dtype))


def _roll_tile(seq_end, src_ref, nxt_ref, new_ref, dst_ref):
    t = src_ref.shape[0]
    dst_ref[0:t - 1] = src_ref[1:t]
    dst_ref[t - 1] = jnp.where(seq_end, new_ref[0], nxt_ref[0])


def _combine_jobs(o_refs, lse_refs, x_ref, wo_ref, o_nat, lse_nat, w_ref, att_ref, emit):
    tm, d = x_ref.shape
    per = tm // STRIDE
    n_heads = d // HEAD_DIM
    heads, lses = [], []
    jobs = []
    slot = 0
    for (_, dil), o_ref, lse_ref in zip(DILATED_GROUPS, o_refs, lse_refs):
        if dil == 1:
            heads.append(functools.partial(lambda hh, ref: ref[:, hh * HEAD_DIM:(hh + 1) * HEAD_DIM], ref=o_ref))
            lses.append(functools.partial(lambda ref: ref[...], lse_ref))
            continue

        def to_natural(r, o_ref=o_ref, lse_ref=lse_ref, slot=slot):
            lse_nat[slot, pl.ds(r, per, stride=STRIDE), :] = lse_ref[r]
            for hh in range(n_heads):
                o_nat[slot * n_heads + hh, pl.ds(r, per, stride=STRIDE), :] = (
                    o_ref[r, :, hh * HEAD_DIM:(hh + 1) * HEAD_DIM])

        jobs += [functools.partial(to_natural, r) for r in range(STRIDE)]
        heads.append(functools.partial(lambda hh, s0: o_nat[s0 + hh], s0=slot * n_heads))
        lses.append(functools.partial(lambda s: lse_nat[s], slot))
        slot += 1

    def weights():
        ls = [l() for l in lses]
        top = functools.reduce(jnp.maximum, ls)
        es = [jnp.exp(l - top) for l in ls]
        inv = 1.0 / functools.reduce(jnp.add, es)
        for g, e in enumerate(es):
            w_ref[g] = e * inv

    def mix_head(hh):
        att = functools.reduce(jnp.add, [w_ref[g][:, hh:hh + 1] * heads[g](hh) for g in range(N_DIL)])
        att_ref[:, hh * HEAD_DIM:(hh + 1) * HEAD_DIM] = att.astype(BF16)

    def project(c0, c1):
        emit(c0, c1, x_ref[:, c0:c1] + jnp.dot(att_ref[...], wo_ref[:, c0:c1], preferred_element_type=F32))

    return (jobs + [weights] + [functools.partial(mix_head, hh) for hh in range(n_heads)]
            + [functools.partial(project, c0, c1) for c0, c1 in _col_chunks(d)])


class _Tiling:
    def __init__(self, n_rows, seq):
        self.tm = ROW_TILE
        self.n_tiles = n_rows // ROW_TILE
        self.per_seq = seq // ROW_TILE
        self.steps = self.n_tiles + 1

    def lead(self, i):
        return jnp.minimum(i, self.n_tiles - 1)

    def trail(self, i):
        return jnp.maximum(i - 1, 0)

    def row(self, width, tile_of):
        return pl.BlockSpec((self.tm, width), lambda i: (tile_of(i), 0))

    def strided(self, width, tile_of):
        return pl.BlockSpec((None, STRIDE, self.tm // STRIDE, width),
                            lambda i: (tile_of(i) // self.per_seq, 0, tile_of(i) % self.per_seq, 0))

    def table(self, tile_of):
        return pl.BlockSpec((self.tm, HEAD_DIM), lambda i: (tile_of(i) % self.per_seq, 0))


def _roll_operands(cache, new, tl):
    b, w, h, e = cache.shape
    per = tl.n_tiles // b
    t = w // per
    blk = pl.BlockSpec((None, t, h, e), lambda i: (tl.lead(i) // per, tl.lead(i) % per, 0, 0))
    nxt = pl.BlockSpec((None, 1, h, e),
                       lambda i: (tl.lead(i) // per, jnp.minimum((tl.lead(i) % per + 1) * t, w - 1), 0, 0))
    cur_new = pl.BlockSpec((None, 1, h, e), lambda i: (tl.lead(i) // per, 0, 0, 0))
    return per, [blk, nxt, cur_new], [cache, cache, new], blk


def _zero_on_first_step(i, *refs):
    @pl.when(i == 0)
    def _():
        for ref in refs:
            ref[...] = jnp.zeros(ref.shape, ref.dtype)


def _emit_to(*refs):
    def emit(c0, c1, value):
        for ref in refs:
            ref[:, c0:c1] = value
    return emit


def _layer0_head_kernel(x_ref, g1_ref, win_ref, wout_ref, gm_ref, wg_ref, scale_ref, wo_ref,
                        src_ref, nxt_ref, new_ref, o_ref, state_ref, dst_ref,
                        h_ref, xn_ref, xs_ref, ext_ref, z_ref, *, tl, roll_per_seq):
    i = pl.program_id(0)
    _zero_on_first_step(i, xs_ref, ext_ref)
    jobs = _pool_jobs(xs_ref.at[(i + 1) % 2], tl.trail(i) % tl.per_seq, gm_ref, wg_ref, scale_ref, wo_ref,
                      ext_ref, z_ref, _emit_to(o_ref), state_ref)
    jobs.insert(len(jobs) // 2, functools.partial(
        _roll_tile, tl.lead(i) % roll_per_seq == roll_per_seq - 1, src_ref, nxt_ref, new_ref, dst_ref))
    _swiglu_half(x_ref, g1_ref, win_ref, wout_ref, h_ref, xn_ref, _emit_to(xs_ref.at[i % 2]), jobs)


def _layer0_head(x, seq, ffn, g_mix, w_group, scale, w_o, cache, new):
    n, d = x.shape
    tl = _Tiling(n, seq)
    g1, w_in, w_out = ffn
    roll_per_seq, roll_specs, roll_args, roll_out = _roll_operands(cache, new, tl)
    return pl.pallas_call(
        functools.partial(_layer0_head_kernel, tl=tl, roll_per_seq=roll_per_seq), grid=(tl.steps,),
        in_specs=[tl.row(d, tl.lead), _const_spec((1, d)), _const_spec(w_in.shape), _const_spec(w_out.shape),
                  _const_spec((1, d)), _const_spec(w_group.shape), _const_spec((1, d)), _const_spec(w_o.shape)]
        + roll_specs,
        out_specs=[tl.row(d, tl.trail),
                   pl.BlockSpec((None, HALO, d), lambda i: (tl.trail(i) // tl.per_seq, 0, 0)), roll_out],
        out_shape=[jax.ShapeDtypeStruct((n, d), F32), jax.ShapeDtypeStruct((n // seq, HALO, d), F32),
                   jax.ShapeDtypeStruct(cache.shape, cache.dtype)],
        scratch_shapes=_swiglu_scratch(tl.tm, d, w_out.shape[0])
        + [pltpu.VMEM((2, tl.tm, d), F32), pltpu.VMEM((tl.tm + HALO, d), F32), pltpu.VMEM((tl.tm, d), BF16)],
        compiler_params=_params(1), name="layer0_head")(x, g1, w_in, w_out, g_mix, w_group, scale, w_o, *roll_args)


def _layer0_tail_kernel(x_ref, g2_ref, win_ref, wout_ref, gkv_ref, wkv_ref, c_ref, a_ref, b_ref,
                        src_ref, nxt_ref, new_ref,
                        o_ref, k_ref, v_ref, ks_ref, vs_ref, kf_ref, vf_ref, dst_ref,
                        h_ref, xn_ref, xs_ref, xkv_ref, yk_ref, yv_ref, *, tl, roll_per_seq, kept_tiles):
    i = pl.program_id(0)
    _zero_on_first_step(i, xs_ref)
    d = x_ref.shape[1]

    def normalise():
        xkv_ref[...] = _rms(xs_ref[(i + 1) % 2], gkv_ref[...]).astype(BF16)

    def store_both(y_ref, nat_ref, str_ref, hh):
        _store_head_natural(y_ref, hh, nat_ref)
        _store_head_strided(y_ref, hh, str_ref)

    jobs = [normalise]
    jobs += _projection_jobs(xkv_ref, wkv_ref, 0, (c_ref, a_ref, b_ref), yk_ref,
                             functools.partial(store_both, yk_ref, k_ref, ks_ref))
    jobs += _projection_jobs(xkv_ref, wkv_ref, d, None, yv_ref,
                             functools.partial(store_both, yv_ref, v_ref, vs_ref))
    jobs.insert(len(jobs) // 2, functools.partial(
        _roll_tile, tl.lead(i) % roll_per_seq == roll_per_seq - 1, src_ref, nxt_ref, new_ref, dst_ref))
    _swiglu_half(x_ref, g2_ref, win_ref, wout_ref, h_ref, xn_ref, _emit_to(o_ref, xs_ref.at[i % 2]), jobs)

    @pl.when(tl.trail(i) % tl.per_seq >= tl.per_seq - kept_tiles)
    def _():
        for hh in range(yk_ref.shape[0]):
            kf_ref[:, hh, :] = yk_ref[hh]
            vf_ref[:, hh, :] = yv_ref[hh]


def _layer0_tail(x, seq, keep, ffn, g_kv, w_kv, tables, cache, new):
    n, d = x.shape
    b = n // seq
    n_heads = d // HEAD_DIM
    tl = _Tiling(n, seq)
    g2, w_in, w_out = ffn
    kept = keep // tl.tm
    roll_per_seq, roll_specs, roll_args, roll_out = _roll_operands(cache, new, tl)

    def kept_block(i):
        t = tl.trail(i)
        return ((t // tl.per_seq) * kept + jnp.maximum(t % tl.per_seq - (tl.per_seq - kept), 0), 0, 0)

    kept_spec = pl.BlockSpec((tl.tm, n_heads, HEAD_DIM), kept_block)
    nat_shape = jax.ShapeDtypeStruct((n, d), BF16)
    str_shape = jax.ShapeDtypeStruct((b, STRIDE, seq // STRIDE, d), BF16)
    kept_shape = jax.ShapeDtypeStruct((b * keep, n_heads, HEAD_DIM), F32)
    head_scratch = pltpu.VMEM((n_heads, tl.tm, HEAD_DIM), F32)
    return pl.pallas_call(
        functools.partial(_layer0_tail_kernel, tl=tl, roll_per_seq=roll_per_seq, kept_tiles=kept),
        grid=(tl.steps,),
        in_specs=[tl.row(d, tl.lead), _const_spec((1, d)), _const_spec(w_in.shape), _const_spec(w_out.shape),
                  _const_spec((1, d)), _const_spec(w_kv.shape)] + [tl.table(tl.trail)] * 3 + roll_specs,
        out_specs=[tl.row(d, tl.lead), tl.row(d, tl.trail), tl.row(d, tl.trail),
                   tl.strided(d, tl.trail), tl.strided(d, tl.trail), kept_spec, kept_spec, roll_out],
        out_shape=[jax.ShapeDtypeStruct((n, d), F32), nat_shape, nat_shape, str_shape, str_shape,
                   kept_shape, kept_shape, jax.ShapeDtypeStruct(cache.shape, cache.dtype)],
        scratch_shapes=_swiglu_scratch(tl.tm, d, w_out.shape[0])
        + [pltpu.VMEM((2, tl.tm, d), F32), pltpu.VMEM((tl.tm, d), BF16), head_scratch, head_scratch],
        compiler_params=_params(1), name="layer0_tail")(x, g2, w_in, w_out, g_kv, w_kv, *tables, *roll_args)


def _layer1_head_kernel(x_ref, g1_ref, win_ref, wout_ref, gm_ref, wq_ref, c_ref, a_ref, b_ref, o_ref, *refs, tl):
    q_refs, (h_ref, xn_ref, xs_ref, xq_ref, y_ref) = refs[:N_DIL], refs[N_DIL:]
    i = pl.program_id(0)
    _zero_on_first_step(i, xs_ref)
    d = x_ref.shape[1]
    n_heads = d // HEAD_DIM

    def normalise():
        xq_ref[...] = _rms(xs_ref[(i + 1) % 2], gm_ref[...]).astype(BF16)

    jobs = [normalise]
    for g, ((_, dil), q_ref) in enumerate(zip(DILATED_GROUPS, q_refs)):
        yg_ref = y_ref.at[g * n_heads:(g + 1) * n_heads]
        store = _store_head_natural if dil == 1 else _store_head_strided
        jobs += _projection_jobs(xq_ref, wq_ref, g * d, (c_ref, a_ref, b_ref), yg_ref,
                                 functools.partial(lambda hh, st, src, dst: st(src, hh, dst),
                                                   st=store, src=yg_ref, dst=q_ref))
    _swiglu_half(x_ref, g1_ref, win_ref, wout_ref, h_ref, xn_ref, _emit_to(o_ref, xs_ref.at[i % 2]), jobs)


def _layer1_head(x, seq, ffn, g_mix, w_q, tables):
    n, d = x.shape
    b = n // seq
    tl = _Tiling(n, seq)
    g1, w_in, w_out = ffn
    q_specs, q_shapes = [], []
    for _, dil in DILATED_GROUPS:
        q_specs.append(tl.row(d, tl.trail) if dil == 1 else tl.strided(d, tl.trail))
        q_shapes.append(jax.ShapeDtypeStruct((n, d) if dil == 1 else (b, STRIDE, seq // STRIDE, d), BF16))
    return pl.pallas_call(
        functools.partial(_layer1_head_kernel, tl=tl), grid=(tl.steps,),
        in_specs=[tl.row(d, tl.lead), _const_spec((1, d)), _const_spec(w_in.shape), _const_spec(w_out.shape),
                  _const_spec((1, d)), _const_spec(w_q.shape)] + [tl.table(tl.trail)] * 3,
        out_specs=[tl.row(d, tl.lead)] + q_specs,
        out_shape=[jax.ShapeDtypeStruct((n, d), F32)] + q_shapes,
        scratch_shapes=_swiglu_scratch(tl.tm, d, w_out.shape[0])
        + [pltpu.VMEM((2, tl.tm, d), F32), pltpu.VMEM((tl.tm, d), BF16),
           pltpu.VMEM((w_q.shape[1] // HEAD_DIM, tl.tm, HEAD_DIM), F32)],
        compiler_params=_params(1), name="layer1_head")(x, g1, w_in, w_out, g_mix, w_q, *tables)


def _layer1_tail_kernel(*refs):
    o_refs, lse_refs = refs[:N_DIL], refs[N_DIL:2 * N_DIL]
    (x_ref, wo_ref, g2_ref, win_ref, wout_ref, gf_ref, out_ref,
     h_ref, xn_ref, xs_ref, y_ref, o_nat, lse_nat, w_ref, att_ref) = refs[2 * N_DIL:]
    i = pl.program_id(0)
    _zero_on_first_step(i, xs_ref)
    jobs = _combine_jobs(o_refs, lse_refs, x_ref, wo_ref, o_nat, lse_nat, w_ref, att_ref,
                         _emit_to(xs_ref.at[i % 2]))
    _swiglu_half(xs_ref.at[(i + 1) % 2], g2_ref, win_ref, wout_ref, h_ref, xn_ref, _emit_to(y_ref), jobs)
    out_ref[...] = _rms(y_ref[...], gf_ref[...])


def _layer1_tail(outs, lses, x, seq, w_o, ffn, g_final):
    n, d = x.shape
    tl = _Tiling(n, seq)
    g2, w_in, w_out = ffn

    def group_specs(width):
        return [tl.row(width, tl.lead) if dil == 1 else tl.strided(width, tl.lead) for _, dil in DILATED_GROUPS]

    n_strided = sum(dil > 1 for _, dil in DILATED_GROUPS)
    return pl.pallas_call(
        _layer1_tail_kernel, grid=(tl.steps,),
        in_specs=group_specs(d) + group_specs(LANES)
        + [tl.row(d, tl.lead), _const_spec(w_o.shape), _const_spec((1, d)), _const_spec(w_in.shape),
           _const_spec(w_out.shape), _const_spec((1, d))],
        out_specs=tl.row(d, tl.trail), out_shape=jax.ShapeDtypeStruct((n, d), F32),
        scratch_shapes=_swiglu_scratch(tl.tm, d, w_out.shape[0])
        + [pltpu.VMEM((2, tl.tm, d), F32), pltpu.VMEM((tl.tm, d), F32),
           pltpu.VMEM((n_strided * (d // HEAD_DIM), tl.tm, HEAD_DIM), F32),
           pltpu.VMEM((n_strided, tl.tm, LANES), F32), pltpu.VMEM((N_DIL, tl.tm, LANES), F32),
           pltpu.VMEM((tl.tm, d), BF16)],
        compiler_params=_params(1), name="layer1_tail")(*outs, *lses, x, w_o, g2, w_in, w_out, g_final)


def _attn_prompt_kernel(q_ref, kc_ref, kp_ref, vc_ref, vp_ref, o_ref, lse_ref, kk_ref, vv_ref):
    n = pl.program_id(2)
    nc, rows, d = q_ref.shape
    blk = ATTN_BLOCK
    sb = blk // nc
    n_heads = d // HEAD_DIM
    for c in range(nc):
        kk_ref[c, 0:sb, :] = kp_ref[c]
        kk_ref[c, sb:, :] = kc_ref[c]
        vv_ref[c, 0:sb, :] = vp_ref[c]
        vv_ref[c, sb:, :] = vc_ref[c]
    qrow = lax.broadcasted_iota(jnp.int32, (blk, 2 * blk), 0)
    kcol = lax.broadcasted_iota(jnp.int32, (blk, 2 * blk), 1)
    dist = nc * (qrow % sb - kcol % (2 * sb) + sb) + (qrow // sb - kcol // (2 * sb))
    band = (dist >= 0) & (dist <= blk)
    lane = lax.broadcasted_iota(jnp.int32, (blk, LANES), 1)
    scale = HEAD_DIM ** -0.5

    def gather(ref, lo, hi, sl):
        parts = [ref[c, lo:hi, sl] for c in range(nc)]
        return parts[0] if nc == 1 else jnp.concatenate(parts, axis=0)

    for u in range(rows // sb):
        valid = band
        if u == 0:
            valid = band & ((kcol % (2 * sb) >= sb) | (n > 0))
        lse_tile = jnp.zeros((blk, LANES), F32)
        for hh in range(n_heads):
            sl = slice(hh * HEAD_DIM, (hh + 1) * HEAD_DIM)
            qh = gather(q_ref, u * sb, (u + 1) * sb, sl)
            kh = gather(kk_ref, u * sb, (u + 2) * sb, sl)
            vh = gather(vv_ref, u * sb, (u + 2) * sb, sl)
            s = lax.dot_general(qh, kh, (((1,), (1,)), ((), ())), preferred_element_type=F32)
            s = jnp.where(valid, s, -jnp.inf)
            m = jnp.max(s, axis=1, keepdims=True)
            p = jnp.exp2((s - m) * (scale * LOG2_E))
            den = jnp.sum(p, axis=1, keepdims=True)
            o = jnp.dot(p.astype(BF16), vh, preferred_element_type=F32) / den
            for c in range(nc):
                o_ref[c, u * sb:(u + 1) * sb, sl] = o[c * sb:(c + 1) * sb]
            lse_tile = jnp.where(lane == hh, m * scale + jnp.log(den), lse_tile)
        for c in range(nc):
            lse_ref[c, u * sb:(u + 1) * sb, :] = lse_tile[c * sb:(c + 1) * sb]


def _attn_prompt(q, k, v, dil):
    b, r_all, length, d = q.shape
    nc = r_all // dil
    sb = ATTN_BLOCK // nc
    rows = ATTN_QUERIES // nc
    view = lambda a: a.reshape(b, nc, dil, length, a.shape[-1])
    cur = pl.BlockSpec((None, nc, None, rows, d), lambda i, r, n: (i, 0, r, n, 0))
    prev = pl.BlockSpec((None, nc, None, sb, d),
                        lambda i, r, n: (i, 0, r, jnp.maximum(n * (rows // sb) - 1, 0), 0))
    o, lse = pl.pallas_call(
        _attn_prompt_kernel, grid=(b, dil, length // rows),
        in_specs=[cur, cur, prev, cur, prev],
        out_specs=[cur, pl.BlockSpec((None, nc, None, rows, LANES), lambda i, r, n: (i, 0, r, n, 0))],
        out_shape=[jax.ShapeDtypeStruct((b, nc, dil, length, d), F32),
                   jax.ShapeDtypeStruct((b, nc, dil, length, LANES), F32)],
        scratch_shapes=[pltpu.VMEM((nc, rows + sb, d), BF16), pltpu.VMEM((nc, rows + sb, d), BF16)],
        compiler_params=_params(3), name=f"attn_prompt_d{dil}")(view(q), view(k), view(k), view(v), view(v))
    return o.reshape(b, r_all, length, d), lse.reshape(b, r_all, length, LANES)


def _ffn_kernel(x_ref, g_ref, win_ref, wout_ref, o_ref, h_ref, xn_ref):
    _swiglu_half(x_ref, g_ref, win_ref, wout_ref, h_ref, xn_ref, _emit_to(o_ref))


def _ffn_final_kernel(x_ref, g_ref, win_ref, wout_ref, gf_ref, o_ref, h_ref, xn_ref, y_ref):
    _swiglu_half(x_ref, g_ref, win_ref, wout_ref, h_ref, xn_ref, _emit_to(y_ref))
    o_ref[...] = _rms(y_ref[...], gf_ref[...])


def _ffn_sample(x, ffn, final_g=None):
    n, d = x.shape
    g, w_in, w_out = ffn
    in_specs = [_full_spec((n, d)), _full_spec((1, d)), _full_spec(w_in.shape), _full_spec(w_out.shape)]
    args = [x, g, w_in, w_out]
    scratch = _swiglu_scratch(n, d, w_out.shape[0])
    body = _ffn_kernel
    if final_g is not None:
        in_specs.append(_full_spec((1, d)))
        args.append(final_g)
        scratch.append(pltpu.VMEM((n, d), F32))
        body = _ffn_final_kernel
    return pl.pallas_call(
        body, grid=(1,), in_specs=in_specs, out_specs=_full_spec((n, d)),
        out_shape=jax.ShapeDtypeStruct((n, d), F32), scratch_shapes=scratch,
        compiler_params=_params(1), name="ffn_sample")(*args)


def _pool_sample_kernel(x_ref, st_ref, g_ref, wg_ref, scale_ref, wo_ref, o_ref, new_ref, z_ref):
    n, d = x_ref.shape
    gd = d // len(POOL_WINDOWS)
    x = x_ref[...]
    h = _rms(x, g_ref[...])
    for gi, w in enumerate(POOL_WINDOWS):
        c0, c1 = gi * gd, (gi + 1) * gd
        cur = h[:, c0:c1]
        tot = cur
        for s in range(1, w):
            tot = tot + st_ref[POOL_BUF - s, :, c0:c1]
        p = (tot * (1.0 / w) - cur).astype(BF16)
        z_ref[:, c0:c1] = (jnp.dot(p, wg_ref[gi], preferred_element_type=F32) * scale_ref[:, c0:c1]).astype(BF16)
    o_ref[...] = x + jnp.dot(z_ref[...], wo_ref[...], preferred_element_type=F32)
    for r in range(POOL_BUF - 1):
        new_ref[r] = st_ref[r + 1]
    new_ref[POOL_BUF - 1] = h


def _pool_sample(x, st, g, w_group, scale, w_o):
    n, d = x.shape
    return pl.pallas_call(
        _pool_sample_kernel, grid=(1,),
        in_specs=[_full_spec((n, d)), _full_spec(st.shape), _full_spec((1, d)), _full_spec(w_group.shape),
                  _full_spec((1, d)), _full_spec(w_o.shape)],
        out_specs=[_full_spec((n, d)), _full_spec(st.shape)],
        out_shape=[jax.ShapeDtypeStruct((n, d), F32), jax.ShapeDtypeStruct(st.shape, F32)],
        scratch_shapes=[pltpu.VMEM((n, d), BF16)],
        compiler_params=_params(1), name="pool_sample")(x, st, g, w_group, scale, w_o)


def _proj_sample_kernel(x_ref, g_ref, w_ref, c_ref, a_ref, b_ref, y_ref, xn_ref, *, n_rope_heads):
    xn_ref[...] = _rms(x_ref[...], g_ref[...]).astype(BF16)
    n_slots = y_ref.shape[0]
    stored_in_place = lambda hh: None
    jobs = _projection_jobs(xn_ref, w_ref, 0, (c_ref, a_ref, b_ref), y_ref.at[0:n_rope_heads], stored_in_place)
    if n_rope_heads < n_slots:
        jobs += _projection_jobs(xn_ref, w_ref, n_rope_heads * HEAD_DIM, None, y_ref.at[n_rope_heads:n_slots],
                                 stored_in_place)
    for job in jobs:
        job()


def _proj_sample(x, g, w, tables, n_rope_heads):
    n, d = x.shape
    n_slots = w.shape[1] // HEAD_DIM
    return pl.pallas_call(
        functools.partial(_proj_sample_kernel, n_rope_heads=n_rope_heads), grid=(1,),
        in_specs=[_full_spec((n, d)), _full_spec((1, d)), _full_spec(w.shape)]
        + [_full_spec((n, HEAD_DIM))] * 3,
        out_specs=_full_spec((n_slots, n, HEAD_DIM)),
        out_shape=jax.ShapeDtypeStruct((n_slots, n, HEAD_DIM), F32),
        scratch_shapes=[pltpu.VMEM((n, d), BF16)],
        compiler_params=_params(1), name="proj_sample")(x, g, w, *tables)


def _attn_sample_kernel(*refs):
    q_ref, kn_ref, vn_ref = refs[:3]
    kc_refs = refs[3:3 + N_DIL]
    vc_refs = refs[3 + N_DIL:3 + 2 * N_DIL]
    att_ref = refs[3 + 2 * N_DIL]
    scale = HEAD_DIM ** -0.5
    kn, vn = kn_ref[...], vn_ref[...]
    outs, lses = [], []
    for g in range(N_DIL):
        qg = q_ref[g]
        s_past = jnp.sum(kc_refs[g][...] * qg[None], axis=2, keepdims=True) * scale
        s_new = jnp.sum(kn * qg, axis=1, keepdims=True) * scale
        m = jnp.maximum(jnp.max(s_past, axis=0), s_new)
        p_past = jnp.exp(s_past - m[None])
        p_new = jnp.exp(s_new - m)
        den = jnp.sum(p_past, axis=0) + p_new
        o = jnp.sum(p_past * vc_refs[g][...], axis=0) + p_new * vn
        outs.append(o / den)
        lses.append(m + jnp.log(den))
    top = functools.reduce(jnp.maximum, lses)
    es = [jnp.exp(l - top) for l in lses]
    inv = 1.0 / functools.reduce(jnp.add, es)
    att_ref[...] = functools.reduce(jnp.add, [e * inv * o for e, o in zip(es, outs)])


def _attn_sample(q, k_new, v_new, cache_k, cache_v):
    b, w, h, e = cache_k.shape
    cache_args, cache_specs = [], []
    for cache in (cache_k, cache_v):
        for win, dil in DILATED_GROUPS:
            n_keys = win // dil
            last = w // dil // n_keys - 1
            cache_args.append(cache.reshape(b, w // dil, dil, h, e))
            cache_specs.append(pl.BlockSpec((None, n_keys, None, h, e),
                                            lambda i, last=last: (i, last, 0, 0, 0)))
    head = pl.BlockSpec((None, h, e), lambda i: (i, 0, 0))
    return pl.pallas_call(
        _attn_sample_kernel, grid=(b,),
        in_specs=[pl.BlockSpec((None, N_DIL, h, e), lambda i: (i, 0, 0, 0)), head, head] + cache_specs,
        out_specs=head, out_shape=jax.ShapeDtypeStruct((b, h, e), F32),
        compiler_params=_params(1), name="attn_sample")(q, k_new, v_new, *cache_args)


def _dense_residual_kernel(a_ref, x_ref, w_ref, o_ref):
    o_ref[...] = x_ref[...] + jnp.dot(a_ref[...].astype(BF16), w_ref[...], preferred_element_type=F32)


def _dense_residual(a, x, w):
    n, d = x.shape
    return pl.pallas_call(
        _dense_residual_kernel, grid=(1,),
        in_specs=[_full_spec(a.shape), _full_spec((n, d)), _full_spec(w.shape)], out_specs=_full_spec((n, d)),
        out_shape=jax.ShapeDtypeStruct((n, d), F32),
        compiler_params=_params(1), name="dense_residual")(a, x, w)


def kernel(x_prompt, x_sample, state_pool, cache_k, cache_v, ffn1_norm, ffn1_w_in, ffn1_w_out, mix_norm,
           ffn2_norm, ffn2_w_in, ffn2_w_out, pool_w_group, pool_scale, pool_w_o, kv_norm, w_kv, attn_w_q,
           attn_w_o, final_norm):
    b, s, d = x_prompt.shape
    bs = x_sample.shape[0]
    assert x_sample.shape[1] == 1
    n_heads = d // HEAD_DIM
    w_cache = cache_k.shape[1]
    assert w_cache == min(WINDOW_MAX, PAST_LEN)
    keep = min(WINDOW_MAX, s)

    bf = lambda a: a.astype(BF16)
    vec = lambda a: a.reshape(1, d)
    f1_in, f1_out, f2_in, f2_out = bf(ffn1_w_in), bf(ffn1_w_out), bf(ffn2_w_in), bf(ffn2_w_out)
    ffn1 = [(vec(ffn1_norm[l]), f1_in[l], f1_out[l]) for l in range(2)]
    ffn2 = [(vec(ffn2_norm[l]), f2_in[l], f2_out[l]) for l in range(2)]
    wg, wpo, wkv, wq, wo = bf(pool_w_group[0]), bf(pool_w_o[0]), bf(w_kv), bf(attn_w_q[0]), bf(attn_w_o[0])

    tab_p = _rope_tables(jnp.arange(s, dtype=jnp.int32))
    tab_s = _rope_tables(jnp.full((bs,), PAST_LEN, dtype=jnp.int32))

    xs = _ffn_sample(x_sample.reshape(bs, d), ffn1[0])
    hist = jnp.swapaxes(state_pool[0], 0, 1)
    xs, hist_new = _pool_sample(xs, hist, vec(mix_norm[0]), wg, vec(pool_scale[0]), wpo)
    pool_s = jnp.swapaxes(hist_new, 0, 1)[None]
    xs = _ffn_sample(xs, ffn2[0])
    kv_new = jnp.swapaxes(_proj_sample(xs, vec(kv_norm), wkv, tab_s, n_heads), 0, 1)
    k_new, v_new = kv_new[:, :n_heads], kv_new[:, n_heads:]
    xs = _ffn_sample(xs, ffn1[1])
    q_new = jnp.swapaxes(_proj_sample(xs, vec(mix_norm[1]), wq, tab_s, N_DIL * n_heads), 0, 1)
    att = _attn_sample(q_new.reshape(bs, N_DIL, n_heads, HEAD_DIM), k_new, v_new, cache_k, cache_v)
    xs = _dense_residual(att.reshape(bs, d), xs, wo)
    y_sample = _ffn_sample(xs, ffn2[1], vec(final_norm)).reshape(bs, 1, d)

    x = x_prompt.reshape(b * s, d)
    x, pool_hist, k_s = _layer0_head(x, s, ffn1[0], vec(mix_norm[0]), wg, vec(pool_scale[0]), wpo,
                                     cache_k, k_new[:, None])
    pool_p = pool_hist[:, HALO - POOL_BUF:][None]
    x, k_nat, v_nat, k_str, v_str, k_keep, v_keep, v_s = _layer0_tail(
        x, s, keep, ffn2[0], vec(kv_norm), wkv, tab_p, cache_v, v_new[:, None])
    x, *qs_by_group = _layer1_head(x, s, ffn1[1], vec(mix_norm[1]), wq, tab_p)
    outs, lses = [], []
    for (win, dil), q in zip(DILATED_GROUPS, qs_by_group):
        assert win // dil == ATTN_BLOCK
        if dil == 1:
            nat = lambda a: a.reshape(b, 1, s, d)
            o, lse = _attn_prompt(nat(q), nat(k_nat), nat(v_nat), 1)
            outs.append(o.reshape(b * s, d))
            lses.append(lse.reshape(b * s, LANES))
        else:
            o, lse = _attn_prompt(q, k_str, v_str, dil)
            outs.append(o)
            lses.append(lse)
    y_prompt = _layer1_tail(outs, lses, x, s, wo, ffn2[1], vec(final_norm)).reshape(b, s, d)
    k_p = k_keep.reshape(b, keep, n_heads, HEAD_DIM)
    v_p = v_keep.reshape(b, keep, n_heads, HEAD_DIM)

    return (y_prompt, y_sample, pool_p, pool_s, k_p, v_p, k_s, v_s)
```

```python
import functools

import jax
import jax.numpy as jnp
from jax import lax
from jax.experimental import pallas as pl
from jax.experimental.pallas import tpu as pltpu

F32 = jnp.float32
BF16 = jnp.bfloat16

POOL_WINDOWS = (2, 4, 8, 16)
POOL_BUF = max(POOL_WINDOWS) - 1
HEAD_DIM = 128
DILATED_GROUPS = ((128, 1), (512, 4), (2048, 16))
N_DIL = len(DILATED_GROUPS)
WINDOW_MAX = max(w for w, _ in DILATED_GROUPS)
STRIDE = max(d for _, d in DILATED_GROUPS)
ATTN_BLOCK = 128
ROT_DIM = HEAD_DIM // 4
ROPE_THETA = 500000.0
RMS_EPS = 1e-6
PAST_LEN = 16384
LOG2_E = 1.4426950408889634

V7X_VMEM_BYTES = 64 * 1024 * 1024
VMEM_LIMIT = (V7X_VMEM_BYTES * 7) // 8
LANES = 128
MXU_WIDTH = 256
HALO = 16

ROW_TILE = 256
BIG_TILE = 512
ATTN_QUERIES = 512
SOFTMAX_ROWS = 128


def _const_spec(shape):
    n = len(shape)
    return pl.BlockSpec(shape, lambda *_: (0,) * n, pipeline_mode=pl.Buffered(1))


def _full_spec(shape):
    return pl.BlockSpec(shape, lambda i: (0,) * len(shape))


def _params(n_axes):
    return pltpu.CompilerParams(dimension_semantics=("arbitrary",) * n_axes,
                                vmem_limit_bytes=VMEM_LIMIT)


def _rms(x, g):
    return x * lax.rsqrt(jnp.mean(x * x, axis=-1, keepdims=True) + RMS_EPS) * g


def _col_chunks(width):
    return [(c, c + MXU_WIDTH) for c in range(0, width, MXU_WIDTH)]


def _swiglu_half(src_ref, g_ref, win_ref, wout_ref, h_ref, xn_ref, emit, side_jobs=()):
    d_ff, d = wout_ref.shape
    ff_chunks, out_chunks = _col_chunks(d_ff), _col_chunks(d)
    n_seg = len(ff_chunks) + len(out_chunks)
    jobs = list(side_jobs)
    done = [0, 0]

    def end_segment():
        done[0] += 1
        while done[1] < len(jobs) and done[1] * n_seg < done[0] * len(jobs):
            jobs[done[1]]()
            done[1] += 1

    xn_ref[...] = _rms(src_ref[...], g_ref[...]).astype(BF16)
    for c0, c1 in ff_chunks:
        gate = jnp.dot(xn_ref[...], win_ref[:, c0:c1], preferred_element_type=F32)
        up = jnp.dot(xn_ref[...], win_ref[:, d_ff + c0:d_ff + c1], preferred_element_type=F32)
        h_ref[:, c0:c1] = (gate * jax.nn.sigmoid(gate) * up).astype(BF16)
        end_segment()
    for c0, c1 in out_chunks:
        emit(c0, c1, src_ref[:, c0:c1] + 0.5 * jnp.dot(h_ref[...], wout_ref[:, c0:c1],
                                                        preferred_element_type=F32))
        end_segment()


def _swiglu_scratch(rows, d, d_ff):
    return [pltpu.VMEM((rows, d_ff), BF16), pltpu.VMEM((rows, d), BF16)]


def _pool_trailing(ext_ref, row0, rows, j, gi, w, gd):
    c0, c1 = gi * gd, (gi + 1) * gd
    cur = ext_ref[row0:row0 + rows, c0:c1]
    tot = cur
    for s in range(1, w):
        tot = tot + ext_ref[row0 - s:row0 - s + rows, c0:c1]
    if j is None:
        return tot * (1.0 / w) - cur
    pos1 = (j * rows + 1 + lax.broadcasted_iota(jnp.int32, (rows, 1), 0)).astype(F32)
    return tot * (1.0 / jnp.minimum(pos1, float(w))) - cur


def _pool_jobs(src_ref, j, g_ref, wg_ref, scale_ref, wo_ref, ext_ref, z_ref, emit, state_ref):
    tm, d = src_ref.shape
    gd = d // len(POOL_WINDOWS)

    def normalise():
        h = _rms(src_ref[...], g_ref[...])
        ext_ref[0:HALO, :] = jnp.where(j == 0, 0.0, ext_ref[tm:tm + HALO, :])
        ext_ref[HALO:HALO + tm, :] = h
        state_ref[...] = h[tm - HALO:, :]

    def group(gi, w):
        p = _pool_trailing(ext_ref, HALO, tm, j, gi, w, gd).astype(BF16)
        z = jnp.dot(p, wg_ref[gi], preferred_element_type=F32) * scale_ref[:, gi * gd:(gi + 1) * gd]
        z_ref[:, gi * gd:(gi + 1) * gd] = z.astype(BF16)

    def project(c0, c1):
        emit(c0, c1, src_ref[:, c0:c1] + jnp.dot(z_ref[...], wo_ref[:, c0:c1], preferred_element_type=F32))

    return ([normalise] + [functools.partial(group, gi, w) for gi, w in enumerate(POOL_WINDOWS)]
            + [functools.partial(project, c0, c1) for c0, c1 in _col_chunks(d)])


def _rope_tables(pos):
    half = ROT_DIM // 2
    inv_freq = ROPE_THETA ** (-jnp.arange(0, ROT_DIM, 2, dtype=F32) / ROT_DIM)
    ang = pos.astype(F32)[:, None] * inv_freq[None, :]
    cos, sin = jnp.cos(ang), jnp.sin(ang)
    t = pos.shape[0]
    c = jnp.concatenate([cos, cos, jnp.ones((t, HEAD_DIM - ROT_DIM), F32)], axis=1)
    a = jnp.concatenate([-sin, jnp.zeros((t, HEAD_DIM - half), F32)], axis=1)
    b = jnp.concatenate([jnp.zeros((t, half), F32), sin, jnp.zeros((t, HEAD_DIM - ROT_DIM), F32)], axis=1)
    return c, a, b


def _projection_jobs(xn_ref, w_ref, col0, table_refs, y_ref, store_head):
    n_slots = y_ref.shape[0]
    half = ROT_DIM // 2
    per_dot = MXU_WIDTH // HEAD_DIM

    def project(h0):
        cols = slice(col0 + h0 * HEAD_DIM, col0 + (h0 + per_dot) * HEAD_DIM)
        y = jnp.dot(xn_ref[...], w_ref[:, cols], preferred_element_type=F32)
        for k in range(per_dot):
            y_ref[h0 + k] = y[:, k * HEAD_DIM:(k + 1) * HEAD_DIM]

    def finish(hh):
        if table_refs is not None:
            c, a, b = (t[...] for t in table_refs)
            yh = y_ref[hh]
            y_ref[hh] = yh * c + pltpu.roll(yh, HEAD_DIM - half, 1) * a + pltpu.roll(yh, half, 1) * b
        store_head(hh)

    jobs = []
    for h0 in range(0, n_slots, per_dot):
        jobs.append(functools.partial(project, h0))
        jobs += [functools.partial(finish, h0 + k) for k in range(per_dot)]
    return jobs


def _store_head_natural(src_ref, hh, dst_ref):
    dst_ref[:, hh * HEAD_DIM:(hh + 1) * HEAD_DIM] = src_ref[hh].astype(dst_ref.dtype)


def _store_head_strided(src_ref, hh, dst_ref):
    per = src_ref.shape[1] // STRIDE
    for r in range(STRIDE):
        dst_ref[r, :, hh * HEAD_DIM:(hh + 1) * HEAD_DIM] = (
            src_ref[hh, pl.ds(r, per, stride=STRIDE), :].astype(dst_ref.dtype))


def _roll_tile(seq_end, src_ref, nxt_ref, new_ref, dst_ref):
    t = src_ref.shape[0]
    dst_ref[0:t - 1] = src_ref[1:t]
    dst_ref[t - 1] = jnp.where(seq_end, new_ref[0], nxt_ref[0])


def _combine_jobs(o_refs, lse_refs, x_ref, wo_ref, o_nat, lse_nat, w_ref, att_ref, emit):
    tm, d = x_ref.shape
    per = tm // STRIDE
    n_heads = d // HEAD_DIM
    heads, lses = [], []
    jobs = []
    slot = 0
    for (_, dil), o_ref, lse_ref in zip(DILATED_GROUPS, o_refs, lse_refs):
        if dil == 1:
            heads.append(functools.partial(lambda hh, ref: ref[:, hh * HEAD_DIM:(hh + 1) * HEAD_DIM], ref=o_ref))
            lses.append(functools.partial(lambda ref: ref[...], lse_ref))
            continue

        def to_natural(r, o_ref=o_ref, lse_ref=lse_ref, slot=slot):
            lse_nat[slot, pl.ds(r, per, stride=STRIDE), :] = lse_ref[r]
            for hh in range(n_heads):
                o_nat[slot * n_heads + hh, pl.ds(r, per, stride=STRIDE), :] = (
                    o_ref[r, :, hh * HEAD_DIM:(hh + 1) * HEAD_DIM])

        jobs += [functools.partial(to_natural, r) for r in range(STRIDE)]
        heads.append(functools.partial(lambda hh, s0: o_nat[s0 + hh], s0=slot * n_heads))
        lses.append(functools.partial(lambda s: lse_nat[s], slot))
        slot += 1

    def weights():
        ls = [l() for l in lses]
        top = functools.reduce(jnp.maximum, ls)
        es = [jnp.exp(l - top) for l in ls]
        inv = 1.0 / functools.reduce(jnp.add, es)
        for g, e in enumerate(es):
            w_ref[g] = e * inv

    def mix_head(hh):
        att = functools.reduce(jnp.add, [w_ref[g][:, hh:hh + 1] * heads[g](hh) for g in range(N_DIL)])
        att_ref[:, hh * HEAD_DIM:(hh + 1) * HEAD_DIM] = att.astype(BF16)

    def project(c0, c1):
        emit(c0, c1, x_ref[:, c0:c1] + jnp.dot(att_ref[...], wo_ref[:, c0:c1], preferred_element_type=F32))

    return (jobs + [weights] + [functools.partial(mix_head, hh) for hh in range(n_heads)]
            + [functools.partial(project, c0, c1) for c0, c1 in _col_chunks(d)])


class _Tiling:
    def __init__(self, n_rows, seq, tm):
        self.tm = tm
        self.n_tiles = n_rows // tm
        self.per_seq = seq // tm
        self.steps = self.n_tiles + 1

    def lead(self, i):
        return jnp.minimum(i, self.n_tiles - 1)

    def trail(self, i):
        return jnp.maximum(i - 1, 0)

    def row(self, width, tile_of):
        return pl.BlockSpec((self.tm, width), lambda i: (tile_of(i), 0))

    def strided(self, width, tile_of):
        return pl.BlockSpec((None, STRIDE, self.tm // STRIDE, width),
                            lambda i: (tile_of(i) // self.per_seq, 0, tile_of(i) % self.per_seq, 0))

    def table(self, tile_of):
        return pl.BlockSpec((self.tm, HEAD_DIM), lambda i: (tile_of(i) % self.per_seq, 0))


def _roll_operands(cache, new, tl):
    b, w, h, e = cache.shape
    per = tl.n_tiles // b
    t = w // per
    blk = pl.BlockSpec((None, t, h, e), lambda i: (tl.lead(i) // per, tl.lead(i) % per, 0, 0))
    nxt = pl.BlockSpec((None, 1, h, e),
                       lambda i: (tl.lead(i) // per, jnp.minimum((tl.lead(i) % per + 1) * t, w - 1), 0, 0))
    cur_new = pl.BlockSpec((None, 1, h, e), lambda i: (tl.lead(i) // per, 0, 0, 0))
    return per, [blk, nxt, cur_new], [cache, cache, new], blk


def _zero_on_first_step(i, *refs):
    @pl.when(i == 0)
    def _():
        for ref in refs:
            ref[...] = jnp.zeros(ref.shape, ref.dtype)


def _emit_to(*refs):
    def emit(c0, c1, value):
        for ref in refs:
            ref[:, c0:c1] = value
    return emit


def _layer0_head_kernel(x_ref, g1_ref, win_ref, wout_ref, gm_ref, wg_ref, scale_ref, wo_ref,
                        src_ref, nxt_ref, new_ref, o_ref, state_ref, dst_ref,
                        h_ref, xn_ref, xs_ref, ext_ref, z_ref, *, tl, roll_per_seq):
    i = pl.program_id(0)
    _zero_on_first_step(i, xs_ref, ext_ref)
    jobs = _pool_jobs(xs_ref.at[(i + 1) % 2], tl.trail(i) % tl.per_seq, gm_ref, wg_ref, scale_ref, wo_ref,
                      ext_ref, z_ref, _emit_to(o_ref), state_ref)
    jobs.insert(len(jobs) // 2, functools.partial(
        _roll_tile, tl.lead(i) % roll_per_seq == roll_per_seq - 1, src_ref, nxt_ref, new_ref, dst_ref))
    _swiglu_half(x_ref, g1_ref, win_ref, wout_ref, h_ref, xn_ref, _emit_to(xs_ref.at[i % 2]), jobs)


def _layer0_head(x, seq, ffn, g_mix, w_group, scale, w_o, cache, new):
    n, d = x.shape
    tl = _Tiling(n, seq, BIG_TILE)
    g1, w_in, w_out = ffn
    roll_per_seq, roll_specs, roll_args, roll_out = _roll_operands(cache, new, tl)
    return pl.pallas_call(
        functools.partial(_layer0_head_kernel, tl=tl, roll_per_seq=roll_per_seq), grid=(tl.steps,),
        in_specs=[tl.row(d, tl.lead), _const_spec((1, d)), _const_spec(w_in.shape), _const_spec(w_out.shape),
                  _const_spec((1, d)), _const_spec(w_group.shape), _const_spec((1, d)), _const_spec(w_o.shape)]
        + roll_specs,
        out_specs=[tl.row(d, tl.trail),
                   pl.BlockSpec((None, HALO, d), lambda i: (tl.trail(i) // tl.per_seq, 0, 0)), roll_out],
        out_shape=[jax.ShapeDtypeStruct((n, d), F32), jax.ShapeDtypeStruct((n // seq, HALO, d), F32),
                   jax.ShapeDtypeStruct(cache.shape, cache.dtype)],
        scratch_shapes=_swiglu_scratch(tl.tm, d, w_out.shape[0])
        + [pltpu.VMEM((2, tl.tm, d), F32), pltpu.VMEM((tl.tm + HALO, d), F32), pltpu.VMEM((tl.tm, d), BF16)],
        compiler_params=_params(1), name="layer0_head")(x, g1, w_in, w_out, g_mix, w_group, scale, w_o, *roll_args)


def _layer0_tail_kernel(x_ref, g2_ref, win_ref, wout_ref, gkv_ref, wkv_ref, c_ref, a_ref, b_ref,
                        src_ref, nxt_ref, new_ref,
                        o_ref, k_ref, v_ref, ks_ref, vs_ref, kf_ref, vf_ref, dst_ref,
                        h_ref, xn_ref, xs_ref, xkv_ref, yk_ref, yv_ref, *, tl, roll_per_seq, kept_tiles):
    i = pl.program_id(0)
    _zero_on_first_step(i, xs_ref)
    d = x_ref.shape[1]

    def normalise():
        xkv_ref[...] = _rms(xs_ref[(i + 1) % 2], gkv_ref[...]).astype(BF16)

    def store_both(y_ref, nat_ref, str_ref, hh):
        _store_head_natural(y_ref, hh, nat_ref)
        _store_head_strided(y_ref, hh, str_ref)

    jobs = [normalise]
    jobs += _projection_jobs(xkv_ref, wkv_ref, 0, (c_ref, a_ref, b_ref), yk_ref,
                             functools.partial(store_both, yk_ref, k_ref, ks_ref))
    jobs += _projection_jobs(xkv_ref, wkv_ref, d, None, yv_ref,
                             functools.partial(store_both, yv_ref, v_ref, vs_ref))
    jobs.insert(len(jobs) // 2, functools.partial(
        _roll_tile, tl.lead(i) % roll_per_seq == roll_per_seq - 1, src_ref, nxt_ref, new_ref, dst_ref))
    _swiglu_half(x_ref, g2_ref, win_ref, wout_ref, h_ref, xn_ref, _emit_to(o_ref, xs_ref.at[i % 2]), jobs)

    @pl.when(tl.trail(i) % tl.per_seq >= tl.per_seq - kept_tiles)
    def _():
        for hh in range(yk_ref.shape[0]):
            kf_ref[:, hh, :] = yk_ref[hh]
            vf_ref[:, hh, :] = yv_ref[hh]


def _layer0_tail(x, seq, keep, ffn, g_kv, w_kv, tables, cache, new):
    n, d = x.shape
    b = n // seq
    n_heads = d // HEAD_DIM
    tl = _Tiling(n, seq, ROW_TILE)
    g2, w_in, w_out = ffn
    kept = keep // tl.tm
    roll_per_seq, roll_specs, roll_args, roll_out = _roll_operands(cache, new, tl)

    def kept_block(i):
        t = tl.trail(i)
        return ((t // tl.per_seq) * kept + jnp.maximum(t % tl.per_seq - (tl.per_seq - kept), 0), 0, 0)

    kept_spec = pl.BlockSpec((tl.tm, n_heads, HEAD_DIM), kept_block)
    nat_shape = jax.ShapeDtypeStruct((n, d), BF16)
    str_shape = jax.ShapeDtypeStruct((b, STRIDE, seq // STRIDE, d), BF16)
    kept_shape = jax.ShapeDtypeStruct((b * keep, n_heads, HEAD_DIM), F32)
    head_scratch = pltpu.VMEM((n_heads, tl.tm, HEAD_DIM), F32)
    return pl.pallas_call(
        functools.partial(_layer0_tail_kernel, tl=tl, roll_per_seq=roll_per_seq, kept_tiles=kept),
        grid=(tl.steps,),
        in_specs=[tl.row(d, tl.lead), _const_spec((1, d)), _const_spec(w_in.shape), _const_spec(w_out.shape),
                  _const_spec((1, d)), _const_spec(w_kv.shape)] + [tl.table(tl.trail)] * 3 + roll_specs,
        out_specs=[tl.row(d, tl.lead), tl.row(d, tl.trail), tl.row(d, tl.trail),
                   tl.strided(d, tl.trail), tl.strided(d, tl.trail), kept_spec, kept_spec, roll_out],
        out_shape=[jax.ShapeDtypeStruct((n, d), F32), nat_shape, nat_shape, str_shape, str_shape,
                   kept_shape, kept_shape, jax.ShapeDtypeStruct(cache.shape, cache.dtype)],
        scratch_shapes=_swiglu_scratch(tl.tm, d, w_out.shape[0])
        + [pltpu.VMEM((2, tl.tm, d), F32), pltpu.VMEM((tl.tm, d), BF16), head_scratch, head_scratch],
        compiler_params=_params(1), name="layer0_tail")(x, g2, w_in, w_out, g_kv, w_kv, *tables, *roll_args)


def _layer1_head_kernel(x_ref, g1_ref, win_ref, wout_ref, gm_ref, wq_ref, c_ref, a_ref, b_ref, o_ref, *refs, tl):
    q_refs, (h_ref, xn_ref, xs_ref, xq_ref, y_ref) = refs[:N_DIL], refs[N_DIL:]
    i = pl.program_id(0)
    _zero_on_first_step(i, xs_ref)
    d = x_ref.shape[1]
    n_heads = d // HEAD_DIM

    def normalise():
        xq_ref[...] = _rms(xs_ref[(i + 1) % 2], gm_ref[...]).astype(BF16)

    jobs = [normalise]
    for g, ((_, dil), q_ref) in enumerate(zip(DILATED_GROUPS, q_refs)):
        yg_ref = y_ref.at[g * n_heads:(g + 1) * n_heads]
        store = _store_head_natural if dil == 1 else _store_head_strided
        jobs += _projection_jobs(xq_ref, wq_ref, g * d, (c_ref, a_ref, b_ref), yg_ref,
                                 functools.partial(lambda hh, st, src, dst: st(src, hh, dst),
                                                   st=store, src=yg_ref, dst=q_ref))
    _swiglu_half(x_ref, g1_ref, win_ref, wout_ref, h_ref, xn_ref, _emit_to(o_ref, xs_ref.at[i % 2]), jobs)


def _layer1_head(x, seq, ffn, g_mix, w_q, tables):
    n, d = x.shape
    b = n // seq
    tl = _Tiling(n, seq, BIG_TILE)
    g1, w_in, w_out = ffn
    q_specs, q_shapes = [], []
    for _, dil in DILATED_GROUPS:
        q_specs.append(tl.row(d, tl.trail) if dil == 1 else tl.strided(d, tl.trail))
        q_shapes.append(jax.ShapeDtypeStruct((n, d) if dil == 1 else (b, STRIDE, seq // STRIDE, d), BF16))
    return pl.pallas_call(
        functools.partial(_layer1_head_kernel, tl=tl), grid=(tl.steps,),
        in_specs=[tl.row(d, tl.lead), _const_spec((1, d)), _const_spec(w_in.shape), _const_spec(w_out.shape),
                  _const_spec((1, d)), _const_spec(w_q.shape)] + [tl.table(tl.trail)] * 3,
        out_specs=[tl.row(d, tl.lead)] + q_specs,
        out_shape=[jax.ShapeDtypeStruct((n, d), F32)] + q_shapes,
        scratch_shapes=_swiglu_scratch(tl.tm, d, w_out.shape[0])
        + [pltpu.VMEM((2, tl.tm, d), F32), pltpu.VMEM((tl.tm, d), BF16),
           pltpu.VMEM((w_q.shape[1] // HEAD_DIM, tl.tm, HEAD_DIM), F32)],
        compiler_params=_params(1), name="layer1_head")(x, g1, w_in, w_out, g_mix, w_q, *tables)


def _layer1_tail_kernel(*refs):
    o_refs, lse_refs = refs[:N_DIL], refs[N_DIL:2 * N_DIL]
    (x_ref, wo_ref, g2_ref, win_ref, wout_ref, gf_ref, out_ref,
     h_ref, xn_ref, xs_ref, y_ref, o_nat, lse_nat, w_ref, att_ref) = refs[2 * N_DIL:]
    i = pl.program_id(0)
    _zero_on_first_step(i, xs_ref)
    jobs = _combine_jobs(o_refs, lse_refs, x_ref, wo_ref, o_nat, lse_nat, w_ref, att_ref,
                         _emit_to(xs_ref.at[i % 2]))
    _swiglu_half(xs_ref.at[(i + 1) % 2], g2_ref, win_ref, wout_ref, h_ref, xn_ref, _emit_to(y_ref), jobs)
    out_ref[...] = _rms(y_ref[...], gf_ref[...])


def _layer1_tail(outs, lses, x, seq, w_o, ffn, g_final):
    n, d = x.shape
    tl = _Tiling(n, seq, BIG_TILE)
    g2, w_in, w_out = ffn

    def group_specs(width):
        return [tl.row(width, tl.lead) if dil == 1 else tl.strided(width, tl.lead) for _, dil in DILATED_GROUPS]

    n_strided = sum(dil > 1 for _, dil in DILATED_GROUPS)
    return pl.pallas_call(
        _layer1_tail_kernel, grid=(tl.steps,),
        in_specs=group_specs(d) + group_specs(LANES)
        + [tl.row(d, tl.lead), _const_spec(w_o.shape), _const_spec((1, d)), _const_spec(w_in.shape),
           _const_spec(w_out.shape), _const_spec((1, d))],
        out_specs=tl.row(d, tl.trail), out_shape=jax.ShapeDtypeStruct((n, d), F32),
        scratch_shapes=_swiglu_scratch(tl.tm, d, w_out.shape[0])
        + [pltpu.VMEM((2, tl.tm, d), F32), pltpu.VMEM((tl.tm, d), F32),
           pltpu.VMEM((n_strided * (d // HEAD_DIM), tl.tm, HEAD_DIM), F32),
           pltpu.VMEM((n_strided, tl.tm, LANES), F32), pltpu.VMEM((N_DIL, tl.tm, LANES), F32),
           pltpu.VMEM((tl.tm, d), BF16)],
        compiler_params=_params(1), name="layer1_tail")(*outs, *lses, x, w_o, g2, w_in, w_out, g_final)


def _attn_prompt_kernel(q_ref, kc_ref, kp_ref, vc_ref, vp_ref, o_ref, lse_ref, kk_ref, vv_ref):
    n = pl.program_id(2)
    nc, rows, d = q_ref.shape
    blk = ATTN_BLOCK
    sb = blk // nc
    n_heads = d // HEAD_DIM
    for c in range(nc):
        kk_ref[c, 0:sb, :] = kp_ref[c]
        kk_ref[c, sb:, :] = kc_ref[c]
        vv_ref[c, 0:sb, :] = vp_ref[c]
        vv_ref[c, sb:, :] = vc_ref[c]
    lane = lax.broadcasted_iota(jnp.int32, (SOFTMAX_ROWS, LANES), 1)
    scale = HEAD_DIM ** -0.5

    def keys(ref, u, sl):
        parts = [ref[c, u * sb:(u + 2) * sb, sl] for c in range(nc)]
        return parts[0] if nc == 1 else jnp.concatenate(parts, axis=0)

    def slab_ranges(r0, r1):
        spans = [(c, max(c * sb, r0), min((c + 1) * sb, r1)) for c in range(nc)]
        return [(c, lo - c * sb, lo, hi) for c, lo, hi in spans if lo < hi]

    for r0 in range(0, blk, SOFTMAX_ROWS):
        r1 = r0 + SOFTMAX_ROWS
        qrow = r0 + lax.broadcasted_iota(jnp.int32, (SOFTMAX_ROWS, 2 * blk), 0)
        kcol = lax.broadcasted_iota(jnp.int32, (SOFTMAX_ROWS, 2 * blk), 1)
        dist = nc * (qrow % sb - kcol % (2 * sb) + sb) + (qrow // sb - kcol // (2 * sb))
        band = (dist >= 0) & (dist <= blk)
        for u in range(rows // sb):
            valid = band
            if u == 0:
                valid = band & ((kcol % (2 * sb) >= sb) | (n > 0))
            lse_tile = jnp.zeros((SOFTMAX_ROWS, LANES), F32)
            for hh in range(n_heads):
                sl = slice(hh * HEAD_DIM, (hh + 1) * HEAD_DIM)
                q_parts = [q_ref[c, u * sb + off:u * sb + off + hi - lo, sl]
                           for c, off, lo, hi in slab_ranges(r0, r1)]
                qh = q_parts[0] if len(q_parts) == 1 else jnp.concatenate(q_parts, axis=0)
                s = lax.dot_general(qh, keys(kk_ref, u, sl), (((1,), (1,)), ((), ())),
                                    preferred_element_type=F32)
                s = jnp.where(valid, s, -jnp.inf)
                m = jnp.max(s, axis=1, keepdims=True)
                p = jnp.exp2((s - m) * (scale * LOG2_E))
                den = jnp.sum(p, axis=1, keepdims=True)
                o = jnp.dot(p.astype(BF16), keys(vv_ref, u, sl), preferred_element_type=F32) / den
                for c, off, lo, hi in slab_ranges(r0, r1):
                    o_ref[c, u * sb + off:u * sb + off + hi - lo, sl] = o[lo - r0:hi - r0]
                lse_tile = jnp.where(lane == hh, m * scale + jnp.log(den), lse_tile)
            for c, off, lo, hi in slab_ranges(r0, r1):
                lse_ref[c, u * sb + off:u * sb + off + hi - lo, :] = lse_tile[lo - r0:hi - r0]


def _attn_prompt(q, k, v, dil):
    b, r_all, length, d = q.shape
    nc = r_all // dil
    sb = ATTN_BLOCK // nc
    rows = ATTN_QUERIES // nc
    view = lambda a: a.reshape(b, nc, dil, length, a.shape[-1])
    cur = pl.BlockSpec((None, nc, None, rows, d), lambda i, r, n: (i, 0, r, n, 0))
    prev = pl.BlockSpec((None, nc, None, sb, d),
                        lambda i, r, n: (i, 0, r, jnp.maximum(n * (rows // sb) - 1, 0), 0))
    o, lse = pl.pallas_call(
        _attn_prompt_kernel, grid=(b, dil, length // rows),
        in_specs=[cur, cur, prev, cur, prev],
        out_specs=[cur, pl.BlockSpec((None, nc, None, rows, LANES), lambda i, r, n: (i, 0, r, n, 0))],
        out_shape=[jax.ShapeDtypeStruct((b, nc, dil, length, d), F32),
                   jax.ShapeDtypeStruct((b, nc, dil, length, LANES), F32)],
        scratch_shapes=[pltpu.VMEM((nc, rows + sb, d), BF16), pltpu.VMEM((nc, rows + sb, d), BF16)],
        compiler_params=_params(3), name=f"attn_prompt_d{dil}")(view(q), view(k), view(k), view(v), view(v))
    return o.reshape(b, r_all, length, d), lse.reshape(b, r_all, length, LANES)


def _ffn_kernel(x_ref, g_ref, win_ref, wout_ref, o_ref, h_ref, xn_ref):
    _swiglu_half(x_ref, g_ref, win_ref, wout_ref, h_ref, xn_ref, _emit_to(o_ref))


def _ffn_final_kernel(x_ref, g_ref, win_ref, wout_ref, gf_ref, o_ref, h_ref, xn_ref, y_ref):
    _swiglu_half(x_ref, g_ref, win_ref, wout_ref, h_ref, xn_ref, _emit_to(y_ref))
    o_ref[...] = _rms(y_ref[...], gf_ref[...])


def _ffn_sample(x, ffn, final_g=None):
    n, d = x.shape
    g, w_in, w_out = ffn
    in_specs = [_full_spec((n, d)), _full_spec((1, d)), _full_spec(w_in.shape), _full_spec(w_out.shape)]
    args = [x, g, w_in, w_out]
    scratch = _swiglu_scratch(n, d, w_out.shape[0])
    body = _ffn_kernel
    if final_g is not None:
        in_specs.append(_full_spec((1, d)))
        args.append(final_g)
        scratch.append(pltpu.VMEM((n, d), F32))
        body = _ffn_final_kernel
    return pl.pallas_call(
        body, grid=(1,), in_specs=in_specs, out_specs=_full_spec((n, d)),
        out_shape=jax.ShapeDtypeStruct((n, d), F32), scratch_shapes=scratch,
        compiler_params=_params(1), name="ffn_sample")(*args)


def _pool_sample_kernel(x_ref, st_ref, g_ref, wg_ref, scale_ref, wo_ref, o_ref, new_ref, z_ref):
    n, d = x_ref.shape
    gd = d // len(POOL_WINDOWS)
    x = x_ref[...]
    h = _rms(x, g_ref[...])
    for gi, w in enumerate(POOL_WINDOWS):
        c0, c1 = gi * gd, (gi + 1) * gd
        cur = h[:, c0:c1]
        tot = cur
        for s in range(1, w):
            tot = tot + st_ref[POOL_BUF - s, :, c0:c1]
        p = (tot * (1.0 / w) - cur).astype(BF16)
        z_ref[:, c0:c1] = (jnp.dot(p, wg_ref[gi], preferred_element_type=F32) * scale_ref[:, c0:c1]).astype(BF16)
    o_ref[...] = x + jnp.dot(z_ref[...], wo_ref[...], preferred_element_type=F32)
    for r in range(POOL_BUF - 1):
        new_ref[r] = st_ref[r + 1]
    new_ref[POOL_BUF - 1] = h


def _pool_sample(x, st, g, w_group, scale, w_o):
    n, d = x.shape
    return pl.pallas_call(
        _pool_sample_kernel, grid=(1,),
        in_specs=[_full_spec((n, d)), _full_spec(st.shape), _full_spec((1, d)), _full_spec(w_group.shape),
                  _full_spec((1, d)), _full_spec(w_o.shape)],
        out_specs=[_full_spec((n, d)), _full_spec(st.shape)],
        out_shape=[jax.ShapeDtypeStruct((n, d), F32), jax.ShapeDtypeStruct(st.shape, F32)],
        scratch_shapes=[pltpu.VMEM((n, d), BF16)],
        compiler_params=_params(1), name="pool_sample")(x, st, g, w_group, scale, w_o)


def _proj_sample_kernel(x_ref, g_ref, w_ref, c_ref, a_ref, b_ref, y_ref, xn_ref, *, n_rope_heads):
    xn_ref[...] = _rms(x_ref[...], g_ref[...]).astype(BF16)
    n_slots = y_ref.shape[0]
    stored_in_place = lambda hh: None
    jobs = _projection_jobs(xn_ref, w_ref, 0, (c_ref, a_ref, b_ref), y_ref.at[0:n_rope_heads], stored_in_place)
    if n_rope_heads < n_slots:
        jobs += _projection_jobs(xn_ref, w_ref, n_rope_heads * HEAD_DIM, None, y_ref.at[n_rope_heads:n_slots],
                                 stored_in_place)
    for job in jobs:
        job()


def _proj_sample(x, g, w, tables, n_rope_heads):
    n, d = x.shape
    n_slots = w.shape[1] // HEAD_DIM
    return pl.pallas_call(
        functools.partial(_proj_sample_kernel, n_rope_heads=n_rope_heads), grid=(1,),
        in_specs=[_full_spec((n, d)), _full_spec((1, d)), _full_spec(w.shape)]
        + [_full_spec((n, HEAD_DIM))] * 3,
        out_specs=_full_spec((n_slots, n, HEAD_DIM)),
        out_shape=jax.ShapeDtypeStruct((n_slots, n, HEAD_DIM), F32),
        scratch_shapes=[pltpu.VMEM((n, d), BF16)],
        compiler_params=_params(1), name="proj_sample")(x, g, w, *tables)


def _attn_sample_kernel(*refs):
    q_ref, kn_ref, vn_ref = refs[:3]
    kc_refs = refs[3:3 + N_DIL]
    vc_refs = refs[3 + N_DIL:3 + 2 * N_DIL]
    att_ref = refs[3 + 2 * N_DIL]
    scale = HEAD_DIM ** -0.5
    kn, vn = kn_ref[...], vn_ref[...]
    outs, lses = [], []
    for g in range(N_DIL):
        qg = q_ref[g]
        s_past = jnp.sum(kc_refs[g][...] * qg[None], axis=2, keepdims=True) * scale
        s_new = jnp.sum(kn * qg, axis=1, keepdims=True) * scale
        m = jnp.maximum(jnp.max(s_past, axis=0), s_new)
        p_past = jnp.exp(s_past - m[None])
        p_new = jnp.exp(s_new - m)
        den = jnp.sum(p_past, axis=0) + p_new
        o = jnp.sum(p_past * vc_refs[g][...], axis=0) + p_new * vn
        outs.append(o / den)
        lses.append(m + jnp.log(den))
    top = functools.reduce(jnp.maximum, lses)
    es = [jnp.exp(l - top) for l in lses]
    inv = 1.0 / functools.reduce(jnp.add, es)
    att_ref[...] = functools.reduce(jnp.add, [e * inv * o for e, o in zip(es, outs)])


def _attn_sample(q, k_new, v_new, cache_k, cache_v):
    b, w, h, e = cache_k.shape
    cache_args, cache_specs = [], []
    for cache in (cache_k, cache_v):
        for win, dil in DILATED_GROUPS:
            n_keys = win // dil
            last = w // dil // n_keys - 1
            cache_args.append(cache.reshape(b, w // dil, dil, h, e))
            cache_specs.append(pl.BlockSpec((None, n_keys, None, h, e),
                                            lambda i, last=last: (i, last, 0, 0, 0)))
    head = pl.BlockSpec((None, h, e), lambda i: (i, 0, 0))
    return pl.pallas_call(
        _attn_sample_kernel, grid=(b,),
        in_specs=[pl.BlockSpec((None, N_DIL, h, e), lambda i: (i, 0, 0, 0)), head, head] + cache_specs,
        out_specs=head, out_shape=jax.ShapeDtypeStruct((b, h, e), F32),
        compiler_params=_params(1), name="attn_sample")(q, k_new, v_new, *cache_args)


def _dense_residual_kernel(a_ref, x_ref, w_ref, o_ref):
    o_ref[...] = x_ref[...] + jnp.dot(a_ref[...].astype(BF16), w_ref[...], preferred_element_type=F32)


def _dense_residual(a, x, w):
    n, d = x.shape
    return pl.pallas_call(
        _dense_residual_kernel, grid=(1,),
        in_specs=[_full_spec(a.shape), _full_spec((n, d)), _full_spec(w.shape)], out_specs=_full_spec((n, d)),
        out_shape=jax.ShapeDtypeStruct((n, d), F32),
        compiler_params=_params(1), name="dense_residual")(a, x, w)


def kernel(x_prompt, x_sample, state_pool, cache_k, cache_v, ffn1_norm, ffn1_w_in, ffn1_w_out, mix_norm,
           ffn2_norm, ffn2_w_in, ffn2_w_out, pool_w_group, pool_scale, pool_w_o, kv_norm, w_kv, attn_w_q,
           attn_w_o, final_norm):
    b, s, d = x_prompt.shape
    bs = x_sample.shape[0]
    assert x_sample.shape[1] == 1
    n_heads = d // HEAD_DIM
    w_cache = cache_k.shape[1]
    assert w_cache == min(WINDOW_MAX, PAST_LEN)
    keep = min(WINDOW_MAX, s)

    bf = lambda a: a.astype(BF16)
    vec = lambda a: a.reshape(1, d)
    ffn1 = [(vec(ffn1_norm[l]), bf(ffn1_w_in[l]), bf(ffn1_w_out[l])) for l in range(2)]
    ffn2 = [(vec(ffn2_norm[l]), bf(ffn2_w_in[l]), bf(ffn2_w_out[l])) for l in range(2)]
    wg, wpo, wkv, wq, wo = bf(pool_w_group[0]), bf(pool_w_o[0]), bf(w_kv), bf(attn_w_q[0]), bf(attn_w_o[0])

    tab_p = _rope_tables(jnp.arange(s, dtype=jnp.int32))
    tab_s = _rope_tables(jnp.full((bs,), PAST_LEN, dtype=jnp.int32))

    xs = _ffn_sample(x_sample.reshape(bs, d), ffn1[0])
    hist = jnp.swapaxes(state_pool[0], 0, 1)
    xs, hist_new = _pool_sample(xs, hist, vec(mix_norm[0]), wg, vec(pool_scale[0]), wpo)
    pool_s = jnp.swapaxes(hist_new, 0, 1)[None]
    xs = _ffn_sample(xs, ffn2[0])
    kv_new = jnp.swapaxes(_proj_sample(xs, vec(kv_norm), wkv, tab_s, n_heads), 0, 1)
    k_new, v_new = kv_new[:, :n_heads], kv_new[:, n_heads:]
    xs = _ffn_sample(xs, ffn1[1])
    q_new = jnp.swapaxes(_proj_sample(xs, vec(mix_norm[1]), wq, tab_s, N_DIL * n_heads), 0, 1)
    att = _attn_sample(q_new.reshape(bs, N_DIL, n_heads, HEAD_DIM), k_new, v_new, cache_k, cache_v)
    xs = _dense_residual(att.reshape(bs, d), xs, wo)
    y_sample = _ffn_sample(xs, ffn2[1], vec(final_norm)).reshape(bs, 1, d)

    x = x_prompt.reshape(b * s, d)
    x, pool_hist, k_s = _layer0_head(x, s, ffn1[0], vec(mix_norm[0]), wg, vec(pool_scale[0]), wpo,
                                     cache_k, k_new[:, None])
    pool_p = pool_hist[:, HALO - POOL_BUF:][None]
    x, k_nat, v_nat, k_str, v_str, k_keep, v_keep, v_s = _layer0_tail(
        x, s, keep, ffn2[0], vec(kv_norm), wkv, tab_p, cache_v, v_new[:, None])
    x, *qs_by_group = _layer1_head(x, s, ffn1[1], vec(mix_norm[1]), wq, tab_p)
    outs, lses = [], []
    for (win, dil), q in zip(DILATED_GROUPS, qs_by_group):
        assert win // dil == ATTN_BLOCK
        if dil == 1:
            nat = lambda a: a.reshape(b, 1, s, d)
            o, lse = _attn_prompt(nat(q), nat(k_nat), nat(v_nat), 1)
            outs.append(o.reshape(b * s, d))
            lses.append(lse.reshape(b * s, LANES))
        else:
            o, lse = _attn_prompt(q, k_str, v_str, dil)
            outs.append(o)
            lses.append(lse)
    y_prompt = _layer1_tail(outs, lses, x, s, wo, ffn2[1], vec(final_norm)).reshape(b, s, d)
    k_p = k_keep.reshape(b, keep, n_heads, HEAD_DIM)
    v_p = v_keep.reshape(b, keep, n_heads, HEAD_DIM)

    return (y_prompt, y_sample, pool_p, pool_s, k_p, v_p, k_s, v_s)
```

```python
import functools

import jax
import jax.numpy as jnp
from jax import lax
from jax.experimental import pallas as pl
from jax.experimental.pallas import tpu as pltpu

F32 = jnp.float32
BF16 = jnp.bfloat16

POOL_WINDOWS = (2, 4, 8, 16)
POOL_BUF = max(POOL_WINDOWS) - 1
HEAD_DIM = 128
DILATED_GROUPS = ((128, 1), (512, 4), (2048, 16))
N_DIL = len(DILATED_GROUPS)
WINDOW_MAX = max(w for w, _ in DILATED_GROUPS)
STRIDE = max(d for _, d in DILATED_GROUPS)
ATTN_BLOCK = 128
ROT_DIM = HEAD_DIM // 4
ROPE_THETA = 500000.0
RMS_EPS = 1e-6
PAST_LEN = 16384
LOG2_E = 1.4426950408889634

V7X_VMEM_BYTES = 64 * 1024 * 1024
VMEM_LIMIT = (V7X_VMEM_BYTES * 29) // 32
LANES = 128
MXU_WIDTH = 256
HALO = 16

ROW_TILE = 256
BIG_TILE = 512
ATTN_QUERIES = 512
SOFTMAX_ROWS = 128


def _const_spec(shape):
    n = len(shape)
    return pl.BlockSpec(shape, lambda *_: (0,) * n, pipeline_mode=pl.Buffered(1))


def _full_spec(shape):
    return pl.BlockSpec(shape, lambda i: (0,) * len(shape))


def _params(n_axes):
    return pltpu.CompilerParams(dimension_semantics=("arbitrary",) * n_axes,
                                vmem_limit_bytes=VMEM_LIMIT)


def _rms(x, g):
    return x * lax.rsqrt(jnp.mean(x * x, axis=-1, keepdims=True) + RMS_EPS) * g


def _col_chunks(width):
    return [(c, c + MXU_WIDTH) for c in range(0, width, MXU_WIDTH)]


def _normalise_into(dst_ref, src_ref, g_ref):
    dst_ref[...] = _rms(src_ref[...], g_ref[...]).astype(dst_ref.dtype)


def _swiglu_half(src_ref, xn_ref, win_ref, wout_ref, h_ref, emit, side_jobs=()):
    d_ff, d = wout_ref.shape
    ff_chunks, out_chunks = _col_chunks(d_ff), _col_chunks(d)
    n_seg = len(ff_chunks) + len(out_chunks)
    jobs = list(side_jobs)
    done = [0, 0]

    def end_segment():
        done[0] += 1
        while done[1] < len(jobs) and done[1] * n_seg < done[0] * len(jobs):
            jobs[done[1]]()
            done[1] += 1

    for c0, c1 in ff_chunks:
        gate = jnp.dot(xn_ref[...], win_ref[:, c0:c1], preferred_element_type=F32)
        up = jnp.dot(xn_ref[...], win_ref[:, d_ff + c0:d_ff + c1], preferred_element_type=F32)
        h_ref[:, c0:c1] = (gate * jax.nn.sigmoid(gate) * up).astype(BF16)
        end_segment()
    for c0, c1 in out_chunks:
        emit(c0, c1, src_ref[:, c0:c1] + 0.5 * jnp.dot(h_ref[...], wout_ref[:, c0:c1],
                                                        preferred_element_type=F32))
        end_segment()


def _pool_trailing(ext_ref, row0, rows, j, gi, w, gd):
    c0, c1 = gi * gd, (gi + 1) * gd
    cur = ext_ref[row0:row0 + rows, c0:c1]
    tot = cur
    for s in range(1, w):
        tot = tot + ext_ref[row0 - s:row0 - s + rows, c0:c1]
    if j is None:
        return tot * (1.0 / w) - cur
    pos1 = (j * rows + 1 + lax.broadcasted_iota(jnp.int32, (rows, 1), 0)).astype(F32)
    return tot * (1.0 / jnp.minimum(pos1, float(w))) - cur


def _pool_jobs(src_ref, j, g_ref, wg_ref, scale_ref, wo_ref, ext_ref, z_ref, o_ref, state_ref, g_next_ref,
               xn_next_ref):
    tm, d = src_ref.shape
    gd = d // len(POOL_WINDOWS)

    def normalise():
        h = _rms(src_ref[...], g_ref[...])
        ext_ref[0:HALO, :] = jnp.where(j == 0, 0.0, ext_ref[tm:tm + HALO, :])
        ext_ref[HALO:HALO + tm, :] = h
        state_ref[...] = h[tm - HALO:, :]

    def group(gi, w):
        p = _pool_trailing(ext_ref, HALO, tm, j, gi, w, gd).astype(BF16)
        z = jnp.dot(p, wg_ref[gi], preferred_element_type=F32) * scale_ref[:, gi * gd:(gi + 1) * gd]
        z_ref[:, gi * gd:(gi + 1) * gd] = z.astype(BF16)

    def project(c0, c1):
        o_ref[:, c0:c1] = src_ref[:, c0:c1] + jnp.dot(z_ref[...], wo_ref[:, c0:c1],
                                                      preferred_element_type=F32)

    return ([normalise] + [functools.partial(group, gi, w) for gi, w in enumerate(POOL_WINDOWS)]
            + [functools.partial(project, c0, c1) for c0, c1 in _col_chunks(d)]
            + [functools.partial(_normalise_into, xn_next_ref, o_ref, g_next_ref)])


def _rope_tables(pos):
    half = ROT_DIM // 2
    inv_freq = ROPE_THETA ** (-jnp.arange(0, ROT_DIM, 2, dtype=F32) / ROT_DIM)
    ang = pos.astype(F32)[:, None] * inv_freq[None, :]
    cos, sin = jnp.cos(ang), jnp.sin(ang)
    t = pos.shape[0]
    c = jnp.concatenate([cos, cos, jnp.ones((t, HEAD_DIM - ROT_DIM), F32)], axis=1)
    a = jnp.concatenate([-sin, jnp.zeros((t, HEAD_DIM - half), F32)], axis=1)
    b = jnp.concatenate([jnp.zeros((t, half), F32), sin, jnp.zeros((t, HEAD_DIM - ROT_DIM), F32)], axis=1)
    return c, a, b


def _projection_jobs(xn_ref, w_ref, col0, table_refs, y_ref, store_head):
    n_slots = y_ref.shape[0]
    half = ROT_DIM // 2
    per_dot = MXU_WIDTH // HEAD_DIM

    def project(h0):
        cols = slice(col0 + h0 * HEAD_DIM, col0 + (h0 + per_dot) * HEAD_DIM)
        y = jnp.dot(xn_ref[...], w_ref[:, cols], preferred_element_type=F32)
        for k in range(per_dot):
            y_ref[h0 + k] = y[:, k * HEAD_DIM:(k + 1) * HEAD_DIM]

    def finish(hh):
        if table_refs is not None:
            c, a, b = (t[...] for t in table_refs)
            yh = y_ref[hh]
            y_ref[hh] = yh * c + pltpu.roll(yh, HEAD_DIM - half, 1) * a + pltpu.roll(yh, half, 1) * b
        store_head(hh)

    jobs = []
    for h0 in range(0, n_slots, per_dot):
        jobs.append(functools.partial(project, h0))
        jobs += [functools.partial(finish, h0 + k) for k in range(per_dot)]
    return jobs


def _store_head_natural(src_ref, hh, dst_ref):
    dst_ref[:, hh * HEAD_DIM:(hh + 1) * HEAD_DIM] = src_ref[hh].astype(dst_ref.dtype)


def _store_head_strided(src_ref, hh, dst_ref):
    per = src_ref.shape[1] // STRIDE
    for r in range(STRIDE):
        dst_ref[r, :, hh * HEAD_DIM:(hh + 1) * HEAD_DIM] = (
            src_ref[hh, pl.ds(r, per, stride=STRIDE), :].astype(dst_ref.dtype))


def _roll_tile(seq_end, src_ref, nxt_ref, new_ref, dst_ref):
    t = src_ref.shape[0]
    dst_ref[0:t - 1] = src_ref[1:t]
    dst_ref[t - 1] = jnp.where(seq_end, new_ref[0], nxt_ref[0])


def _combine_jobs(o_refs, lse_refs, x_ref, wo_ref, o_nat, lse_nat, w_ref, att_ref, dst_ref, g_next_ref,
                  xn_next_ref):
    tm, d = x_ref.shape
    per = tm // STRIDE
    n_heads = d // HEAD_DIM
    heads, lses = [], []
    jobs = []
    slot = 0
    for (_, dil), o_ref, lse_ref in zip(DILATED_GROUPS, o_refs, lse_refs):
        if dil == 1:
            heads.append(functools.partial(lambda hh, ref: ref[:, hh * HEAD_DIM:(hh + 1) * HEAD_DIM], ref=o_ref))
            lses.append(functools.partial(lambda ref: ref[...], lse_ref))
            continue

        def to_natural(r, o_ref=o_ref, lse_ref=lse_ref, slot=slot):
            lse_nat[slot, pl.ds(r, per, stride=STRIDE), :] = lse_ref[r]
            for hh in range(n_heads):
                o_nat[slot * n_heads + hh, pl.ds(r, per, stride=STRIDE), :] = (
                    o_ref[r, :, hh * HEAD_DIM:(hh + 1) * HEAD_DIM])

        jobs += [functools.partial(to_natural, r) for r in range(STRIDE)]
        heads.append(functools.partial(lambda hh, s0: o_nat[s0 + hh], s0=slot * n_heads))
        lses.append(functools.partial(lambda s: lse_nat[s], slot))
        slot += 1

    def weights():
        ls = [l() for l in lses]
        top = functools.reduce(jnp.maximum, ls)
        es = [jnp.exp(l - top) for l in ls]
        inv = 1.0 / functools.reduce(jnp.add, es)
        for g, e in enumerate(es):
            w_ref[g] = e * inv

    def mix_head(hh):
        att = functools.reduce(jnp.add, [w_ref[g][:, hh:hh + 1] * heads[g](hh) for g in range(N_DIL)])
        att_ref[:, hh * HEAD_DIM:(hh + 1) * HEAD_DIM] = att.astype(BF16)

    def project(c0, c1):
        dst_ref[:, c0:c1] = x_ref[:, c0:c1] + jnp.dot(att_ref[...], wo_ref[:, c0:c1],
                                                      preferred_element_type=F32)

    return (jobs + [weights] + [functools.partial(mix_head, hh) for hh in range(n_heads)]
            + [functools.partial(project, c0, c1) for c0, c1 in _col_chunks(d)]
            + [functools.partial(_normalise_into, xn_next_ref, dst_ref, g_next_ref)])


class _Tiling:
    def __init__(self, n_rows, seq, tm):
        self.tm = tm
        self.n_tiles = n_rows // tm
        self.per_seq = seq // tm
        self.steps = self.n_tiles + 1

    def lead(self, i):
        return jnp.minimum(i, self.n_tiles - 1)

    def trail(self, i):
        return jnp.maximum(i - 1, 0)

    def row(self, width, tile_of):
        return pl.BlockSpec((self.tm, width), lambda i: (tile_of(i), 0))

    def strided(self, width, tile_of):
        return pl.BlockSpec((None, STRIDE, self.tm // STRIDE, width),
                            lambda i: (tile_of(i) // self.per_seq, 0, tile_of(i) % self.per_seq, 0))

    def table(self, tile_of):
        return pl.BlockSpec((self.tm, HEAD_DIM), lambda i: (tile_of(i) % self.per_seq, 0))


def _roll_operands(cache, new, tl):
    b, w, h, e = cache.shape
    per = tl.n_tiles // b
    t = w // per
    blk = pl.BlockSpec((None, t, h, e), lambda i: (tl.lead(i) // per, tl.lead(i) % per, 0, 0))
    nxt = pl.BlockSpec((None, 1, h, e),
                       lambda i: (tl.lead(i) // per, jnp.minimum((tl.lead(i) % per + 1) * t, w - 1), 0, 0))
    cur_new = pl.BlockSpec((None, 1, h, e), lambda i: (tl.lead(i) // per, 0, 0, 0))
    return per, [blk, nxt, cur_new], [cache, cache, new], blk


def _zero_on_first_step(i, *refs):
    @pl.when(i == 0)
    def _():
        for ref in refs:
            ref[...] = jnp.zeros(ref.shape, ref.dtype)


def _emit_to(*refs):
    def emit(c0, c1, value):
        for ref in refs:
            ref[:, c0:c1] = value
    return emit


def _layer0_head_kernel(x_ref, g1_ref, win_ref, wout_ref, gm_ref, wg_ref, scale_ref, wo_ref, gn_ref,
                        src_ref, nxt_ref, new_ref, o_ref, on_ref, state_ref, dst_ref,
                        h_ref, xn_ref, xs_ref, ext_ref, z_ref, *, tl, roll_per_seq):
    i = pl.program_id(0)
    _zero_on_first_step(i, xs_ref, ext_ref)
    jobs = _pool_jobs(xs_ref.at[(i + 1) % 2], tl.trail(i) % tl.per_seq, gm_ref, wg_ref, scale_ref, wo_ref,
                      ext_ref, z_ref, o_ref, state_ref, gn_ref, on_ref)
    jobs.insert(len(jobs) // 2, functools.partial(
        _roll_tile, tl.lead(i) % roll_per_seq == roll_per_seq - 1, src_ref, nxt_ref, new_ref, dst_ref))
    _normalise_into(xn_ref, x_ref, g1_ref)
    _swiglu_half(x_ref, xn_ref, win_ref, wout_ref, h_ref, _emit_to(xs_ref.at[i % 2]), jobs)


def _layer0_head(x, seq, ffn, g_mix, w_group, scale, w_o, g_next, cache, new):
    n, d = x.shape
    tl = _Tiling(n, seq, ROW_TILE)
    g1, w_in, w_out = ffn
    roll_per_seq, roll_specs, roll_args, roll_out = _roll_operands(cache, new, tl)
    return pl.pallas_call(
        functools.partial(_layer0_head_kernel, tl=tl, roll_per_seq=roll_per_seq), grid=(tl.steps,),
        in_specs=[tl.row(d, tl.lead), _const_spec((1, d)), _const_spec(w_in.shape), _const_spec(w_out.shape),
                  _const_spec((1, d)), _const_spec(w_group.shape), _const_spec((1, d)), _const_spec(w_o.shape),
                  _const_spec((1, d))] + roll_specs,
        out_specs=[tl.row(d, tl.trail), tl.row(d, tl.trail),
                   pl.BlockSpec((None, HALO, d), lambda i: (tl.trail(i) // tl.per_seq, 0, 0)), roll_out],
        out_shape=[jax.ShapeDtypeStruct((n, d), F32), jax.ShapeDtypeStruct((n, d), BF16),
                   jax.ShapeDtypeStruct((n // seq, HALO, d), F32),
                   jax.ShapeDtypeStruct(cache.shape, cache.dtype)],
        scratch_shapes=[pltpu.VMEM((tl.tm, w_out.shape[0]), BF16), pltpu.VMEM((tl.tm, d), BF16),
                        pltpu.VMEM((2, tl.tm, d), F32), pltpu.VMEM((tl.tm + HALO, d), F32),
                        pltpu.VMEM((tl.tm, d), BF16)],
        compiler_params=_params(1), name="layer0_head")(
            x, g1, w_in, w_out, g_mix, w_group, scale, w_o, g_next, *roll_args)


def _layer0_tail_kernel(x_ref, xn_ref, win_ref, wout_ref, gkv_ref, wkv_ref, gn_ref, c_ref, a_ref, b_ref,
                        src_ref, nxt_ref, new_ref,
                        o_ref, on_ref, k_ref, v_ref, ks_ref, vs_ref, kf_ref, vf_ref, dst_ref,
                        h_ref, xs_ref, xkv_ref, yk_ref, yv_ref, *, tl, roll_per_seq, kept_tiles):
    i = pl.program_id(0)
    _zero_on_first_step(i, xs_ref)
    d = x_ref.shape[1]

    def normalise():
        x = xs_ref[(i + 1) % 2]
        y = x * lax.rsqrt(jnp.mean(x * x, axis=-1, keepdims=True) + RMS_EPS)
        xkv_ref[...] = (y * gkv_ref[...]).astype(BF16)
        on_ref[...] = (y * gn_ref[...]).astype(BF16)

    def store_both(y_ref, nat_ref, str_ref, hh):
        _store_head_natural(y_ref, hh, nat_ref)
        _store_head_strided(y_ref, hh, str_ref)

    jobs = [normalise]
    jobs += _projection_jobs(xkv_ref, wkv_ref, 0, (c_ref, a_ref, b_ref), yk_ref,
                             functools.partial(store_both, yk_ref, k_ref, ks_ref))
    jobs += _projection_jobs(xkv_ref, wkv_ref, d, None, yv_ref,
                             functools.partial(store_both, yv_ref, v_ref, vs_ref))
    jobs.insert(len(jobs) // 2, functools.partial(
        _roll_tile, tl.lead(i) % roll_per_seq == roll_per_seq - 1, src_ref, nxt_ref, new_ref, dst_ref))
    _swiglu_half(x_ref, xn_ref, win_ref, wout_ref, h_ref, _emit_to(o_ref, xs_ref.at[i % 2]), jobs)

    @pl.when(tl.trail(i) % tl.per_seq >= tl.per_seq - kept_tiles)
    def _():
        for hh in range(yk_ref.shape[0]):
            kf_ref[:, hh, :] = yk_ref[hh]
            vf_ref[:, hh, :] = yv_ref[hh]


def _layer0_tail(x, xn, seq, keep, ffn, g_kv, w_kv, g_next, tables, cache, new):
    n, d = x.shape
    b = n // seq
    n_heads = d // HEAD_DIM
    tl = _Tiling(n, seq, ROW_TILE)
    _, w_in, w_out = ffn
    kept = keep // tl.tm
    roll_per_seq, roll_specs, roll_args, roll_out = _roll_operands(cache, new, tl)

    def kept_block(i):
        t = tl.trail(i)
        return ((t // tl.per_seq) * kept + jnp.maximum(t % tl.per_seq - (tl.per_seq - kept), 0), 0, 0)

    kept_spec = pl.BlockSpec((tl.tm, n_heads, HEAD_DIM), kept_block)
    nat_shape = jax.ShapeDtypeStruct((n, d), BF16)
    str_shape = jax.ShapeDtypeStruct((b, STRIDE, seq // STRIDE, d), BF16)
    kept_shape = jax.ShapeDtypeStruct((b * keep, n_heads, HEAD_DIM), F32)
    head_scratch = pltpu.VMEM((n_heads, tl.tm, HEAD_DIM), F32)
    return pl.pallas_call(
        functools.partial(_layer0_tail_kernel, tl=tl, roll_per_seq=roll_per_seq, kept_tiles=kept),
        grid=(tl.steps,),
        in_specs=[tl.row(d, tl.lead), tl.row(d, tl.lead), _const_spec(w_in.shape), _const_spec(w_out.shape),
                  _const_spec((1, d)), _const_spec(w_kv.shape), _const_spec((1, d))]
        + [tl.table(tl.trail)] * 3 + roll_specs,
        out_specs=[tl.row(d, tl.lead), tl.row(d, tl.trail), tl.row(d, tl.trail), tl.row(d, tl.trail),
                   tl.strided(d, tl.trail), tl.strided(d, tl.trail), kept_spec, kept_spec, roll_out],
        out_shape=[jax.ShapeDtypeStruct((n, d), F32), nat_shape, nat_shape, nat_shape, str_shape, str_shape,
                   kept_shape, kept_shape, jax.ShapeDtypeStruct(cache.shape, cache.dtype)],
        scratch_shapes=[pltpu.VMEM((tl.tm, w_out.shape[0]), BF16), pltpu.VMEM((2, tl.tm, d), F32),
                        pltpu.VMEM((tl.tm, d), BF16), head_scratch, head_scratch],
        compiler_params=_params(1), name="layer0_tail")(
            x, xn, w_in, w_out, g_kv, w_kv, g_next, *tables, *roll_args)


def _layer1_head_kernel(x_ref, xn_ref, win_ref, wout_ref, gm_ref, wq_ref, c_ref, a_ref, b_ref, o_ref, *refs, tl):
    q_refs, (h_ref, xs_ref, xq_ref, y_ref) = refs[:N_DIL], refs[N_DIL:]
    i = pl.program_id(0)
    _zero_on_first_step(i, xs_ref)
    d = x_ref.shape[1]
    n_heads = d // HEAD_DIM

    def normalise():
        xq_ref[...] = _rms(xs_ref[(i + 1) % 2], gm_ref[...]).astype(BF16)

    jobs = [normalise]
    for g, ((_, dil), q_ref) in enumerate(zip(DILATED_GROUPS, q_refs)):
        yg_ref = y_ref.at[g * n_heads:(g + 1) * n_heads]
        store = _store_head_natural if dil == 1 else _store_head_strided
        jobs += _projection_jobs(xq_ref, wq_ref, g * d, (c_ref, a_ref, b_ref), yg_ref,
                                 functools.partial(lambda hh, st, src, dst: st(src, hh, dst),
                                                   st=store, src=yg_ref, dst=q_ref))
    _swiglu_half(x_ref, xn_ref, win_ref, wout_ref, h_ref, _emit_to(o_ref, xs_ref.at[i % 2]), jobs)


def _layer1_head(x, xn, seq, ffn, g_mix, w_q, tables):
    n, d = x.shape
    b = n // seq
    tl = _Tiling(n, seq, BIG_TILE)
    _, w_in, w_out = ffn
    q_specs, q_shapes = [], []
    for _, dil in DILATED_GROUPS:
        q_specs.append(tl.row(d, tl.trail) if dil == 1 else tl.strided(d, tl.trail))
        q_shapes.append(jax.ShapeDtypeStruct((n, d) if dil == 1 else (b, STRIDE, seq // STRIDE, d), BF16))
    return pl.pallas_call(
        functools.partial(_layer1_head_kernel, tl=tl), grid=(tl.steps,),
        in_specs=[tl.row(d, tl.lead), tl.row(d, tl.lead), _const_spec(w_in.shape), _const_spec(w_out.shape),
                  _const_spec((1, d)), _const_spec(w_q.shape)] + [tl.table(tl.trail)] * 3,
        out_specs=[tl.row(d, tl.lead)] + q_specs,
        out_shape=[jax.ShapeDtypeStruct((n, d), F32)] + q_shapes,
        scratch_shapes=[pltpu.VMEM((tl.tm, w_out.shape[0]), BF16), pltpu.VMEM((2, tl.tm, d), F32),
                        pltpu.VMEM((tl.tm, d), BF16),
                        pltpu.VMEM((w_q.shape[1] // HEAD_DIM, tl.tm, HEAD_DIM), F32)],
        compiler_params=_params(1), name="layer1_head")(x, xn, w_in, w_out, g_mix, w_q, *tables)


def _layer1_tail_kernel(*refs):
    o_refs, lse_refs = refs[:N_DIL], refs[N_DIL:2 * N_DIL]
    (x_ref, wo_ref, g2_ref, win_ref, wout_ref, gf_ref, out_ref,
     h_ref, xs_ref, xns_ref, o_nat, lse_nat, w_ref, att_ref) = refs[2 * N_DIL:]
    i = pl.program_id(0)
    _zero_on_first_step(i, xs_ref, xns_ref)
    jobs = _combine_jobs(o_refs, lse_refs, x_ref, wo_ref, o_nat, lse_nat, w_ref, att_ref,
                         xs_ref.at[i % 2], g2_ref, xns_ref.at[i % 2])
    _swiglu_half(xs_ref.at[(i + 1) % 2], xns_ref.at[(i + 1) % 2], win_ref, wout_ref, h_ref,
                 _emit_to(out_ref), jobs)
    out_ref[...] = _rms(out_ref[...], gf_ref[...])


def _layer1_tail(outs, lses, x, seq, w_o, ffn, g_final):
    n, d = x.shape
    tl = _Tiling(n, seq, BIG_TILE)
    g2, w_in, w_out = ffn

    def group_specs(width):
        return [tl.row(width, tl.lead) if dil == 1 else tl.strided(width, tl.lead) for _, dil in DILATED_GROUPS]

    n_strided = sum(dil > 1 for _, dil in DILATED_GROUPS)
    return pl.pallas_call(
        _layer1_tail_kernel, grid=(tl.steps,),
        in_specs=group_specs(d) + group_specs(LANES)
        + [tl.row(d, tl.lead), _const_spec(w_o.shape), _const_spec((1, d)), _const_spec(w_in.shape),
           _const_spec(w_out.shape), _const_spec((1, d))],
        out_specs=tl.row(d, tl.trail), out_shape=jax.ShapeDtypeStruct((n, d), F32),
        scratch_shapes=[pltpu.VMEM((tl.tm, w_out.shape[0]), BF16), pltpu.VMEM((2, tl.tm, d), F32),
                        pltpu.VMEM((2, tl.tm, d), BF16),
                        pltpu.VMEM((n_strided * (d // HEAD_DIM), tl.tm, HEAD_DIM), F32),
                        pltpu.VMEM((n_strided, tl.tm, LANES), F32), pltpu.VMEM((N_DIL, tl.tm, LANES), F32),
                        pltpu.VMEM((tl.tm, d), BF16)],
        compiler_params=_params(1), name="layer1_tail")(*outs, *lses, x, w_o, g2, w_in, w_out, g_final)


def _attn_prompt_kernel(q_ref, kc_ref, kp_ref, vc_ref, vp_ref, o_ref, lse_ref, kk_ref, vv_ref):
    n = pl.program_id(2)
    nc, rows, d = q_ref.shape
    blk = ATTN_BLOCK
    sb = blk // nc
    n_heads = d // HEAD_DIM
    for c in range(nc):
        kk_ref[c, 0:sb, :] = kp_ref[c]
        kk_ref[c, sb:, :] = kc_ref[c]
        vv_ref[c, 0:sb, :] = vp_ref[c]
        vv_ref[c, sb:, :] = vc_ref[c]
    lane = lax.broadcasted_iota(jnp.int32, (SOFTMAX_ROWS, LANES), 1)
    scale = HEAD_DIM ** -0.5

    def keys(ref, u, sl):
        parts = [ref[c, u * sb:(u + 2) * sb, sl] for c in range(nc)]
        return parts[0] if nc == 1 else jnp.concatenate(parts, axis=0)

    def slab_ranges(r0, r1):
        spans = [(c, max(c * sb, r0), min((c + 1) * sb, r1)) for c in range(nc)]
        return [(c, lo - c * sb, lo, hi) for c, lo, hi in spans if lo < hi]

    for r0 in range(0, blk, SOFTMAX_ROWS):
        r1 = r0 + SOFTMAX_ROWS
        qrow = r0 + lax.broadcasted_iota(jnp.int32, (SOFTMAX_ROWS, 2 * blk), 0)
        kcol = lax.broadcasted_iota(jnp.int32, (SOFTMAX_ROWS, 2 * blk), 1)
        dist = nc * (qrow % sb - kcol % (2 * sb) + sb) + (qrow // sb - kcol // (2 * sb))
        band = (dist >= 0) & (dist <= blk)
        for u in range(rows // sb):
            valid = band
            if u == 0:
                valid = band & ((kcol % (2 * sb) >= sb) | (n > 0))
            lse_tile = jnp.zeros((SOFTMAX_ROWS, LANES), F32)
            for hh in range(n_heads):
                sl = slice(hh * HEAD_DIM, (hh + 1) * HEAD_DIM)
                q_parts = [q_ref[c, u * sb + off:u * sb + off + hi - lo, sl]
                           for c, off, lo, hi in slab_ranges(r0, r1)]
                qh = q_parts[0] if len(q_parts) == 1 else jnp.concatenate(q_parts, axis=0)
                s = lax.dot_general(qh, keys(kk_ref, u, sl), (((1,), (1,)), ((), ())),
                                    preferred_element_type=F32)
                s = jnp.where(valid, s, -jnp.inf)
                m = jnp.max(s, axis=1, keepdims=True)
                p = jnp.exp2((s - m) * (scale * LOG2_E))
                den = jnp.sum(p, axis=1, keepdims=True)
                o = jnp.dot(p.astype(BF16), keys(vv_ref, u, sl), preferred_element_type=F32) / den
                for c, off, lo, hi in slab_ranges(r0, r1):
                    o_ref[c, u * sb + off:u * sb + off + hi - lo, sl] = o[lo - r0:hi - r0]
                lse_tile = jnp.where(lane == hh, m * scale + jnp.log(den), lse_tile)
            for c, off, lo, hi in slab_ranges(r0, r1):
                lse_ref[c, u * sb + off:u * sb + off + hi - lo, :] = lse_tile[lo - r0:hi - r0]


def _attn_prompt(q, k, v, dil):
    b, r_all, length, d = q.shape
    nc = r_all // dil
    sb = ATTN_BLOCK // nc
    rows = ATTN_QUERIES // nc
    view = lambda a: a.reshape(b, nc, dil, length, a.shape[-1])
    cur = pl.BlockSpec((None, nc, None, rows, d), lambda i, r, n: (i, 0, r, n, 0))
    prev = pl.BlockSpec((None, nc, None, sb, d),
                        lambda i, r, n: (i, 0, r, jnp.maximum(n * (rows // sb) - 1, 0), 0))
    o, lse = pl.pallas_call(
        _attn_prompt_kernel, grid=(b, dil, length // rows),
        in_specs=[cur, cur, prev, cur, prev],
        out_specs=[cur, pl.BlockSpec((None, nc, None, rows, LANES), lambda i, r, n: (i, 0, r, n, 0))],
        out_shape=[jax.ShapeDtypeStruct((b, nc, dil, length, d), F32),
                   jax.ShapeDtypeStruct((b, nc, dil, length, LANES), F32)],
        scratch_shapes=[pltpu.VMEM((nc, rows + sb, d), BF16), pltpu.VMEM((nc, rows + sb, d), BF16)],
        compiler_params=_params(3), name=f"attn_prompt_d{dil}")(view(q), view(k), view(k), view(v), view(v))
    return o.reshape(b, r_all, length, d), lse.reshape(b, r_all, length, LANES)


def _ffn_kernel(x_ref, g_ref, win_ref, wout_ref, o_ref, h_ref, xn_ref):
    _normalise_into(xn_ref, x_ref, g_ref)
    _swiglu_half(x_ref, xn_ref, win_ref, wout_ref, h_ref, _emit_to(o_ref))


def _ffn_final_kernel(x_ref, g_ref, win_ref, wout_ref, gf_ref, o_ref, h_ref, xn_ref):
    _ffn_kernel(x_ref, g_ref, win_ref, wout_ref, o_ref, h_ref, xn_ref)
    o_ref[...] = _rms(o_ref[...], gf_ref[...])


def _ffn_sample(x, ffn, final_g=None):
    n, d = x.shape
    g, w_in, w_out = ffn
    in_specs = [_full_spec((n, d)), _full_spec((1, d)), _full_spec(w_in.shape), _full_spec(w_out.shape)]
    args = [x, g, w_in, w_out]
    body = _ffn_kernel
    if final_g is not None:
        in_specs.append(_full_spec((1, d)))
        args.append(final_g)
        body = _ffn_final_kernel
    return pl.pallas_call(
        body, grid=(1,), in_specs=in_specs, out_specs=_full_spec((n, d)),
        out_shape=jax.ShapeDtypeStruct((n, d), F32),
        scratch_shapes=[pltpu.VMEM((n, w_out.shape[0]), BF16), pltpu.VMEM((n, d), BF16)],
        compiler_params=_params(1), name="ffn_sample")(*args)


def _pool_sample_kernel(x_ref, st_ref, g_ref, wg_ref, scale_ref, wo_ref, o_ref, new_ref, z_ref):
    n, d = x_ref.shape
    gd = d // len(POOL_WINDOWS)
    x = x_ref[...]
    h = _rms(x, g_ref[...])
    for gi, w in enumerate(POOL_WINDOWS):
        c0, c1 = gi * gd, (gi + 1) * gd
        cur = h[:, c0:c1]
        tot = cur
        for s in range(1, w):
            tot = tot + st_ref[POOL_BUF - s, :, c0:c1]
        p = (tot * (1.0 / w) - cur).astype(BF16)
        z_ref[:, c0:c1] = (jnp.dot(p, wg_ref[gi], preferred_element_type=F32) * scale_ref[:, c0:c1]).astype(BF16)
    o_ref[...] = x + jnp.dot(z_ref[...], wo_ref[...], preferred_element_type=F32)
    for r in range(POOL_BUF - 1):
        new_ref[r] = st_ref[r + 1]
    new_ref[POOL_BUF - 1] = h


def _pool_sample(x, st, g, w_group, scale, w_o):
    n, d = x.shape
    return pl.pallas_call(
        _pool_sample_kernel, grid=(1,),
        in_specs=[_full_spec((n, d)), _full_spec(st.shape), _full_spec((1, d)), _full_spec(w_group.shape),
                  _full_spec((1, d)), _full_spec(w_o.shape)],
        out_specs=[_full_spec((n, d)), _full_spec(st.shape)],
        out_shape=[jax.ShapeDtypeStruct((n, d), F32), jax.ShapeDtypeStruct(st.shape, F32)],
        scratch_shapes=[pltpu.VMEM((n, d), BF16)],
        compiler_params=_params(1), name="pool_sample")(x, st, g, w_group, scale, w_o)


def _proj_sample_kernel(x_ref, g_ref, w_ref, c_ref, a_ref, b_ref, y_ref, xn_ref, *, n_rope_heads):
    xn_ref[...] = _rms(x_ref[...], g_ref[...]).astype(BF16)
    n_slots = y_ref.shape[0]
    stored_in_place = lambda hh: None
    jobs = _projection_jobs(xn_ref, w_ref, 0, (c_ref, a_ref, b_ref), y_ref.at[0:n_rope_heads], stored_in_place)
    if n_rope_heads < n_slots:
        jobs += _projection_jobs(xn_ref, w_ref, n_rope_heads * HEAD_DIM, None, y_ref.at[n_rope_heads:n_slots],
                                 stored_in_place)
    for job in jobs:
        job()


def _proj_sample(x, g, w, tables, n_rope_heads):
    n, d = x.shape
    n_slots = w.shape[1] // HEAD_DIM
    return pl.pallas_call(
        functools.partial(_proj_sample_kernel, n_rope_heads=n_rope_heads), grid=(1,),
        in_specs=[_full_spec((n, d)), _full_spec((1, d)), _full_spec(w.shape)]
        + [_full_spec((n, HEAD_DIM))] * 3,
        out_specs=_full_spec((n_slots, n, HEAD_DIM)),
        out_shape=jax.ShapeDtypeStruct((n_slots, n, HEAD_DIM), F32),
        scratch_shapes=[pltpu.VMEM((n, d), BF16)],
        compiler_params=_params(1), name="proj_sample")(x, g, w, *tables)


def _attn_sample_kernel(*refs):
    q_ref, kn_ref, vn_ref = refs[:3]
    kc_refs = refs[3:3 + N_DIL]
    vc_refs = refs[3 + N_DIL:3 + 2 * N_DIL]
    att_ref = refs[3 + 2 * N_DIL]
    scale = HEAD_DIM ** -0.5
    kn, vn = kn_ref[...], vn_ref[...]
    outs, lses = [], []
    for g in range(N_DIL):
        qg = q_ref[g]
        s_past = jnp.sum(kc_refs[g][...] * qg[None], axis=2, keepdims=True) * scale
        s_new = jnp.sum(kn * qg, axis=1, keepdims=True) * scale
        m = jnp.maximum(jnp.max(s_past, axis=0), s_new)
        p_past = jnp.exp(s_past - m[None])
        p_new = jnp.exp(s_new - m)
        den = jnp.sum(p_past, axis=0) + p_new
        o = jnp.sum(p_past * vc_refs[g][...], axis=0) + p_new * vn
        outs.append(o / den)
        lses.append(m + jnp.log(den))
    top = functools.reduce(jnp.maximum, lses)
    es = [jnp.exp(l - top) for l in lses]
    inv = 1.0 / functools.reduce(jnp.add, es)
    att_ref[...] = functools.reduce(jnp.add, [e * inv * o for e, o in zip(es, outs)])


def _attn_sample(q, k_new, v_new, cache_k, cache_v):
    b, w, h, e = cache_k.shape
    cache_args, cache_specs = [], []
    for cache in (cache_k, cache_v):
        for win, dil in DILATED_GROUPS:
            n_keys = win // dil
            last = w // dil // n_keys - 1
            cache_args.append(cache.reshape(b, w // dil, dil, h, e))
            cache_specs.append(pl.BlockSpec((None, n_keys, None, h, e),
                                            lambda i, last=last: (i, last, 0, 0, 0)))
    head = pl.BlockSpec((None, h, e), lambda i: (i, 0, 0))
    return pl.pallas_call(
        _attn_sample_kernel, grid=(b,),
        in_specs=[pl.BlockSpec((None, N_DIL, h, e), lambda i: (i, 0, 0, 0)), head, head] + cache_specs,
        out_specs=head, out_shape=jax.ShapeDtypeStruct((b, h, e), F32),
        compiler_params=_params(1), name="attn_sample")(q, k_new, v_new, *cache_args)


def _dense_residual_kernel(a_ref, x_ref, w_ref, o_ref):
    o_ref[...] = x_ref[...] + jnp.dot(a_ref[...].astype(BF16), w_ref[...], preferred_element_type=F32)


def _dense_residual(a, x, w):
    n, d = x.shape
    return pl.pallas_call(
        _dense_residual_kernel, grid=(1,),
        in_specs=[_full_spec(a.shape), _full_spec((n, d)), _full_spec(w.shape)], out_specs=_full_spec((n, d)),
        out_shape=jax.ShapeDtypeStruct((n, d), F32),
        compiler_params=_params(1), name="dense_residual")(a, x, w)


def kernel(x_prompt, x_sample, state_pool, cache_k, cache_v, ffn1_norm, ffn1_w_in, ffn1_w_out, mix_norm,
           ffn2_norm, ffn2_w_in, ffn2_w_out, pool_w_group, pool_scale, pool_w_o, kv_norm, w_kv, attn_w_q,
           attn_w_o, final_norm):
    b, s, d = x_prompt.shape
    bs = x_sample.shape[0]
    assert x_sample.shape[1] == 1
    n_heads = d // HEAD_DIM
    w_cache = cache_k.shape[1]
    assert w_cache == min(WINDOW_MAX, PAST_LEN)
    keep = min(WINDOW_MAX, s)

    bf = lambda a: a.astype(BF16)
    vec = lambda a: a.reshape(1, d)
    ffn1 = [(vec(ffn1_norm[l]), bf(ffn1_w_in[l]), bf(ffn1_w_out[l])) for l in range(2)]
    ffn2 = [(vec(ffn2_norm[l]), bf(ffn2_w_in[l]), bf(ffn2_w_out[l])) for l in range(2)]
    wg, wpo, wkv, wq, wo = bf(pool_w_group[0]), bf(pool_w_o[0]), bf(w_kv), bf(attn_w_q[0]), bf(attn_w_o[0])

    tab_p = _rope_tables(jnp.arange(s, dtype=jnp.int32))
    tab_s = _rope_tables(jnp.full((bs,), PAST_LEN, dtype=jnp.int32))

    xs = _ffn_sample(x_sample.reshape(bs, d), ffn1[0])
    hist = jnp.swapaxes(state_pool[0], 0, 1)
    xs, hist_new = _pool_sample(xs, hist, vec(mix_norm[0]), wg, vec(pool_scale[0]), wpo)
    pool_s = jnp.swapaxes(hist_new, 0, 1)[None]
    xs = _ffn_sample(xs, ffn2[0])
    kv_new = jnp.swapaxes(_proj_sample(xs, vec(kv_norm), wkv, tab_s, n_heads), 0, 1)
    k_new, v_new = kv_new[:, :n_heads], kv_new[:, n_heads:]
    xs = _ffn_sample(xs, ffn1[1])
    q_new = jnp.swapaxes(_proj_sample(xs, vec(mix_norm[1]), wq, tab_s, N_DIL * n_heads), 0, 1)
    att = _attn_sample(q_new.reshape(bs, N_DIL, n_heads, HEAD_DIM), k_new, v_new, cache_k, cache_v)
    xs = _dense_residual(att.reshape(bs, d), xs, wo)
    y_sample = _ffn_sample(xs, ffn2[1], vec(final_norm)).reshape(bs, 1, d)

    x = x_prompt.reshape(b * s, d)
    x, xn, pool_hist, k_s = _layer0_head(x, s, ffn1[0], vec(mix_norm[0]), wg, vec(pool_scale[0]), wpo,
                                         ffn2[0][0], cache_k, k_new[:, None])
    pool_p = pool_hist[:, HALO - POOL_BUF:][None]
    x, xn, k_nat, v_nat, k_str, v_str, k_keep, v_keep, v_s = _layer0_tail(
        x, xn, s, keep, ffn2[0], vec(kv_norm), wkv, ffn1[1][0], tab_p, cache_v, v_new[:, None])
    x, *qs_by_group = _layer1_head(x, xn, s, ffn1[1], vec(mix_norm[1]), wq, tab_p)
    outs, lses = [], []
    for (win, dil), q in zip(DILATED_GROUPS, qs_by_group):
        assert win // dil == ATTN_BLOCK
        if dil == 1:
            nat = lambda a: a.reshape(b, 1, s, d)
            o, lse = _attn_prompt(nat(q), nat(k_nat), nat(v_nat), 1)
            outs.append(o.reshape(b * s, d))
            lses.append(lse.reshape(b * s, LANES))
        else:
            o, lse = _attn_prompt(q, k_str, v_str, dil)
            outs.append(o)
            lses.append(lse)
    y_prompt = _layer1_tail(outs, lses, x, s, wo, ffn2[1], vec(final_norm)).reshape(b, s, d)
    k_p = k_keep.reshape(b, keep, n_heads, HEAD_DIM)
    v_p = v_keep.reshape(b, keep, n_heads, HEAD_DIM)

    return (y_prompt, y_sample, pool_p, pool_s, k_p, v_p, k_s, v_s)
```

```python
import functools

import jax
import jax.numpy as jnp
from jax import lax
from jax.experimental import pallas as pl
from jax.experimental.pallas import tpu as pltpu

F32 = jnp.float32
BF16 = jnp.bfloat16

POOL_WINDOWS = (2, 4, 8, 16)
POOL_BUF = max(POOL_WINDOWS) - 1
HEAD_DIM = 128
DILATED_GROUPS = ((128, 1), (512, 4), (2048, 16))
N_DIL = len(DILATED_GROUPS)
WINDOW_MAX = max(w for w, _ in DILATED_GROUPS)
STRIDE = max(d for _, d in DILATED_GROUPS)
ATTN_BLOCK = 128
ROT_DIM = HEAD_DIM // 4
ROPE_THETA = 500000.0
RMS_EPS = 1e-6
PAST_LEN = 16384
LOG2_E = 1.4426950408889634

V7X_VMEM_BYTES = 64 * 1024 * 1024
VMEM_LIMIT = (V7X_VMEM_BYTES * 29) // 32
LANES = 128
MXU_WIDTH = 256
HALO = 16

ROW_TILE = 256
BIG_TILE = 512
ATTN_QUERIES = 1024
SOFTMAX_ROWS = 128


def _const_spec(shape):
    n = len(shape)
    return pl.BlockSpec(shape, lambda *_: (0,) * n, pipeline_mode=pl.Buffered(1))


def _full_spec(shape):
    return pl.BlockSpec(shape, lambda i: (0,) * len(shape))


def _params(n_axes):
    return pltpu.CompilerParams(dimension_semantics=("arbitrary",) * n_axes,
                                vmem_limit_bytes=VMEM_LIMIT)


def _rms(x, g):
    return x * lax.rsqrt(jnp.mean(x * x, axis=-1, keepdims=True) + RMS_EPS) * g


def _col_chunks(width):
    return [(c, c + MXU_WIDTH) for c in range(0, width, MXU_WIDTH)]


def _normalise_into(dst_ref, src_ref, g_ref):
    dst_ref[...] = _rms(src_ref[...], g_ref[...]).astype(dst_ref.dtype)


def _swiglu_half(src_ref, xn_ref, win_ref, wout_ref, h_ref, emit, side_jobs=()):
    d_ff, d = wout_ref.shape
    ff_chunks, out_chunks = _col_chunks(d_ff), _col_chunks(d)
    n_seg = len(ff_chunks) + len(out_chunks)
    jobs = list(side_jobs)
    done = [0, 0]

    def end_segment():
        done[0] += 1
        while done[1] < len(jobs) and done[1] * n_seg < done[0] * len(jobs):
            jobs[done[1]]()
            done[1] += 1

    for c0, c1 in ff_chunks:
        gate = jnp.dot(xn_ref[...], win_ref[:, c0:c1], preferred_element_type=F32)
        up = jnp.dot(xn_ref[...], win_ref[:, d_ff + c0:d_ff + c1], preferred_element_type=F32)
        h_ref[:, c0:c1] = (gate * jax.nn.sigmoid(gate) * up).astype(BF16)
        end_segment()
    for c0, c1 in out_chunks:
        emit(c0, c1, src_ref[:, c0:c1] + 0.5 * jnp.dot(h_ref[...], wout_ref[:, c0:c1],
                                                        preferred_element_type=F32))
        end_segment()


def _pool_trailing(ext_ref, row0, rows, j, gi, w, gd):
    c0, c1 = gi * gd, (gi + 1) * gd
    cur = ext_ref[row0:row0 + rows, c0:c1]
    tot = cur
    for s in range(1, w):
        tot = tot + ext_ref[row0 - s:row0 - s + rows, c0:c1]
    if j is None:
        return tot * (1.0 / w) - cur
    pos1 = (j * rows + 1 + lax.broadcasted_iota(jnp.int32, (rows, 1), 0)).astype(F32)
    return tot * (1.0 / jnp.minimum(pos1, float(w))) - cur


def _pool_jobs(src_ref, j, g_ref, wg_ref, scale_ref, wo_ref, ext_ref, z_ref, o_ref, state_ref, g_next_ref,
               xn_next_ref):
    tm, d = src_ref.shape
    gd = d // len(POOL_WINDOWS)

    def normalise():
        h = _rms(src_ref[...], g_ref[...])
        ext_ref[0:HALO, :] = jnp.where(j == 0, 0.0, ext_ref[tm:tm + HALO, :])
        ext_ref[HALO:HALO + tm, :] = h
        state_ref[...] = h[tm - HALO:, :]

    def group(gi, w):
        p = _pool_trailing(ext_ref, HALO, tm, j, gi, w, gd).astype(BF16)
        z = jnp.dot(p, wg_ref[gi], preferred_element_type=F32) * scale_ref[:, gi * gd:(gi + 1) * gd]
        z_ref[:, gi * gd:(gi + 1) * gd] = z.astype(BF16)

    def project(c0, c1):
        o_ref[:, c0:c1] = src_ref[:, c0:c1] + jnp.dot(z_ref[...], wo_ref[:, c0:c1],
                                                      preferred_element_type=F32)

    return ([normalise] + [functools.partial(group, gi, w) for gi, w in enumerate(POOL_WINDOWS)]
            + [functools.partial(project, c0, c1) for c0, c1 in _col_chunks(d)]
            + [functools.partial(_normalise_into, xn_next_ref, o_ref, g_next_ref)])


def _rope_tables(pos):
    half = ROT_DIM // 2
    inv_freq = ROPE_THETA ** (-jnp.arange(0, ROT_DIM, 2, dtype=F32) / ROT_DIM)
    ang = pos.astype(F32)[:, None] * inv_freq[None, :]
    cos, sin = jnp.cos(ang), jnp.sin(ang)
    t = pos.shape[0]
    c = jnp.concatenate([cos, cos, jnp.ones((t, HEAD_DIM - ROT_DIM), F32)], axis=1)
    a = jnp.concatenate([-sin, jnp.zeros((t, HEAD_DIM - half), F32)], axis=1)
    b = jnp.concatenate([jnp.zeros((t, half), F32), sin, jnp.zeros((t, HEAD_DIM - ROT_DIM), F32)], axis=1)
    return c, a, b


def _projection_jobs(xn_ref, w_ref, col0, table_refs, y_ref, store_head):
    n_slots = y_ref.shape[0]
    half = ROT_DIM // 2
    per_dot = MXU_WIDTH // HEAD_DIM

    def project(h0):
        cols = slice(col0 + h0 * HEAD_DIM, col0 + (h0 + per_dot) * HEAD_DIM)
        y = jnp.dot(xn_ref[...], w_ref[:, cols], preferred_element_type=F32)
        for k in range(per_dot):
            y_ref[h0 + k] = y[:, k * HEAD_DIM:(k + 1) * HEAD_DIM]

    def finish(hh):
        if table_refs is not None:
            c, a, b = (t[...] for t in table_refs)
            yh = y_ref[hh]
            y_ref[hh] = yh * c + pltpu.roll(yh, HEAD_DIM - half, 1) * a + pltpu.roll(yh, half, 1) * b
        store_head(hh)

    jobs = []
    for h0 in range(0, n_slots, per_dot):
        jobs.append(functools.partial(project, h0))
        jobs += [functools.partial(finish, h0 + k) for k in range(per_dot)]
    return jobs


def _store_head_natural(src_ref, hh, dst_ref):
    dst_ref[:, hh * HEAD_DIM:(hh + 1) * HEAD_DIM] = src_ref[hh].astype(dst_ref.dtype)


def _store_head_strided(src_ref, hh, dst_ref):
    per = src_ref.shape[1] // STRIDE
    for r in range(STRIDE):
        dst_ref[r, :, hh * HEAD_DIM:(hh + 1) * HEAD_DIM] = (
            src_ref[hh, pl.ds(r, per, stride=STRIDE), :].astype(dst_ref.dtype))


def _roll_tile(seq_end, src_ref, nxt_ref, new_ref, dst_ref):
    t = src_ref.shape[0]
    dst_ref[0:t - 1] = src_ref[1:t]
    dst_ref[t - 1] = jnp.where(seq_end, new_ref[0], nxt_ref[0])


def _combine_jobs(o_refs, lse_refs, x_ref, wo_ref, o_nat, lse_nat, w_ref, att_ref, dst_ref, g_next_ref,
                  xn_next_ref):
    tm, d = x_ref.shape
    per = tm // STRIDE
    n_heads = d // HEAD_DIM
    heads, lses = [], []
    jobs = []
    slot = 0
    for (_, dil), o_ref, lse_ref in zip(DILATED_GROUPS, o_refs, lse_refs):
        if dil == 1:
            heads.append(functools.partial(
                lambda hh, ref: ref[:, hh * HEAD_DIM:(hh + 1) * HEAD_DIM].astype(F32), ref=o_ref))
            lses.append(functools.partial(lambda ref: ref[...], lse_ref))
            continue

        def to_natural(r, o_ref=o_ref, lse_ref=lse_ref, slot=slot):
            lse_nat[slot, pl.ds(r, per, stride=STRIDE), :] = lse_ref[r]
            for hh in range(n_heads):
                o_nat[slot * n_heads + hh, pl.ds(r, per, stride=STRIDE), :] = (
                    o_ref[r, :, hh * HEAD_DIM:(hh + 1) * HEAD_DIM].astype(F32))

        jobs += [functools.partial(to_natural, r) for r in range(STRIDE)]
        heads.append(functools.partial(lambda hh, s0: o_nat[s0 + hh], s0=slot * n_heads))
        lses.append(functools.partial(lambda s: lse_nat[s], slot))
        slot += 1

    def weights():
        ls = [l() for l in lses]
        top = functools.reduce(jnp.maximum, ls)
        es = [jnp.exp(l - top) for l in ls]
        inv = 1.0 / functools.reduce(jnp.add, es)
        for g, e in enumerate(es):
            w_ref[g] = e * inv

    def mix_head(hh):
        att = functools.reduce(jnp.add, [w_ref[g][:, hh:hh + 1] * heads[g](hh) for g in range(N_DIL)])
        att_ref[:, hh * HEAD_DIM:(hh + 1) * HEAD_DIM] = att.astype(BF16)

    def project(c0, c1):
        dst_ref[:, c0:c1] = x_ref[:, c0:c1] + jnp.dot(att_ref[...], wo_ref[:, c0:c1],
                                                      preferred_element_type=F32)

    return (jobs + [weights] + [functools.partial(mix_head, hh) for hh in range(n_heads)]
            + [functools.partial(project, c0, c1) for c0, c1 in _col_chunks(d)]
            + [functools.partial(_normalise_into, xn_next_ref, dst_ref, g_next_ref)])


class _Tiling:
    def __init__(self, n_rows, seq, tm):
        self.tm = tm
        self.n_tiles = n_rows // tm
        self.per_seq = seq // tm
        self.steps = self.n_tiles + 1

    def lead(self, i):
        return jnp.minimum(i, self.n_tiles - 1)

    def trail(self, i):
        return jnp.maximum(i - 1, 0)

    def row(self, width, tile_of):
        return pl.BlockSpec((self.tm, width), lambda i: (tile_of(i), 0))

    def strided(self, width, tile_of):
        return pl.BlockSpec((None, STRIDE, self.tm // STRIDE, width),
                            lambda i: (tile_of(i) // self.per_seq, 0, tile_of(i) % self.per_seq, 0))

    def table(self, tile_of):
        return pl.BlockSpec((self.tm, HEAD_DIM), lambda i: (tile_of(i) % self.per_seq, 0))


def _roll_operands(cache, new, tl):
    b, w, h, e = cache.shape
    per = tl.n_tiles // b
    t = w // per
    blk = pl.BlockSpec((None, t, h, e), lambda i: (tl.lead(i) // per, tl.lead(i) % per, 0, 0))
    nxt = pl.BlockSpec((None, 1, h, e),
                       lambda i: (tl.lead(i) // per, jnp.minimum((tl.lead(i) % per + 1) * t, w - 1), 0, 0))
    cur_new = pl.BlockSpec((None, 1, h, e), lambda i: (tl.lead(i) // per, 0, 0, 0))
    return per, [blk, nxt, cur_new], [cache, cache, new], blk


def _zero_on_first_step(i, *refs):
    @pl.when(i == 0)
    def _():
        for ref in refs:
            ref[...] = jnp.zeros(ref.shape, ref.dtype)


def _emit_to(*refs):
    def emit(c0, c1, value):
        for ref in refs:
            ref[:, c0:c1] = value
    return emit


def _layer0_head_kernel(x_ref, g1_ref, win_ref, wout_ref, gm_ref, wg_ref, scale_ref, wo_ref, gn_ref,
                        src_ref, nxt_ref, new_ref, o_ref, on_ref, state_ref, dst_ref,
                        h_ref, xn_ref, xs_ref, ext_ref, z_ref, *, tl, roll_per_seq):
    i = pl.program_id(0)
    _zero_on_first_step(i, xs_ref, ext_ref)
    jobs = _pool_jobs(xs_ref.at[(i + 1) % 2], tl.trail(i) % tl.per_seq, gm_ref, wg_ref, scale_ref, wo_ref,
                      ext_ref, z_ref, o_ref, state_ref, gn_ref, on_ref)
    jobs.insert(len(jobs) // 2, functools.partial(
        _roll_tile, tl.lead(i) % roll_per_seq == roll_per_seq - 1, src_ref, nxt_ref, new_ref, dst_ref))
    _normalise_into(xn_ref, x_ref, g1_ref)
    _swiglu_half(x_ref, xn_ref, win_ref, wout_ref, h_ref, _emit_to(xs_ref.at[i % 2]), jobs)


def _layer0_head(x, seq, ffn, g_mix, w_group, scale, w_o, g_next, cache, new):
    n, d = x.shape
    tl = _Tiling(n, seq, BIG_TILE)
    g1, w_in, w_out = ffn
    roll_per_seq, roll_specs, roll_args, roll_out = _roll_operands(cache, new, tl)
    return pl.pallas_call(
        functools.partial(_layer0_head_kernel, tl=tl, roll_per_seq=roll_per_seq), grid=(tl.steps,),
        in_specs=[tl.row(d, tl.lead), _const_spec((1, d)), _const_spec(w_in.shape), _const_spec(w_out.shape),
                  _const_spec((1, d)), _const_spec(w_group.shape), _const_spec((1, d)), _const_spec(w_o.shape),
                  _const_spec((1, d))] + roll_specs,
        out_specs=[tl.row(d, tl.trail), tl.row(d, tl.trail),
                   pl.BlockSpec((None, HALO, d), lambda i: (tl.trail(i) // tl.per_seq, 0, 0)), roll_out],
        out_shape=[jax.ShapeDtypeStruct((n, d), F32), jax.ShapeDtypeStruct((n, d), BF16),
                   jax.ShapeDtypeStruct((n // seq, HALO, d), F32),
                   jax.ShapeDtypeStruct(cache.shape, cache.dtype)],
        scratch_shapes=[pltpu.VMEM((tl.tm, w_out.shape[0]), BF16), pltpu.VMEM((tl.tm, d), BF16),
                        pltpu.VMEM((2, tl.tm, d), F32), pltpu.VMEM((tl.tm + HALO, d), F32),
                        pltpu.VMEM((tl.tm, d), BF16)],
        compiler_params=_params(1), name="layer0_head")(
            x, g1, w_in, w_out, g_mix, w_group, scale, w_o, g_next, *roll_args)


def _layer0_tail_kernel(x_ref, xn_ref, win_ref, wout_ref, gkv_ref, wkv_ref, gn_ref, c_ref, a_ref, b_ref,
                        src_ref, nxt_ref, new_ref,
                        o_ref, on_ref, k_ref, v_ref, ks_ref, vs_ref, kf_ref, vf_ref, dst_ref,
                        h_ref, xs_ref, xkv_ref, yk_ref, yv_ref, *, tl, roll_per_seq, kept_tiles):
    i = pl.program_id(0)
    _zero_on_first_step(i, xs_ref)
    d = x_ref.shape[1]

    def normalise():
        x = xs_ref[(i + 1) % 2]
        y = x * lax.rsqrt(jnp.mean(x * x, axis=-1, keepdims=True) + RMS_EPS)
        xkv_ref[...] = (y * gkv_ref[...]).astype(BF16)
        on_ref[...] = (y * gn_ref[...]).astype(BF16)

    def store_both(y_ref, nat_ref, str_ref, hh):
        _store_head_natural(y_ref, hh, nat_ref)
        _store_head_strided(y_ref, hh, str_ref)

    jobs = [normalise]
    jobs += _projection_jobs(xkv_ref, wkv_ref, 0, (c_ref, a_ref, b_ref), yk_ref,
                             functools.partial(store_both, yk_ref, k_ref, ks_ref))
    jobs += _projection_jobs(xkv_ref, wkv_ref, d, None, yv_ref,
                             functools.partial(store_both, yv_ref, v_ref, vs_ref))
    jobs.insert(len(jobs) // 2, functools.partial(
        _roll_tile, tl.lead(i) % roll_per_seq == roll_per_seq - 1, src_ref, nxt_ref, new_ref, dst_ref))
    _swiglu_half(x_ref, xn_ref, win_ref, wout_ref, h_ref, _emit_to(o_ref, xs_ref.at[i % 2]), jobs)

    @pl.when(tl.trail(i) % tl.per_seq >= tl.per_seq - kept_tiles)
    def _():
        for hh in range(yk_ref.shape[0]):
            kf_ref[:, hh, :] = yk_ref[hh]
            vf_ref[:, hh, :] = yv_ref[hh]


def _layer0_tail(x, xn, seq, keep, ffn, g_kv, w_kv, g_next, tables, cache, new):
    n, d = x.shape
    b = n // seq
    n_heads = d // HEAD_DIM
    tl = _Tiling(n, seq, ROW_TILE)
    _, w_in, w_out = ffn
    kept = keep // tl.tm
    roll_per_seq, roll_specs, roll_args, roll_out = _roll_operands(cache, new, tl)

    def kept_block(i):
        t = tl.trail(i)
        return ((t // tl.per_seq) * kept + jnp.maximum(t % tl.per_seq - (tl.per_seq - kept), 0), 0, 0)

    kept_spec = pl.BlockSpec((tl.tm, n_heads, HEAD_DIM), kept_block)
    nat_shape = jax.ShapeDtypeStruct((n, d), BF16)
    str_shape = jax.ShapeDtypeStruct((b, STRIDE, seq // STRIDE, d), BF16)
    kept_shape = jax.ShapeDtypeStruct((b * keep, n_heads, HEAD_DIM), F32)
    head_scratch = pltpu.VMEM((n_heads, tl.tm, HEAD_DIM), F32)
    return pl.pallas_call(
        functools.partial(_layer0_tail_kernel, tl=tl, roll_per_seq=roll_per_seq, kept_tiles=kept),
        grid=(tl.steps,),
        in_specs=[tl.row(d, tl.lead), tl.row(d, tl.lead), _const_spec(w_in.shape), _const_spec(w_out.shape),
                  _const_spec((1, d)), _const_spec(w_kv.shape), _const_spec((1, d))]
        + [tl.table(tl.trail)] * 3 + roll_specs,
        out_specs=[tl.row(d, tl.lead), tl.row(d, tl.trail), tl.row(d, tl.trail), tl.row(d, tl.trail),
                   tl.strided(d, tl.trail), tl.strided(d, tl.trail), kept_spec, kept_spec, roll_out],
        out_shape=[jax.ShapeDtypeStruct((n, d), F32), nat_shape, nat_shape, nat_shape, str_shape, str_shape,
                   kept_shape, kept_shape, jax.ShapeDtypeStruct(cache.shape, cache.dtype)],
        scratch_shapes=[pltpu.VMEM((tl.tm, w_out.shape[0]), BF16), pltpu.VMEM((2, tl.tm, d), F32),
                        pltpu.VMEM((tl.tm, d), BF16), head_scratch, head_scratch],
        compiler_params=_params(1), name="layer0_tail")(
            x, xn, w_in, w_out, g_kv, w_kv, g_next, *tables, *roll_args)


def _layer1_head_kernel(x_ref, xn_ref, win_ref, wout_ref, gm_ref, wq_ref, c_ref, a_ref, b_ref, o_ref, *refs, tl):
    q_refs, (h_ref, xs_ref, xq_ref, y_ref) = refs[:N_DIL], refs[N_DIL:]
    i = pl.program_id(0)
    _zero_on_first_step(i, xs_ref)
    d = x_ref.shape[1]
    n_heads = d // HEAD_DIM

    def normalise():
        xq_ref[...] = _rms(xs_ref[(i + 1) % 2], gm_ref[...]).astype(BF16)

    jobs = [normalise]
    for g, ((_, dil), q_ref) in enumerate(zip(DILATED_GROUPS, q_refs)):
        yg_ref = y_ref.at[g * n_heads:(g + 1) * n_heads]
        store = _store_head_natural if dil == 1 else _store_head_strided
        jobs += _projection_jobs(xq_ref, wq_ref, g * d, (c_ref, a_ref, b_ref), yg_ref,
                                 functools.partial(lambda hh, st, src, dst: st(src, hh, dst),
                                                   st=store, src=yg_ref, dst=q_ref))
    _swiglu_half(x_ref, xn_ref, win_ref, wout_ref, h_ref, _emit_to(o_ref, xs_ref.at[i % 2]), jobs)


def _layer1_head(x, xn, seq, ffn, g_mix, w_q, tables):
    n, d = x.shape
    b = n // seq
    tl = _Tiling(n, seq, BIG_TILE)
    _, w_in, w_out = ffn
    q_specs, q_shapes = [], []
    for _, dil in DILATED_GROUPS:
        q_specs.append(tl.row(d, tl.trail) if dil == 1 else tl.strided(d, tl.trail))
        q_shapes.append(jax.ShapeDtypeStruct((n, d) if dil == 1 else (b, STRIDE, seq // STRIDE, d), BF16))
    return pl.pallas_call(
        functools.partial(_layer1_head_kernel, tl=tl), grid=(tl.steps,),
        in_specs=[tl.row(d, tl.lead), tl.row(d, tl.lead), _const_spec(w_in.shape), _const_spec(w_out.shape),
                  _const_spec((1, d)), _const_spec(w_q.shape)] + [tl.table(tl.trail)] * 3,
        out_specs=[tl.row(d, tl.lead)] + q_specs,
        out_shape=[jax.ShapeDtypeStruct((n, d), F32)] + q_shapes,
        scratch_shapes=[pltpu.VMEM((tl.tm, w_out.shape[0]), BF16), pltpu.VMEM((2, tl.tm, d), F32),
                        pltpu.VMEM((tl.tm, d), BF16),
                        pltpu.VMEM((w_q.shape[1] // HEAD_DIM, tl.tm, HEAD_DIM), F32)],
        compiler_params=_params(1), name="layer1_head")(x, xn, w_in, w_out, g_mix, w_q, *tables)


def _layer1_tail_kernel(*refs):
    o_refs, lse_refs = refs[:N_DIL], refs[N_DIL:2 * N_DIL]
    (x_ref, wo_ref, g2_ref, win_ref, wout_ref, gf_ref, out_ref,
     h_ref, xs_ref, xns_ref, o_nat, lse_nat, w_ref, att_ref) = refs[2 * N_DIL:]
    i = pl.program_id(0)
    _zero_on_first_step(i, xs_ref, xns_ref)
    jobs = _combine_jobs(o_refs, lse_refs, x_ref, wo_ref, o_nat, lse_nat, w_ref, att_ref,
                         xs_ref.at[i % 2], g2_ref, xns_ref.at[i % 2])
    _swiglu_half(xs_ref.at[(i + 1) % 2], xns_ref.at[(i + 1) % 2], win_ref, wout_ref, h_ref,
                 _emit_to(out_ref), jobs)
    out_ref[...] = _rms(out_ref[...], gf_ref[...])


def _layer1_tail(outs, lses, x, seq, w_o, ffn, g_final):
    n, d = x.shape
    tl = _Tiling(n, seq, BIG_TILE)
    g2, w_in, w_out = ffn

    def group_specs(width):
        return [tl.row(width, tl.lead) if dil == 1 else tl.strided(width, tl.lead) for _, dil in DILATED_GROUPS]

    n_strided = sum(dil > 1 for _, dil in DILATED_GROUPS)
    return pl.pallas_call(
        _layer1_tail_kernel, grid=(tl.steps,),
        in_specs=group_specs(d) + group_specs(LANES)
        + [tl.row(d, tl.lead), _const_spec(w_o.shape), _const_spec((1, d)), _const_spec(w_in.shape),
           _const_spec(w_out.shape), _const_spec((1, d))],
        out_specs=tl.row(d, tl.trail), out_shape=jax.ShapeDtypeStruct((n, d), F32),
        scratch_shapes=[pltpu.VMEM((tl.tm, w_out.shape[0]), BF16), pltpu.VMEM((2, tl.tm, d), F32),
                        pltpu.VMEM((2, tl.tm, d), BF16),
                        pltpu.VMEM((n_strided * (d // HEAD_DIM), tl.tm, HEAD_DIM), F32),
                        pltpu.VMEM((n_strided, tl.tm, LANES), F32), pltpu.VMEM((N_DIL, tl.tm, LANES), F32),
                        pltpu.VMEM((tl.tm, d), BF16)],
        compiler_params=_params(1), name="layer1_tail")(*outs, *lses, x, w_o, g2, w_in, w_out, g_final)


def _attn_prompt_kernel(q_ref, kc_ref, kp_ref, vc_ref, vp_ref, o_ref, lse_ref, kk_ref, vv_ref):
    n = pl.program_id(2)
    nc, rows, d = q_ref.shape
    blk = ATTN_BLOCK
    sb = blk // nc
    n_heads = d // HEAD_DIM
    for c in range(nc):
        kk_ref[c, 0:sb, :] = kp_ref[c]
        kk_ref[c, sb:, :] = kc_ref[c]
        vv_ref[c, 0:sb, :] = vp_ref[c]
        vv_ref[c, sb:, :] = vc_ref[c]
    lane = lax.broadcasted_iota(jnp.int32, (SOFTMAX_ROWS, LANES), 1)
    scale = HEAD_DIM ** -0.5

    def keys(ref, u, sl):
        parts = [ref[c, u * sb:(u + 2) * sb, sl] for c in range(nc)]
        return parts[0] if nc == 1 else jnp.concatenate(parts, axis=0)

    def slab_ranges(r0, r1):
        spans = [(c, max(c * sb, r0), min((c + 1) * sb, r1)) for c in range(nc)]
        return [(c, lo - c * sb, lo, hi) for c, lo, hi in spans if lo < hi]

    for r0 in range(0, blk, SOFTMAX_ROWS):
        r1 = r0 + SOFTMAX_ROWS
        qrow = r0 + lax.broadcasted_iota(jnp.int32, (SOFTMAX_ROWS, 2 * blk), 0)
        kcol = lax.broadcasted_iota(jnp.int32, (SOFTMAX_ROWS, 2 * blk), 1)
        dist = nc * (qrow % sb - kcol % (2 * sb) + sb) + (qrow // sb - kcol // (2 * sb))
        band = (dist >= 0) & (dist <= blk)
        for u in range(rows // sb):
            valid = band
            if u == 0:
                valid = band & ((kcol % (2 * sb) >= sb) | (n > 0))
            lse_tile = jnp.zeros((SOFTMAX_ROWS, LANES), F32)
            for hh in range(n_heads):
                sl = slice(hh * HEAD_DIM, (hh + 1) * HEAD_DIM)
                q_parts = [q_ref[c, u * sb + off:u * sb + off + hi - lo, sl]
                           for c, off, lo, hi in slab_ranges(r0, r1)]
                qh = q_parts[0] if len(q_parts) == 1 else jnp.concatenate(q_parts, axis=0)
                s = lax.dot_general(qh, keys(kk_ref, u, sl), (((1,), (1,)), ((), ())),
                                    preferred_element_type=F32)
                s = jnp.where(valid, s, -jnp.inf)
                m = jnp.max(s, axis=1, keepdims=True)
                p = jnp.exp2((s - m) * (scale * LOG2_E))
                den = jnp.sum(p, axis=1, keepdims=True)
                o = jnp.dot(p.astype(BF16), keys(vv_ref, u, sl), preferred_element_type=F32) / den
                for c, off, lo, hi in slab_ranges(r0, r1):
                    o_ref[c, u * sb + off:u * sb + off + hi - lo, sl] = o[lo - r0:hi - r0].astype(o_ref.dtype)
                lse_tile = jnp.where(lane == hh, m * scale + jnp.log(den), lse_tile)
            for c, off, lo, hi in slab_ranges(r0, r1):
                lse_ref[c, u * sb + off:u * sb + off + hi - lo, :] = lse_tile[lo - r0:hi - r0]


def _attn_prompt(q, k, v, dil):
    b, r_all, length, d = q.shape
    nc = r_all // dil
    sb = ATTN_BLOCK // nc
    rows = min(ATTN_QUERIES // nc, length)
    view = lambda a: a.reshape(b, nc, dil, length, a.shape[-1])
    cur = pl.BlockSpec((None, nc, None, rows, d), lambda i, r, n: (i, 0, r, n, 0))
    prev = pl.BlockSpec((None, nc, None, sb, d),
                        lambda i, r, n: (i, 0, r, jnp.maximum(n * (rows // sb) - 1, 0), 0))
    o, lse = pl.pallas_call(
        _attn_prompt_kernel, grid=(b, dil, length // rows),
        in_specs=[cur, cur, prev, cur, prev],
        out_specs=[cur, pl.BlockSpec((None, nc, None, rows, LANES), lambda i, r, n: (i, 0, r, n, 0))],
        out_shape=[jax.ShapeDtypeStruct((b, nc, dil, length, d), BF16),
                   jax.ShapeDtypeStruct((b, nc, dil, length, LANES), F32)],
        scratch_shapes=[pltpu.VMEM((nc, rows + sb, d), BF16), pltpu.VMEM((nc, rows + sb, d), BF16)],
        compiler_params=_params(3), name=f"attn_prompt_d{dil}")(view(q), view(k), view(k), view(v), view(v))
    return o.reshape(b, r_all, length, d), lse.reshape(b, r_all, length, LANES)


def _ffn_kernel(x_ref, g_ref, win_ref, wout_ref, o_ref, h_ref, xn_ref):
    _normalise_into(xn_ref, x_ref, g_ref)
    _swiglu_half(x_ref, xn_ref, win_ref, wout_ref, h_ref, _emit_to(o_ref))


def _ffn_final_kernel(x_ref, g_ref, win_ref, wout_ref, gf_ref, o_ref, h_ref, xn_ref):
    _ffn_kernel(x_ref, g_ref, win_ref, wout_ref, o_ref, h_ref, xn_ref)
    o_ref[...] = _rms(o_ref[...], gf_ref[...])


def _ffn_sample(x, ffn, final_g=None):
    n, d = x.shape
    g, w_in, w_out = ffn
    in_specs = [_full_spec((n, d)), _full_spec((1, d)), _full_spec(w_in.shape), _full_spec(w_out.shape)]
    args = [x, g, w_in, w_out]
    body = _ffn_kernel
    if final_g is not None:
        in_specs.append(_full_spec((1, d)))
        args.append(final_g)
        body = _ffn_final_kernel
    return pl.pallas_call(
        body, grid=(1,), in_specs=in_specs, out_specs=_full_spec((n, d)),
        out_shape=jax.ShapeDtypeStruct((n, d), F32),
        scratch_shapes=[pltpu.VMEM((n, w_out.shape[0]), BF16), pltpu.VMEM((n, d), BF16)],
        compiler_params=_params(1), name="ffn_sample")(*args)


def _pool_sample_kernel(x_ref, st_ref, g_ref, wg_ref, scale_ref, wo_ref, o_ref, new_ref, z_ref):
    n, d = x_ref.shape
    gd = d // len(POOL_WINDOWS)
    x = x_ref[...]
    h = _rms(x, g_ref[...])
    for gi, w in enumerate(POOL_WINDOWS):
        c0, c1 = gi * gd, (gi + 1) * gd
        cur = h[:, c0:c1]
        tot = cur
        for s in range(1, w):
            tot = tot + st_ref[POOL_BUF - s, :, c0:c1]
        p = (tot * (1.0 / w) - cur).astype(BF16)
        z_ref[:, c0:c1] = (jnp.dot(p, wg_ref[gi], preferred_element_type=F32) * scale_ref[:, c0:c1]).astype(BF16)
    o_ref[...] = x + jnp.dot(z_ref[...], wo_ref[...], preferred_element_type=F32)
    for r in range(POOL_BUF - 1):
        new_ref[r] = st_ref[r + 1]
    new_ref[POOL_BUF - 1] = h


def _pool_sample(x, st, g, w_group, scale, w_o):
    n, d = x.shape
    return pl.pallas_call(
        _pool_sample_kernel, grid=(1,),
        in_specs=[_full_spec((n, d)), _full_spec(st.shape), _full_spec((1, d)), _full_spec(w_group.shape),
                  _full_spec((1, d)), _full_spec(w_o.shape)],
        out_specs=[_full_spec((n, d)), _full_spec(st.shape)],
        out_shape=[jax.ShapeDtypeStruct((n, d), F32), jax.ShapeDtypeStruct(st.shape, F32)],
        scratch_shapes=[pltpu.VMEM((n, d), BF16)],
        compiler_params=_params(1), name="pool_sample")(x, st, g, w_group, scale, w_o)


def _proj_sample_kernel(x_ref, g_ref, w_ref, c_ref, a_ref, b_ref, y_ref, xn_ref, *, n_rope_heads):
    xn_ref[...] = _rms(x_ref[...], g_ref[...]).astype(BF16)
    n_slots = y_ref.shape[0]
    stored_in_place = lambda hh: None
    jobs = _projection_jobs(xn_ref, w_ref, 0, (c_ref, a_ref, b_ref), y_ref.at[0:n_rope_heads], stored_in_place)
    if n_rope_heads < n_slots:
        jobs += _projection_jobs(xn_ref, w_ref, n_rope_heads * HEAD_DIM, None, y_ref.at[n_rope_heads:n_slots],
                                 stored_in_place)
    for job in jobs:
        job()


def _proj_sample(x, g, w, tables, n_rope_heads):
    n, d = x.shape
    n_slots = w.shape[1] // HEAD_DIM
    return pl.pallas_call(
        functools.partial(_proj_sample_kernel, n_rope_heads=n_rope_heads), grid=(1,),
        in_specs=[_full_spec((n, d)), _full_spec((1, d)), _full_spec(w.shape)]
        + [_full_spec((n, HEAD_DIM))] * 3,
        out_specs=_full_spec((n_slots, n, HEAD_DIM)),
        out_shape=jax.ShapeDtypeStruct((n_slots, n, HEAD_DIM), F32),
        scratch_shapes=[pltpu.VMEM((n, d), BF16)],
        compiler_params=_params(1), name="proj_sample")(x, g, w, *tables)


def _attn_sample_kernel(*refs):
    q_ref, kn_ref, vn_ref = refs[:3]
    kc_refs = refs[3:3 + N_DIL]
    vc_refs = refs[3 + N_DIL:3 + 2 * N_DIL]
    att_ref = refs[3 + 2 * N_DIL]
    scale = HEAD_DIM ** -0.5
    kn, vn = kn_ref[...], vn_ref[...]
    outs, lses = [], []
    for g in range(N_DIL):
        qg = q_ref[g]
        s_past = jnp.sum(kc_refs[g][...] * qg[None], axis=2, keepdims=True) * scale
        s_new = jnp.sum(kn * qg, axis=1, keepdims=True) * scale
        m = jnp.maximum(jnp.max(s_past, axis=0), s_new)
        p_past = jnp.exp(s_past - m[None])
        p_new = jnp.exp(s_new - m)
        den = jnp.sum(p_past, axis=0) + p_new
        o = jnp.sum(p_past * vc_refs[g][...], axis=0) + p_new * vn
        outs.append(o / den)
        lses.append(m + jnp.log(den))
    top = functools.reduce(jnp.maximum, lses)
    es = [jnp.exp(l - top) for l in lses]
    inv = 1.0 / functools.reduce(jnp.add, es)
    att_ref[...] = functools.reduce(jnp.add, [e * inv * o for e, o in zip(es, outs)])


def _attn_sample(q, k_new, v_new, cache_k, cache_v):
    b, w, h, e = cache_k.shape
    cache_args, cache_specs = [], []
    for cache in (cache_k, cache_v):
        for win, dil in DILATED_GROUPS:
            n_keys = win // dil
            last = w // dil // n_keys - 1
            cache_args.append(cache.reshape(b, w // dil, dil, h, e))
            cache_specs.append(pl.BlockSpec((None, n_keys, None, h, e),
                                            lambda i, last=last: (i, last, 0, 0, 0)))
    head = pl.BlockSpec((None, h, e), lambda i: (i, 0, 0))
    return pl.pallas_call(
        _attn_sample_kernel, grid=(b,),
        in_specs=[pl.BlockSpec((None, N_DIL, h, e), lambda i: (i, 0, 0, 0)), head, head] + cache_specs,
        out_specs=head, out_shape=jax.ShapeDtypeStruct((b, h, e), F32),
        compiler_params=_params(1), name="attn_sample")(q, k_new, v_new, *cache_args)


def _dense_residual_kernel(a_ref, x_ref, w_ref, o_ref):
    o_ref[...] = x_ref[...] + jnp.dot(a_ref[...].astype(BF16), w_ref[...], preferred_element_type=F32)


def _dense_residual(a, x, w):
    n, d = x.shape
    return pl.pallas_call(
        _dense_residual_kernel, grid=(1,),
        in_specs=[_full_spec(a.shape), _full_spec((n, d)), _full_spec(w.shape)], out_specs=_full_spec((n, d)),
        out_shape=jax.ShapeDtypeStruct((n, d), F32),
        compiler_params=_params(1), name="dense_residual")(a, x, w)


def kernel(x_prompt, x_sample, state_pool, cache_k, cache_v, ffn1_norm, ffn1_w_in, ffn1_w_out, mix_norm,
           ffn2_norm, ffn2_w_in, ffn2_w_out, pool_w_group, pool_scale, pool_w_o, kv_norm, w_kv, attn_w_q,
           attn_w_o, final_norm):
    b, s, d = x_prompt.shape
    bs = x_sample.shape[0]
    assert x_sample.shape[1] == 1
    n_heads = d // HEAD_DIM
    w_cache = cache_k.shape[1]
    assert w_cache == min(WINDOW_MAX, PAST_LEN)
    keep = min(WINDOW_MAX, s)

    bf = lambda a: a.astype(BF16)
    vec = lambda a: a.reshape(1, d)
    ffn1 = [(vec(ffn1_norm[l]), bf(ffn1_w_in[l]), bf(ffn1_w_out[l])) for l in range(2)]
    ffn2 = [(vec(ffn2_norm[l]), bf(ffn2_w_in[l]), bf(ffn2_w_out[l])) for l in range(2)]
    wg, wpo, wkv, wq, wo = bf(pool_w_group[0]), bf(pool_w_o[0]), bf(w_kv), bf(attn_w_q[0]), bf(attn_w_o[0])

    tab_p = _rope_tables(jnp.arange(s, dtype=jnp.int32))
    tab_s = _rope_tables(jnp.full((bs,), PAST_LEN, dtype=jnp.int32))

    xs = _ffn_sample(x_sample.reshape(bs, d), ffn1[0])
    hist = jnp.swapaxes(state_pool[0], 0, 1)
    xs, hist_new = _pool_sample(xs, hist, vec(mix_norm[0]), wg, vec(pool_scale[0]), wpo)
    pool_s = jnp.swapaxes(hist_new, 0, 1)[None]
    xs = _ffn_sample(xs, ffn2[0])
    kv_new = jnp.swapaxes(_proj_sample(xs, vec(kv_norm), wkv, tab_s, n_heads), 0, 1)
    k_new, v_new = kv_new[:, :n_heads], kv_new[:, n_heads:]
    xs = _ffn_sample(xs, ffn1[1])
    q_new = jnp.swapaxes(_proj_sample(xs, vec(mix_norm[1]), wq, tab_s, N_DIL * n_heads), 0, 1)
    att = _attn_sample(q_new.reshape(bs, N_DIL, n_heads, HEAD_DIM), k_new, v_new, cache_k, cache_v)
    xs = _dense_residual(att.reshape(bs, d), xs, wo)
    y_sample = _ffn_sample(xs, ffn2[1], vec(final_norm)).reshape(bs, 1, d)

    x = x_prompt.reshape(b * s, d)
    x, xn, pool_hist, k_s = _layer0_head(x, s, ffn1[0], vec(mix_norm[0]), wg, vec(pool_scale[0]), wpo,
                                         ffn2[0][0], cache_k, k_new[:, None])
    pool_p = pool_hist[:, HALO - POOL_BUF:][None]
    x, xn, k_nat, v_nat, k_str, v_str, k_keep, v_keep, v_s = _layer0_tail(
        x, xn, s, keep, ffn2[0], vec(kv_norm), wkv, ffn1[1][0], tab_p, cache_v, v_new[:, None])
    x, *qs_by_group = _layer1_head(x, xn, s, ffn1[1], vec(mix_norm[1]), wq, tab_p)
    outs, lses = [], []
    for (win, dil), q in zip(DILATED_GROUPS, qs_by_group):
        assert win // dil == ATTN_BLOCK
        if dil == 1:
            nat = lambda a: a.reshape(b, 1, s, d)
            o, lse = _attn_prompt(nat(q), nat(k_nat), nat(v_nat), 1)
            outs.append(o.reshape(b * s, d))
            lses.append(lse.reshape(b * s, LANES))
        else:
            o, lse = _attn_prompt(q, k_str, v_str, dil)
            outs.append(o)
            lses.append(lse)
    y_prompt = _layer1_tail(outs, lses, x, s, wo, ffn2[1], vec(final_norm)).reshape(b, s, d)
    k_p = k_keep.reshape(b, keep, n_heads, HEAD_DIM)
    v_p = v_keep.reshape(b, keep, n_heads, HEAD_DIM)

    return (y_prompt, y_sample, pool_p, pool_s, k_p, v_p, k_s, v_s)
```

```python
import functools

import jax
import jax.numpy as jnp
from jax import lax
from jax.experimental import pallas as pl
from jax.experimental.pallas import tpu as pltpu

F32 = jnp.float32
BF16 = jnp.bfloat16

POOL_WINDOWS = (2, 4, 8, 16)
POOL_BUF = max(POOL_WINDOWS) - 1
HEAD_DIM = 128
DILATED_GROUPS = ((128, 1), (512, 4), (2048, 16))
N_DIL = len(DILATED_GROUPS)
WINDOW_MAX = max(w for w, _ in DILATED_GROUPS)
STRIDE = max(d for _, d in DILATED_GROUPS)
ATTN_BLOCK = 128
ROT_DIM = HEAD_DIM // 4
ROPE_THETA = 500000.0
RMS_EPS = 1e-6
PAST_LEN = 16384
LOG2_E = 1.4426950408889634

V7X_VMEM_BYTES = 64 * 1024 * 1024
VMEM_LIMIT = (V7X_VMEM_BYTES * 29) // 32
LANES = 128
MXU_WIDTH = 256
HALO = 16

ROW_TILE = 256
BIG_TILE = 512
ATTN_QUERIES = 1024
SOFTMAX_ROWS = 128


def _const_spec(shape):
    n = len(shape)
    return pl.BlockSpec(shape, lambda *_: (0,) * n, pipeline_mode=pl.Buffered(1))


def _full_spec(shape):
    return pl.BlockSpec(shape, lambda i: (0,) * len(shape))


def _params(n_axes):
    return pltpu.CompilerParams(dimension_semantics=("arbitrary",) * n_axes,
                                vmem_limit_bytes=VMEM_LIMIT)


def _rms(x, g):
    return x * lax.rsqrt(jnp.mean(x * x, axis=-1, keepdims=True) + RMS_EPS) * g


def _col_chunks(width):
    return [(c, c + MXU_WIDTH) for c in range(0, width, MXU_WIDTH)]


def _normalise_into(dst_ref, src_ref, g_ref):
    dst_ref[...] = _rms(src_ref[...], g_ref[...]).astype(dst_ref.dtype)


def _swiglu_half(src_ref, xn_ref, win_ref, wout_ref, h_ref, emit, side_jobs=()):
    d_ff, d = wout_ref.shape
    ff_chunks, out_chunks = _col_chunks(d_ff), _col_chunks(d)
    n_seg = len(ff_chunks) + len(out_chunks)
    jobs = list(side_jobs)
    done = [0, 0]

    def end_segment():
        done[0] += 1
        while done[1] < len(jobs) and done[1] * n_seg < done[0] * len(jobs):
            jobs[done[1]]()
            done[1] += 1

    for c0, c1 in ff_chunks:
        gate = jnp.dot(xn_ref[...], win_ref[:, c0:c1], preferred_element_type=F32)
        up = jnp.dot(xn_ref[...], win_ref[:, d_ff + c0:d_ff + c1], preferred_element_type=F32)
        h_ref[:, c0:c1] = (gate * jax.nn.sigmoid(gate) * up).astype(BF16)
        end_segment()
    for c0, c1 in out_chunks:
        emit(c0, c1, src_ref[:, c0:c1] + 0.5 * jnp.dot(h_ref[...], wout_ref[:, c0:c1],
                                                        preferred_element_type=F32))
        end_segment()


def _pool_trailing(ext_ref, row0, rows, j, gi, w, gd):
    c0, c1 = gi * gd, (gi + 1) * gd
    cur = ext_ref[row0:row0 + rows, c0:c1]
    tot = cur
    for s in range(1, w):
        tot = tot + ext_ref[row0 - s:row0 - s + rows, c0:c1]
    if j is None:
        return tot * (1.0 / w) - cur
    pos1 = (j * rows + 1 + lax.broadcasted_iota(jnp.int32, (rows, 1), 0)).astype(F32)
    return tot * (1.0 / jnp.minimum(pos1, float(w))) - cur


def _pool_jobs(src_ref, j, g_ref, wg_ref, scale_ref, wo_ref, ext_ref, z_ref, o_ref, state_ref, g_next_ref,
               xn_next_ref):
    tm, d = src_ref.shape
    gd = d // len(POOL_WINDOWS)

    def normalise():
        h = _rms(src_ref[...], g_ref[...])
        ext_ref[0:HALO, :] = jnp.where(j == 0, 0.0, ext_ref[tm:tm + HALO, :])
        ext_ref[HALO:HALO + tm, :] = h
        state_ref[...] = h[tm - HALO:, :]

    def group(gi, w):
        p = _pool_trailing(ext_ref, HALO, tm, j, gi, w, gd).astype(BF16)
        z = jnp.dot(p, wg_ref[gi], preferred_element_type=F32) * scale_ref[:, gi * gd:(gi + 1) * gd]
        z_ref[:, gi * gd:(gi + 1) * gd] = z.astype(BF16)

    def project(c0, c1):
        o_ref[:, c0:c1] = src_ref[:, c0:c1] + jnp.dot(z_ref[...], wo_ref[:, c0:c1],
                                                      preferred_element_type=F32)

    return ([normalise] + [functools.partial(group, gi, w) for gi, w in enumerate(POOL_WINDOWS)]
            + [functools.partial(project, c0, c1) for c0, c1 in _col_chunks(d)]
            + [functools.partial(_normalise_into, xn_next_ref, o_ref, g_next_ref)])


def _rope_tables(pos):
    half = ROT_DIM // 2
    inv_freq = ROPE_THETA ** (-jnp.arange(0, ROT_DIM, 2, dtype=F32) / ROT_DIM)
    ang = pos.astype(F32)[:, None] * inv_freq[None, :]
    cos, sin = jnp.cos(ang), jnp.sin(ang)
    t = pos.shape[0]
    c = jnp.concatenate([cos, cos, jnp.ones((t, HEAD_DIM - ROT_DIM), F32)], axis=1)
    a = jnp.concatenate([-sin, jnp.zeros((t, HEAD_DIM - half), F32)], axis=1)
    b = jnp.concatenate([jnp.zeros((t, half), F32), sin, jnp.zeros((t, HEAD_DIM - ROT_DIM), F32)], axis=1)
    return c, a, b


def _projection_jobs(xn_ref, w_ref, col0, table_refs, y_ref, store_head):
    n_slots = y_ref.shape[0]
    half = ROT_DIM // 2
    per_dot = MXU_WIDTH // HEAD_DIM

    def project(h0):
        cols = slice(col0 + h0 * HEAD_DIM, col0 + (h0 + per_dot) * HEAD_DIM)
        y = jnp.dot(xn_ref[...], w_ref[:, cols], preferred_element_type=F32)
        for k in range(per_dot):
            y_ref[h0 + k] = y[:, k * HEAD_DIM:(k + 1) * HEAD_DIM]

    def finish(hh):
        if table_refs is not None:
            c, a, b = (t[...] for t in table_refs)
            yh = y_ref[hh]
            y_ref[hh] = yh * c + pltpu.roll(yh, HEAD_DIM - half, 1) * a + pltpu.roll(yh, half, 1) * b
        store_head(hh)

    jobs = []
    for h0 in range(0, n_slots, per_dot):
        jobs.append(functools.partial(project, h0))
        jobs += [functools.partial(finish, h0 + k) for k in range(per_dot)]
    return jobs


def _store_head_natural(src_ref, hh, dst_ref):
    dst_ref[:, hh * HEAD_DIM:(hh + 1) * HEAD_DIM] = src_ref[hh].astype(dst_ref.dtype)


def _store_head_strided(src_ref, hh, dst_ref):
    per = src_ref.shape[1] // STRIDE
    for r in range(STRIDE):
        dst_ref[r, :, hh * HEAD_DIM:(hh + 1) * HEAD_DIM] = (
            src_ref[hh, pl.ds(r, per, stride=STRIDE), :].astype(dst_ref.dtype))


def _roll_tile(seq_end, src_ref, nxt_ref, new_ref, dst_ref):
    t = src_ref.shape[0]
    dst_ref[0:t - 1] = src_ref[1:t]
    dst_ref[t - 1] = jnp.where(seq_end, new_ref[0], nxt_ref[0])


def _combine_jobs(o_refs, lse_refs, x_ref, wo_ref, o_nat, lse_nat, w_ref, att_ref, dst_ref, g_next_ref,
                  xn_next_ref):
    tm, d = x_ref.shape
    per = tm // STRIDE
    n_heads = d // HEAD_DIM
    heads, lses = [], []
    jobs = []
    slot = 0
    for (_, dil), o_ref, lse_ref in zip(DILATED_GROUPS, o_refs, lse_refs):
        if dil == 1:
            heads.append(functools.partial(
                lambda hh, ref: ref[:, hh * HEAD_DIM:(hh + 1) * HEAD_DIM].astype(F32), ref=o_ref))
            lses.append(functools.partial(lambda ref: ref[...], lse_ref))
            continue

        def to_natural(r, o_ref=o_ref, lse_ref=lse_ref, slot=slot):
            lse_nat[slot, pl.ds(r, per, stride=STRIDE), :] = lse_ref[r]
            for hh in range(n_heads):
                o_nat[slot * n_heads + hh, pl.ds(r, per, stride=STRIDE), :] = (
                    o_ref[r, :, hh * HEAD_DIM:(hh + 1) * HEAD_DIM].astype(F32))

        jobs += [functools.partial(to_natural, r) for r in range(STRIDE)]
        heads.append(functools.partial(lambda hh, s0: o_nat[s0 + hh], s0=slot * n_heads))
        lses.append(functools.partial(lambda s: lse_nat[s], slot))
        slot += 1

    def weights():
        ls = [l() for l in lses]
        top = functools.reduce(jnp.maximum, ls)
        es = [jnp.exp(l - top) for l in ls]
        inv = 1.0 / functools.reduce(jnp.add, es)
        for g, e in enumerate(es):
            w_ref[g] = e * inv

    def mix_head(hh):
        att = functools.reduce(jnp.add, [w_ref[g][:, hh:hh + 1] * heads[g](hh) for g in range(N_DIL)])
        att_ref[:, hh * HEAD_DIM:(hh + 1) * HEAD_DIM] = att.astype(BF16)

    def project(c0, c1):
        dst_ref[:, c0:c1] = x_ref[:, c0:c1] + jnp.dot(att_ref[...], wo_ref[:, c0:c1],
                                                      preferred_element_type=F32)

    return (jobs + [weights] + [functools.partial(mix_head, hh) for hh in range(n_heads)]
            + [functools.partial(project, c0, c1) for c0, c1 in _col_chunks(d)]
            + [functools.partial(_normalise_into, xn_next_ref, dst_ref, g_next_ref)])


class _Tiling:
    def __init__(self, n_rows, seq, tm):
        self.tm = tm
        self.n_tiles = n_rows // tm
        self.per_seq = seq // tm
        self.steps = self.n_tiles + 1

    def lead(self, i):
        return jnp.minimum(i, self.n_tiles - 1)

    def trail(self, i):
        return jnp.maximum(i - 1, 0)

    def row(self, width, tile_of):
        return pl.BlockSpec((self.tm, width), lambda i: (tile_of(i), 0))

    def strided(self, width, tile_of):
        return pl.BlockSpec((None, STRIDE, self.tm // STRIDE, width),
                            lambda i: (tile_of(i) // self.per_seq, 0, tile_of(i) % self.per_seq, 0))

    def table(self, tile_of):
        return pl.BlockSpec((self.tm, HEAD_DIM), lambda i: (tile_of(i) % self.per_seq, 0))


def _roll_operands(cache, new, tl):
    b, w, h, e = cache.shape
    per = tl.n_tiles // b
    t = w // per
    blk = pl.BlockSpec((None, t, h, e), lambda i: (tl.lead(i) // per, tl.lead(i) % per, 0, 0))
    nxt = pl.BlockSpec((None, 1, h, e),
                       lambda i: (tl.lead(i) // per, jnp.minimum((tl.lead(i) % per + 1) * t, w - 1), 0, 0))
    cur_new = pl.BlockSpec((None, 1, h, e), lambda i: (tl.lead(i) // per, 0, 0, 0))
    return per, [blk, nxt, cur_new], [cache, cache, new], blk


def _zero_on_first_step(i, *refs):
    @pl.when(i == 0)
    def _():
        for ref in refs:
            ref[...] = jnp.zeros(ref.shape, ref.dtype)


def _run_all(jobs):
    for job in jobs:
        job()


def _with_job_in_middle(jobs, job):
    jobs = list(jobs)
    jobs.insert(len(jobs) // 2, job)
    return jobs


def _run_stages(both_have_a_tile, both, alone):
    pl.when(both_have_a_tile)(both)
    pl.when(jnp.logical_not(both_have_a_tile))(alone)


def _emit_to(*refs):
    def emit(c0, c1, value):
        for ref in refs:
            ref[:, c0:c1] = value
    return emit


def _layer0_head_kernel(x_ref, g1_ref, win_ref, wout_ref, gm_ref, wg_ref, scale_ref, wo_ref, gn_ref,
                        src_ref, nxt_ref, new_ref, o_ref, on_ref, state_ref, dst_ref,
                        h_ref, xn_ref, xs_ref, ext_ref, z_ref, *, tl, roll_per_seq):
    i = pl.program_id(0)
    _zero_on_first_step(i, xs_ref, ext_ref)
    trailing = _pool_jobs(xs_ref.at[(i + 1) % 2], tl.trail(i) % tl.per_seq, gm_ref, wg_ref, scale_ref, wo_ref,
                          ext_ref, z_ref, o_ref, state_ref, gn_ref, on_ref)
    roll = functools.partial(
        _roll_tile, tl.lead(i) % roll_per_seq == roll_per_seq - 1, src_ref, nxt_ref, new_ref, dst_ref)

    def both():
        _normalise_into(xn_ref, x_ref, g1_ref)
        _swiglu_half(x_ref, xn_ref, win_ref, wout_ref, h_ref, _emit_to(xs_ref.at[i % 2]),
                     _with_job_in_middle(trailing, roll))

    _run_stages(i < tl.n_tiles, both, functools.partial(_run_all, trailing))


def _layer0_head(x, seq, ffn, g_mix, w_group, scale, w_o, g_next, cache, new):
    n, d = x.shape
    tl = _Tiling(n, seq, BIG_TILE)
    g1, w_in, w_out = ffn
    roll_per_seq, roll_specs, roll_args, roll_out = _roll_operands(cache, new, tl)
    return pl.pallas_call(
        functools.partial(_layer0_head_kernel, tl=tl, roll_per_seq=roll_per_seq), grid=(tl.steps,),
        in_specs=[tl.row(d, tl.lead), _const_spec((1, d)), _const_spec(w_in.shape), _const_spec(w_out.shape),
                  _const_spec((1, d)), _const_spec(w_group.shape), _const_spec((1, d)), _const_spec(w_o.shape),
                  _const_spec((1, d))] + roll_specs,
        out_specs=[tl.row(d, tl.trail), tl.row(d, tl.trail),
                   pl.BlockSpec((None, HALO, d), lambda i: (tl.trail(i) // tl.per_seq, 0, 0)), roll_out],
        out_shape=[jax.ShapeDtypeStruct((n, d), F32), jax.ShapeDtypeStruct((n, d), BF16),
                   jax.ShapeDtypeStruct((n // seq, HALO, d), F32),
                   jax.ShapeDtypeStruct(cache.shape, cache.dtype)],
        scratch_shapes=[pltpu.VMEM((tl.tm, w_out.shape[0]), BF16), pltpu.VMEM((tl.tm, d), BF16),
                        pltpu.VMEM((2, tl.tm, d), F32), pltpu.VMEM((tl.tm + HALO, d), F32),
                        pltpu.VMEM((tl.tm, d), BF16)],
        compiler_params=_params(1), name="layer0_head")(
            x, g1, w_in, w_out, g_mix, w_group, scale, w_o, g_next, *roll_args)


def _layer0_tail_kernel(x_ref, xn_ref, win_ref, wout_ref, gkv_ref, wkv_ref, gn_ref, c_ref, a_ref, b_ref,
                        src_ref, nxt_ref, new_ref,
                        o_ref, on_ref, k_ref, v_ref, ks_ref, vs_ref, kf_ref, vf_ref, dst_ref,
                        h_ref, xs_ref, xkv_ref, yk_ref, yv_ref, *, tl, roll_per_seq, kept_tiles):
    i = pl.program_id(0)
    _zero_on_first_step(i, xs_ref)
    d = x_ref.shape[1]

    def normalise():
        x = xs_ref[(i + 1) % 2]
        y = x * lax.rsqrt(jnp.mean(x * x, axis=-1, keepdims=True) + RMS_EPS)
        xkv_ref[...] = (y * gkv_ref[...]).astype(BF16)
        on_ref[...] = (y * gn_ref[...]).astype(BF16)

    def store_both(y_ref, nat_ref, str_ref, hh):
        _store_head_natural(y_ref, hh, nat_ref)
        _store_head_strided(y_ref, hh, str_ref)

    trailing = [normalise]
    trailing += _projection_jobs(xkv_ref, wkv_ref, 0, (c_ref, a_ref, b_ref), yk_ref,
                                 functools.partial(store_both, yk_ref, k_ref, ks_ref))
    trailing += _projection_jobs(xkv_ref, wkv_ref, d, None, yv_ref,
                                 functools.partial(store_both, yv_ref, v_ref, vs_ref))
    roll = functools.partial(
        _roll_tile, tl.lead(i) % roll_per_seq == roll_per_seq - 1, src_ref, nxt_ref, new_ref, dst_ref)

    def both():
        _swiglu_half(x_ref, xn_ref, win_ref, wout_ref, h_ref, _emit_to(o_ref, xs_ref.at[i % 2]),
                     _with_job_in_middle(trailing, roll))

    _run_stages(i < tl.n_tiles, both, functools.partial(_run_all, trailing))

    @pl.when(tl.trail(i) % tl.per_seq >= tl.per_seq - kept_tiles)
    def _():
        for hh in range(yk_ref.shape[0]):
            kf_ref[:, hh, :] = yk_ref[hh]
            vf_ref[:, hh, :] = yv_ref[hh]


def _layer0_tail(x, xn, seq, keep, ffn, g_kv, w_kv, g_next, tables, cache, new):
    n, d = x.shape
    b = n // seq
    n_heads = d // HEAD_DIM
    tl = _Tiling(n, seq, ROW_TILE)
    _, w_in, w_out = ffn
    kept = keep // tl.tm
    roll_per_seq, roll_specs, roll_args, roll_out = _roll_operands(cache, new, tl)

    def kept_block(i):
        t = tl.trail(i)
        return ((t // tl.per_seq) * kept + jnp.maximum(t % tl.per_seq - (tl.per_seq - kept), 0), 0, 0)

    kept_spec = pl.BlockSpec((tl.tm, n_heads, HEAD_DIM), kept_block)
    nat_shape = jax.ShapeDtypeStruct((n, d), BF16)
    str_shape = jax.ShapeDtypeStruct((b, STRIDE, seq // STRIDE, d), BF16)
    kept_shape = jax.ShapeDtypeStruct((b * keep, n_heads, HEAD_DIM), F32)
    head_scratch = pltpu.VMEM((n_heads, tl.tm, HEAD_DIM), F32)
    return pl.pallas_call(
        functools.partial(_layer0_tail_kernel, tl=tl, roll_per_seq=roll_per_seq, kept_tiles=kept),
        grid=(tl.steps,),
        in_specs=[tl.row(d, tl.lead), tl.row(d, tl.lead), _const_spec(w_in.shape), _const_spec(w_out.shape),
                  _const_spec((1, d)), _const_spec(w_kv.shape), _const_spec((1, d))]
        + [tl.table(tl.trail)] * 3 + roll_specs,
        out_specs=[tl.row(d, tl.lead), tl.row(d, tl.trail), tl.row(d, tl.trail), tl.row(d, tl.trail),
                   tl.strided(d, tl.trail), tl.strided(d, tl.trail), kept_spec, kept_spec, roll_out],
        out_shape=[jax.ShapeDtypeStruct((n, d), F32), nat_shape, nat_shape, nat_shape, str_shape, str_shape,
                   kept_shape, kept_shape, jax.ShapeDtypeStruct(cache.shape, cache.dtype)],
        scratch_shapes=[pltpu.VMEM((tl.tm, w_out.shape[0]), BF16), pltpu.VMEM((2, tl.tm, d), F32),
                        pltpu.VMEM((tl.tm, d), BF16), head_scratch, head_scratch],
        compiler_params=_params(1), name="layer0_tail")(
            x, xn, w_in, w_out, g_kv, w_kv, g_next, *tables, *roll_args)


def _layer1_head_kernel(x_ref, xn_ref, win_ref, wout_ref, gm_ref, wq_ref, c_ref, a_ref, b_ref, o_ref, *refs, tl):
    q_refs, (h_ref, xs_ref, xq_ref, y_ref) = refs[:N_DIL], refs[N_DIL:]
    i = pl.program_id(0)
    _zero_on_first_step(i, xs_ref)
    d = x_ref.shape[1]
    n_heads = d // HEAD_DIM

    def normalise():
        xq_ref[...] = _rms(xs_ref[(i + 1) % 2], gm_ref[...]).astype(BF16)

    trailing = [normalise]
    for g, ((_, dil), q_ref) in enumerate(zip(DILATED_GROUPS, q_refs)):
        yg_ref = y_ref.at[g * n_heads:(g + 1) * n_heads]
        store = _store_head_natural if dil == 1 else _store_head_strided
        trailing += _projection_jobs(xq_ref, wq_ref, g * d, (c_ref, a_ref, b_ref), yg_ref,
                                     functools.partial(lambda hh, st, src, dst: st(src, hh, dst),
                                                       st=store, src=yg_ref, dst=q_ref))

    def both():
        _swiglu_half(x_ref, xn_ref, win_ref, wout_ref, h_ref, _emit_to(o_ref, xs_ref.at[i % 2]), trailing)

    _run_stages(i < tl.n_tiles, both, functools.partial(_run_all, trailing))


def _layer1_head(x, xn, seq, ffn, g_mix, w_q, tables):
    n, d = x.shape
    b = n // seq
    tl = _Tiling(n, seq, BIG_TILE)
    _, w_in, w_out = ffn
    q_specs, q_shapes = [], []
    for _, dil in DILATED_GROUPS:
        q_specs.append(tl.row(d, tl.trail) if dil == 1 else tl.strided(d, tl.trail))
        q_shapes.append(jax.ShapeDtypeStruct((n, d) if dil == 1 else (b, STRIDE, seq // STRIDE, d), BF16))
    return pl.pallas_call(
        functools.partial(_layer1_head_kernel, tl=tl), grid=(tl.steps,),
        in_specs=[tl.row(d, tl.lead), tl.row(d, tl.lead), _const_spec(w_in.shape), _const_spec(w_out.shape),
                  _const_spec((1, d)), _const_spec(w_q.shape)] + [tl.table(tl.trail)] * 3,
        out_specs=[tl.row(d, tl.lead)] + q_specs,
        out_shape=[jax.ShapeDtypeStruct((n, d), F32)] + q_shapes,
        scratch_shapes=[pltpu.VMEM((tl.tm, w_out.shape[0]), BF16), pltpu.VMEM((2, tl.tm, d), F32),
                        pltpu.VMEM((tl.tm, d), BF16),
                        pltpu.VMEM((w_q.shape[1] // HEAD_DIM, tl.tm, HEAD_DIM), F32)],
        compiler_params=_params(1), name="layer1_head")(x, xn, w_in, w_out, g_mix, w_q, *tables)


def _layer1_tail_kernel(*refs):
    o_refs, lse_refs = refs[:N_DIL], refs[N_DIL:2 * N_DIL]
    (x_ref, wo_ref, g2_ref, win_ref, wout_ref, gf_ref, out_ref,
     h_ref, xs_ref, xns_ref, o_nat, lse_nat, w_ref, att_ref) = refs[2 * N_DIL:]
    i = pl.program_id(0)
    leading = _combine_jobs(o_refs, lse_refs, x_ref, wo_ref, o_nat, lse_nat, w_ref, att_ref,
                            xs_ref.at[i % 2], g2_ref, xns_ref.at[i % 2])

    def both():
        _swiglu_half(xs_ref.at[(i + 1) % 2], xns_ref.at[(i + 1) % 2], win_ref, wout_ref, h_ref,
                     _emit_to(out_ref), leading)
        out_ref[...] = _rms(out_ref[...], gf_ref[...])

    _run_stages(i > 0, both, functools.partial(_run_all, leading))


def _layer1_tail(outs, lses, x, seq, w_o, ffn, g_final):
    n, d = x.shape
    tl = _Tiling(n, seq, BIG_TILE)
    g2, w_in, w_out = ffn

    def group_specs(width):
        return [tl.row(width, tl.lead) if dil == 1 else tl.strided(width, tl.lead) for _, dil in DILATED_GROUPS]

    n_strided = sum(dil > 1 for _, dil in DILATED_GROUPS)
    return pl.pallas_call(
        _layer1_tail_kernel, grid=(tl.steps,),
        in_specs=group_specs(d) + group_specs(LANES)
        + [tl.row(d, tl.lead), _const_spec(w_o.shape), _const_spec((1, d)), _const_spec(w_in.shape),
           _const_spec(w_out.shape), _const_spec((1, d))],
        out_specs=tl.row(d, tl.trail), out_shape=jax.ShapeDtypeStruct((n, d), F32),
        scratch_shapes=[pltpu.VMEM((tl.tm, w_out.shape[0]), BF16), pltpu.VMEM((2, tl.tm, d), F32),
                        pltpu.VMEM((2, tl.tm, d), BF16),
                        pltpu.VMEM((n_strided * (d // HEAD_DIM), tl.tm, HEAD_DIM), F32),
                        pltpu.VMEM((n_strided, tl.tm, LANES), F32), pltpu.VMEM((N_DIL, tl.tm, LANES), F32),
                        pltpu.VMEM((tl.tm, d), BF16)],
        compiler_params=_params(1), name="layer1_tail")(*outs, *lses, x, w_o, g2, w_in, w_out, g_final)


def _attn_prompt_kernel(q_ref, kc_ref, kp_ref, vc_ref, vp_ref, o_ref, lse_ref, kk_ref, vv_ref):
    n = pl.program_id(2)
    nc, rows, d = q_ref.shape
    blk = ATTN_BLOCK
    sb = blk // nc
    n_heads = d // HEAD_DIM
    for c in range(nc):
        kk_ref[c, 0:sb, :] = kp_ref[c]
        kk_ref[c, sb:, :] = kc_ref[c]
        vv_ref[c, 0:sb, :] = vp_ref[c]
        vv_ref[c, sb:, :] = vc_ref[c]
    lane = lax.broadcasted_iota(jnp.int32, (SOFTMAX_ROWS, LANES), 1)
    scale = HEAD_DIM ** -0.5

    def keys(ref, u, sl):
        parts = [ref[c, u * sb:(u + 2) * sb, sl] for c in range(nc)]
        return parts[0] if nc == 1 else jnp.concatenate(parts, axis=0)

    def slab_ranges(r0, r1):
        spans = [(c, max(c * sb, r0), min((c + 1) * sb, r1)) for c in range(nc)]
        return [(c, lo - c * sb, lo, hi) for c, lo, hi in spans if lo < hi]

    for r0 in range(0, blk, SOFTMAX_ROWS):
        r1 = r0 + SOFTMAX_ROWS
        qrow = r0 + lax.broadcasted_iota(jnp.int32, (SOFTMAX_ROWS, 2 * blk), 0)
        kcol = lax.broadcasted_iota(jnp.int32, (SOFTMAX_ROWS, 2 * blk), 1)
        dist = nc * (qrow % sb - kcol % (2 * sb) + sb) + (qrow // sb - kcol // (2 * sb))
        band = (dist >= 0) & (dist <= blk)
        for u in range(rows // sb):
            valid = band
            if u == 0:
                valid = band & ((kcol % (2 * sb) >= sb) | (n > 0))
            lse_tile = jnp.zeros((SOFTMAX_ROWS, LANES), F32)
            for hh in range(n_heads):
                sl = slice(hh * HEAD_DIM, (hh + 1) * HEAD_DIM)
                q_parts = [q_ref[c, u * sb + off:u * sb + off + hi - lo, sl]
                           for c, off, lo, hi in slab_ranges(r0, r1)]
                qh = q_parts[0] if len(q_parts) == 1 else jnp.concatenate(q_parts, axis=0)
                s = lax.dot_general(qh, keys(kk_ref, u, sl), (((1,), (1,)), ((), ())),
                                    preferred_element_type=F32)
                s = jnp.where(valid, s, -jnp.inf)
                m = jnp.max(s, axis=1, keepdims=True)
                p = jnp.exp2((s - m) * (scale * LOG2_E))
                den = jnp.sum(p, axis=1, keepdims=True)
                o = jnp.dot(p.astype(BF16), keys(vv_ref, u, sl), preferred_element_type=F32) / den
                for c, off, lo, hi in slab_ranges(r0, r1):
                    o_ref[c, u * sb + off:u * sb + off + hi - lo, sl] = o[lo - r0:hi - r0].astype(o_ref.dtype)
                lse_tile = jnp.where(lane == hh, m * scale + jnp.log(den), lse_tile)
            for c, off, lo, hi in slab_ranges(r0, r1):
                lse_ref[c, u * sb + off:u * sb + off + hi - lo, :] = lse_tile[lo - r0:hi - r0]


def _attn_prompt(q, k, v, dil):
    b, r_all, length, d = q.shape
    nc = r_all // dil
    sb = ATTN_BLOCK // nc
    rows = min(ATTN_QUERIES // nc, length)
    view = lambda a: a.reshape(b, nc, dil, length, a.shape[-1])
    cur = pl.BlockSpec((None, nc, None, rows, d), lambda i, r, n: (i, 0, r, n, 0))
    prev = pl.BlockSpec((None, nc, None, sb, d),
                        lambda i, r, n: (i, 0, r, jnp.maximum(n * (rows // sb) - 1, 0), 0))
    o, lse = pl.pallas_call(
        _attn_prompt_kernel, grid=(b, dil, length // rows),
        in_specs=[cur, cur, prev, cur, prev],
        out_specs=[cur, pl.BlockSpec((None, nc, None, rows, LANES), lambda i, r, n: (i, 0, r, n, 0))],
        out_shape=[jax.ShapeDtypeStruct((b, nc, dil, length, d), BF16),
                   jax.ShapeDtypeStruct((b, nc, dil, length, LANES), F32)],
        scratch_shapes=[pltpu.VMEM((nc, rows + sb, d), BF16), pltpu.VMEM((nc, rows + sb, d), BF16)],
        compiler_params=_params(3), name=f"attn_prompt_d{dil}")(view(q), view(k), view(k), view(v), view(v))
    return o.reshape(b, r_all, length, d), lse.reshape(b, r_all, length, LANES)


def _ffn_kernel(x_ref, g_ref, win_ref, wout_ref, o_ref, h_ref, xn_ref):
    _normalise_into(xn_ref, x_ref, g_ref)
    _swiglu_half(x_ref, xn_ref, win_ref, wout_ref, h_ref, _emit_to(o_ref))


def _ffn_final_kernel(x_ref, g_ref, win_ref, wout_ref, gf_ref, o_ref, h_ref, xn_ref):
    _ffn_kernel(x_ref, g_ref, win_ref, wout_ref, o_ref, h_ref, xn_ref)
    o_ref[...] = _rms(o_ref[...], gf_ref[...])


def _ffn_sample(x, ffn, final_g=None):
    n, d = x.shape
    g, w_in, w_out = ffn
    in_specs = [_full_spec((n, d)), _full_spec((1, d)), _full_spec(w_in.shape), _full_spec(w_out.shape)]
    args = [x, g, w_in, w_out]
    body = _ffn_kernel
    if final_g is not None:
        in_specs.append(_full_spec((1, d)))
        args.append(final_g)
        body = _ffn_final_kernel
    return pl.pallas_call(
        body, grid=(1,), in_specs=in_specs, out_specs=_full_spec((n, d)),
        out_shape=jax.ShapeDtypeStruct((n, d), F32),
        scratch_shapes=[pltpu.VMEM((n, w_out.shape[0]), BF16), pltpu.VMEM((n, d), BF16)],
        compiler_params=_params(1), name="ffn_sample")(*args)


def _pool_sample_kernel(x_ref, st_ref, g_ref, wg_ref, scale_ref, wo_ref, o_ref, new_ref, z_ref):
    n, d = x_ref.shape
    gd = d // len(POOL_WINDOWS)
    x = x_ref[...]
    h = _rms(x, g_ref[...])
    for gi, w in enumerate(POOL_WINDOWS):
        c0, c1 = gi * gd, (gi + 1) * gd
        cur = h[:, c0:c1]
        tot = cur
        for s in range(1, w):
            tot = tot + st_ref[POOL_BUF - s, :, c0:c1]
        p = (tot * (1.0 / w) - cur).astype(BF16)
        z_ref[:, c0:c1] = (jnp.dot(p, wg_ref[gi], preferred_element_type=F32) * scale_ref[:, c0:c1]).astype(BF16)
    o_ref[...] = x + jnp.dot(z_ref[...], wo_ref[...], preferred_element_type=F32)
    for r in range(POOL_BUF - 1):
        new_ref[r] = st_ref[r + 1]
    new_ref[POOL_BUF - 1] = h


def _pool_sample(x, st, g, w_group, scale, w_o):
    n, d = x.shape
    return pl.pallas_call(
        _pool_sample_kernel, grid=(1,),
        in_specs=[_full_spec((n, d)), _full_spec(st.shape), _full_spec((1, d)), _full_spec(w_group.shape),
                  _full_spec((1, d)), _full_spec(w_o.shape)],
        out_specs=[_full_spec((n, d)), _full_spec(st.shape)],
        out_shape=[jax.ShapeDtypeStruct((n, d), F32), jax.ShapeDtypeStruct(st.shape, F32)],
        scratch_shapes=[pltpu.VMEM((n, d), BF16)],
        compiler_params=_params(1), name="pool_sample")(x, st, g, w_group, scale, w_o)


def _proj_sample_kernel(x_ref, g_ref, w_ref, c_ref, a_ref, b_ref, y_ref, xn_ref, *, n_rope_heads):
    xn_ref[...] = _rms(x_ref[...], g_ref[...]).astype(BF16)
    n_slots = y_ref.shape[0]
    stored_in_place = lambda hh: None
    jobs = _projection_jobs(xn_ref, w_ref, 0, (c_ref, a_ref, b_ref), y_ref.at[0:n_rope_heads], stored_in_place)
    if n_rope_heads < n_slots:
        jobs += _projection_jobs(xn_ref, w_ref, n_rope_heads * HEAD_DIM, None, y_ref.at[n_rope_heads:n_slots],
                                 stored_in_place)
    for job in jobs:
        job()


def _proj_sample(x, g, w, tables, n_rope_heads):
    n, d = x.shape
    n_slots = w.shape[1] // HEAD_DIM
    return pl.pallas_call(
        functools.partial(_proj_sample_kernel, n_rope_heads=n_rope_heads), grid=(1,),
        in_specs=[_full_spec((n, d)), _full_spec((1, d)), _full_spec(w.shape)]
        + [_full_spec((n, HEAD_DIM))] * 3,
        out_specs=_full_spec((n_slots, n, HEAD_DIM)),
        out_shape=jax.ShapeDtypeStruct((n_slots, n, HEAD_DIM), F32),
        scratch_shapes=[pltpu.VMEM((n, d), BF16)],
        compiler_params=_params(1), name="proj_sample")(x, g, w, *tables)


def _attn_sample_kernel(*refs):
    q_ref, kn_ref, vn_ref = refs[:3]
    kc_refs = refs[3:3 + N_DIL]
    vc_refs = refs[3 + N_DIL:3 + 2 * N_DIL]
    att_ref = refs[3 + 2 * N_DIL]
    scale = HEAD_DIM ** -0.5
    kn, vn = kn_ref[...], vn_ref[...]
    outs, lses = [], []
    for g in range(N_DIL):
        qg = q_ref[g]
        s_past = jnp.sum(kc_refs[g][...] * qg[None], axis=2, keepdims=True) * scale
        s_new = jnp.sum(kn * qg, axis=1, keepdims=True) * scale
        m = jnp.maximum(jnp.max(s_past, axis=0), s_new)
        p_past = jnp.exp(s_past - m[None])
        p_new = jnp.exp(s_new - m)
        den = jnp.sum(p_past, axis=0) + p_new
        o = jnp.sum(p_past * vc_refs[g][...], axis=0) + p_new * vn
        outs.append(o / den)
        lses.append(m + jnp.log(den))
    top = functools.reduce(jnp.maximum, lses)
    es = [jnp.exp(l - top) for l in lses]
    inv = 1.0 / functools.reduce(jnp.add, es)
    att_ref[...] = functools.reduce(jnp.add, [e * inv * o for e, o in zip(es, outs)])


def _attn_sample(q, k_new, v_new, cache_k, cache_v):
    b, w, h, e = cache_k.shape
    cache_args, cache_specs = [], []
    for cache in (cache_k, cache_v):
        for win, dil in DILATED_GROUPS:
            n_keys = win // dil
            last = w // dil // n_keys - 1
            cache_args.append(cache.reshape(b, w // dil, dil, h, e))
            cache_specs.append(pl.BlockSpec((None, n_keys, None, h, e),
                                            lambda i, last=last: (i, last, 0, 0, 0)))
    head = pl.BlockSpec((None, h, e), lambda i: (i, 0, 0))
    return pl.pallas_call(
        _attn_sample_kernel, grid=(b,),
        in_specs=[pl.BlockSpec((None, N_DIL, h, e), lambda i: (i, 0, 0, 0)), head, head] + cache_specs,
        out_specs=head, out_shape=jax.ShapeDtypeStruct((b, h, e), F32),
        compiler_params=_params(1), name="attn_sample")(q, k_new, v_new, *cache_args)


def _dense_residual_kernel(a_ref, x_ref, w_ref, o_ref):
    o_ref[...] = x_ref[...] + jnp.dot(a_ref[...].astype(BF16), w_ref[...], preferred_element_type=F32)


def _dense_residual(a, x, w):
    n, d = x.shape
    return pl.pallas_call(
        _dense_residual_kernel, grid=(1,),
        in_specs=[_full_spec(a.shape), _full_spec((n, d)), _full_spec(w.shape)], out_specs=_full_spec((n, d)),
        out_shape=jax.ShapeDtypeStruct((n, d), F32),
        compiler_params=_params(1), name="dense_residual")(a, x, w)


def kernel(x_prompt, x_sample, state_pool, cache_k, cache_v, ffn1_norm, ffn1_w_in, ffn1_w_out, mix_norm,
           ffn2_norm, ffn2_w_in, ffn2_w_out, pool_w_group, pool_scale, pool_w_o, kv_norm, w_kv, attn_w_q,
           attn_w_o, final_norm):
    b, s, d = x_prompt.shape
    bs = x_sample.shape[0]
    assert x_sample.shape[1] == 1
    n_heads = d // HEAD_DIM
    w_cache = cache_k.shape[1]
    assert w_cache == min(WINDOW_MAX, PAST_LEN)
    keep = min(WINDOW_MAX, s)

    bf = lambda a: a.astype(BF16)
    vec = lambda a: a.reshape(1, d)
    ffn1 = [(vec(ffn1_norm[l]), bf(ffn1_w_in[l]), bf(ffn1_w_out[l])) for l in range(2)]
    ffn2 = [(vec(ffn2_norm[l]), bf(ffn2_w_in[l]), bf(ffn2_w_out[l])) for l in range(2)]
    wg, wpo, wkv, wq, wo = bf(pool_w_group[0]), bf(pool_w_o[0]), bf(w_kv), bf(attn_w_q[0]), bf(attn_w_o[0])

    tab_p = _rope_tables(jnp.arange(s, dtype=jnp.int32))
    tab_s = _rope_tables(jnp.full((bs,), PAST_LEN, dtype=jnp.int32))

    xs = _ffn_sample(x_sample.reshape(bs, d), ffn1[0])
    hist = jnp.swapaxes(state_pool[0], 0, 1)
    xs, hist_new = _pool_sample(xs, hist, vec(mix_norm[0]), wg, vec(pool_scale[0]), wpo)
    pool_s = jnp.swapaxes(hist_new, 0, 1)[None]
    xs = _ffn_sample(xs, ffn2[0])
    kv_new = jnp.swapaxes(_proj_sample(xs, vec(kv_norm), wkv, tab_s, n_heads), 0, 1)
    k_new, v_new = kv_new[:, :n_heads], kv_new[:, n_heads:]
    xs = _ffn_sample(xs, ffn1[1])
    q_new = jnp.swapaxes(_proj_sample(xs, vec(mix_norm[1]), wq, tab_s, N_DIL * n_heads), 0, 1)
    att = _attn_sample(q_new.reshape(bs, N_DIL, n_heads, HEAD_DIM), k_new, v_new, cache_k, cache_v)
    xs = _dense_residual(att.reshape(bs, d), xs, wo)
    y_sample = _ffn_sample(xs, ffn2[1], vec(final_norm)).reshape(bs, 1, d)

    x = x_prompt.reshape(b * s, d)
    x, xn, pool_hist, k_s = _layer0_head(x, s, ffn1[0], vec(mix_norm[0]), wg, vec(pool_scale[0]), wpo,
                                         ffn2[0][0], cache_k, k_new[:, None])
    pool_p = pool_hist[:, HALO - POOL_BUF:][None]
    x, xn, k_nat, v_nat, k_str, v_str, k_keep, v_keep, v_s = _layer0_tail(
        x, xn, s, keep, ffn2[0], vec(kv_norm), wkv, ffn1[1][0], tab_p, cache_v, v_new[:, None])
    x, *qs_by_group = _layer1_head(x, xn, s, ffn1[1], vec(mix_norm[1]), wq, tab_p)
    outs, lses = [], []
    for (win, dil), q in zip(DILATED_GROUPS, qs_by_group):
        assert win // dil == ATTN_BLOCK
        if dil == 1:
            nat = lambda a: a.reshape(b, 1, s, d)
            o, lse = _attn_prompt(nat(q), nat(k_nat), nat(v_nat), 1)
            outs.append(o.reshape(b * s, d))
            lses.append(lse.reshape(b * s, LANES))
        else:
            o, lse = _attn_prompt(q, k_str, v_str, dil)
            outs.append(o)
            lses.append(lse)
    y_prompt = _layer1_tail(outs, lses, x, s, wo, ffn2[1], vec(final_norm)).reshape(b, s, d)
    k_p = k_keep.reshape(b, keep, n_heads, HEAD_DIM)
    v_p = v_keep.reshape(b, keep, n_heads, HEAD_DIM)

    return (y_prompt, y_sample, pool_p, pool_s, k_p, v_p, k_s, v_s)
```

```python
import functools
from typing import NamedTuple

import jax
import jax.numpy as jnp
from jax import lax
from jax.experimental import pallas as pl
from jax.experimental.pallas import tpu as pltpu

F32 = jnp.float32
BF16 = jnp.bfloat16

POOL_WINDOWS = (2, 4, 8, 16)
POOL_BUF = max(POOL_WINDOWS) - 1
HEAD_DIM = 128
DILATED_GROUPS = ((128, 1), (512, 4), (2048, 16))
N_DIL = len(DILATED_GROUPS)
WINDOW_MAX = max(w for w, _ in DILATED_GROUPS)
STRIDE = max(d for _, d in DILATED_GROUPS)
ATTN_BLOCK = 128
ROT_DIM = HEAD_DIM // 4
ROPE_THETA = 500000.0
RMS_EPS = 1e-6
PAST_LEN = 16384
LOG2_E = 1.4426950408889634

V7X_VMEM_BYTES = 64 * 1024 * 1024
VMEM_LIMIT = (V7X_VMEM_BYTES * 29) // 32
LANES = 128
MXU_WIDTH = 256
HALO = 16

ROW_TILE = 256
BIG_TILE = 512
ATTN_QUERIES = 1024
SOFTMAX_ROWS = 128


def _const_spec(shape):
    n = len(shape)
    return pl.BlockSpec(shape, lambda *_: (0,) * n, pipeline_mode=pl.Buffered(1))


def _full_spec(shape):
    return pl.BlockSpec(shape, lambda i: (0,) * len(shape))


class _Ffn(NamedTuple):
    gain: jax.Array
    w_in: jax.Array
    w_out: jax.Array
    layer: int

    @property
    def d_ff(self):
        return self.w_out.shape[1]

    def weight_specs(self):
        def spec(w):
            return pl.BlockSpec((None,) + w.shape[1:], lambda *_: (self.layer, 0, 0),
                                pipeline_mode=pl.Buffered(1))
        return [spec(self.w_in), spec(self.w_out)]


def _params(n_axes):
    return pltpu.CompilerParams(dimension_semantics=("arbitrary",) * n_axes,
                                vmem_limit_bytes=VMEM_LIMIT)


def _rms(x, g):
    return x * lax.rsqrt(jnp.mean(x * x, axis=-1, keepdims=True) + RMS_EPS) * g


def _col_chunks(width):
    return [(c, c + MXU_WIDTH) for c in range(0, width, MXU_WIDTH)]


def _normalise_into(dst_ref, src_ref, g_ref):
    dst_ref[...] = _rms(src_ref[...], g_ref[...]).astype(dst_ref.dtype)


def _swiglu_half(src_ref, xn_ref, win_ref, wout_ref, h_ref, emit, side_jobs=()):
    d_ff, d = wout_ref.shape
    ff_chunks, out_chunks = _col_chunks(d_ff), _col_chunks(d)
    n_seg = len(ff_chunks) + len(out_chunks)
    jobs = list(side_jobs)
    done = [0, 0]

    def end_segment():
        done[0] += 1
        while done[1] < len(jobs) and done[1] * n_seg < done[0] * len(jobs):
            jobs[done[1]]()
            done[1] += 1

    for c0, c1 in ff_chunks:
        gate = jnp.dot(xn_ref[...], win_ref[:, c0:c1], preferred_element_type=F32)
        up = jnp.dot(xn_ref[...], win_ref[:, d_ff + c0:d_ff + c1], preferred_element_type=F32)
        h_ref[:, c0:c1] = (gate * jax.nn.sigmoid(gate) * up).astype(BF16)
        end_segment()
    for c0, c1 in out_chunks:
        emit(c0, c1, src_ref[:, c0:c1] + 0.5 * jnp.dot(h_ref[...], wout_ref[:, c0:c1],
                                                        preferred_element_type=F32))
        end_segment()


def _pool_trailing(ext_ref, row0, rows, j, gi, w, gd):
    c0, c1 = gi * gd, (gi + 1) * gd
    cur = ext_ref[row0:row0 + rows, c0:c1]
    tot = cur
    for s in range(1, w):
        tot = tot + ext_ref[row0 - s:row0 - s + rows, c0:c1]
    if j is None:
        return tot * (1.0 / w) - cur
    pos1 = (j * rows + 1 + lax.broadcasted_iota(jnp.int32, (rows, 1), 0)).astype(F32)
    return tot * (1.0 / jnp.minimum(pos1, float(w))) - cur


def _pool_jobs(src_ref, j, g_ref, wg_ref, scale_ref, wo_ref, ext_ref, z_ref, o_ref, state_ref, g_next_ref,
               xn_next_ref):
    tm, d = src_ref.shape
    gd = d // len(POOL_WINDOWS)

    def normalise():
        h = _rms(src_ref[...], g_ref[...])
        ext_ref[0:HALO, :] = jnp.where(j == 0, 0.0, ext_ref[tm:tm + HALO, :])
        ext_ref[HALO:HALO + tm, :] = h
        state_ref[...] = h[tm - HALO:, :]

    def group(gi, w):
        p = _pool_trailing(ext_ref, HALO, tm, j, gi, w, gd).astype(BF16)
        z = jnp.dot(p, wg_ref[gi], preferred_element_type=F32) * scale_ref[:, gi * gd:(gi + 1) * gd]
        z_ref[:, gi * gd:(gi + 1) * gd] = z.astype(BF16)

    def project(c0, c1):
        o_ref[:, c0:c1] = src_ref[:, c0:c1] + jnp.dot(z_ref[...], wo_ref[:, c0:c1],
                                                      preferred_element_type=F32)

    return ([normalise] + [functools.partial(group, gi, w) for gi, w in enumerate(POOL_WINDOWS)]
            + [functools.partial(project, c0, c1) for c0, c1 in _col_chunks(d)]
            + [functools.partial(_normalise_into, xn_next_ref, o_ref, g_next_ref)])


def _rope_tables(pos):
    half = ROT_DIM // 2
    inv_freq = ROPE_THETA ** (-jnp.arange(0, ROT_DIM, 2, dtype=F32) / ROT_DIM)
    ang = pos.astype(F32)[:, None] * inv_freq[None, :]
    cos, sin = jnp.cos(ang), jnp.sin(ang)
    t = pos.shape[0]
    c = jnp.concatenate([cos, cos, jnp.ones((t, HEAD_DIM - ROT_DIM), F32)], axis=1)
    a = jnp.concatenate([-sin, jnp.zeros((t, HEAD_DIM - half), F32)], axis=1)
    b = jnp.concatenate([jnp.zeros((t, half), F32), sin, jnp.zeros((t, HEAD_DIM - ROT_DIM), F32)], axis=1)
    return c, a, b


def _projection_jobs(xn_ref, w_ref, col0, table_refs, y_ref, store_head):
    n_slots = y_ref.shape[0]
    half = ROT_DIM // 2
    per_dot = MXU_WIDTH // HEAD_DIM

    def project(h0):
        cols = slice(col0 + h0 * HEAD_DIM, col0 + (h0 + per_dot) * HEAD_DIM)
        y = jnp.dot(xn_ref[...], w_ref[:, cols], preferred_element_type=F32)
        for k in range(per_dot):
            y_ref[h0 + k] = y[:, k * HEAD_DIM:(k + 1) * HEAD_DIM]

    def finish(hh):
        if table_refs is not None:
            c, a, b = (t[...] for t in table_refs)
            yh = y_ref[hh]
            y_ref[hh] = yh * c + pltpu.roll(yh, HEAD_DIM - half, 1) * a + pltpu.roll(yh, half, 1) * b
        store_head(hh)

    jobs = []
    for h0 in range(0, n_slots, per_dot):
        jobs.append(functools.partial(project, h0))
        jobs += [functools.partial(finish, h0 + k) for k in range(per_dot)]
    return jobs


def _store_head_natural(src_ref, hh, dst_ref):
    dst_ref[:, hh * HEAD_DIM:(hh + 1) * HEAD_DIM] = src_ref[hh].astype(dst_ref.dtype)


def _store_head_strided(src_ref, hh, dst_ref):
    per = src_ref.shape[1] // STRIDE
    for r in range(STRIDE):
        dst_ref[r, :, hh * HEAD_DIM:(hh + 1) * HEAD_DIM] = (
            src_ref[hh, pl.ds(r, per, stride=STRIDE), :].astype(dst_ref.dtype))


def _roll_tile(seq_end, src_ref, nxt_ref, new_ref, dst_ref):
    t = src_ref.shape[0]
    dst_ref[0:t - 1] = src_ref[1:t]
    dst_ref[t - 1] = jnp.where(seq_end, new_ref[0], nxt_ref[0])


def _combine_jobs(o_refs, lse_refs, x_ref, wo_ref, o_nat, lse_nat, w_ref, att_ref, dst_ref, g_next_ref,
                  xn_next_ref):
    tm, d = x_ref.shape
    per = tm // STRIDE
    n_heads = d // HEAD_DIM
    heads, lses = [], []
    jobs = []
    slot = 0
    for (_, dil), o_ref, lse_ref in zip(DILATED_GROUPS, o_refs, lse_refs):
        if dil == 1:
            heads.append(functools.partial(
                lambda hh, ref: ref[:, hh * HEAD_DIM:(hh + 1) * HEAD_DIM].astype(F32), ref=o_ref))
            lses.append(functools.partial(lambda ref: ref[...], lse_ref))
            continue

        def to_natural(r, o_ref=o_ref, lse_ref=lse_ref, slot=slot):
            lse_nat[slot, pl.ds(r, per, stride=STRIDE), :] = lse_ref[r]
            for hh in range(n_heads):
                o_nat[slot * n_heads + hh, pl.ds(r, per, stride=STRIDE), :] = (
                    o_ref[r, :, hh * HEAD_DIM:(hh + 1) * HEAD_DIM].astype(F32))

        jobs += [functools.partial(to_natural, r) for r in range(STRIDE)]
        heads.append(functools.partial(lambda hh, s0: o_nat[s0 + hh], s0=slot * n_heads))
        lses.append(functools.partial(lambda s: lse_nat[s], slot))
        slot += 1

    def weights():
        ls = [l() for l in lses]
        top = functools.reduce(jnp.maximum, ls)
        es = [jnp.exp(l - top) for l in ls]
        inv = 1.0 / functools.reduce(jnp.add, es)
        for g, e in enumerate(es):
            w_ref[g] = e * inv

    def mix_head(hh):
        att = functools.reduce(jnp.add, [w_ref[g][:, hh:hh + 1] * heads[g](hh) for g in range(N_DIL)])
        att_ref[:, hh * HEAD_DIM:(hh + 1) * HEAD_DIM] = att.astype(BF16)

    def project(c0, c1):
        dst_ref[:, c0:c1] = x_ref[:, c0:c1] + jnp.dot(att_ref[...], wo_ref[:, c0:c1],
                                                      preferred_element_type=F32)

    return (jobs + [weights] + [functools.partial(mix_head, hh) for hh in range(n_heads)]
            + [functools.partial(project, c0, c1) for c0, c1 in _col_chunks(d)]
            + [functools.partial(_normalise_into, xn_next_ref, dst_ref, g_next_ref)])


class _Tiling:
    def __init__(self, n_rows, seq, tm):
        self.tm = tm
        self.n_tiles = n_rows // tm
        self.per_seq = seq // tm
        self.steps = self.n_tiles + 1

    def lead(self, i):
        return jnp.minimum(i, self.n_tiles - 1)

    def trail(self, i):
        return jnp.maximum(i - 1, 0)

    def row(self, width, tile_of):
        return pl.BlockSpec((self.tm, width), lambda i: (tile_of(i), 0))

    def strided(self, width, tile_of):
        return pl.BlockSpec((None, STRIDE, self.tm // STRIDE, width),
                            lambda i: (tile_of(i) // self.per_seq, 0, tile_of(i) % self.per_seq, 0))

    def table(self, tile_of):
        return pl.BlockSpec((self.tm, HEAD_DIM), lambda i: (tile_of(i) % self.per_seq, 0))


def _roll_operands(cache, new, tl):
    b, w, h, e = cache.shape
    per = tl.n_tiles // b
    t = w // per
    blk = pl.BlockSpec((None, t, h, e), lambda i: (tl.lead(i) // per, tl.lead(i) % per, 0, 0))
    nxt = pl.BlockSpec((None, 1, h, e),
                       lambda i: (tl.lead(i) // per, jnp.minimum((tl.lead(i) % per + 1) * t, w - 1), 0, 0))
    cur_new = pl.BlockSpec((None, 1, h, e), lambda i: (tl.lead(i) // per, 0, 0, 0))
    return per, [blk, nxt, cur_new], [cache, cache, new], blk


def _zero_on_first_step(i, *refs):
    @pl.when(i == 0)
    def _():
        for ref in refs:
            ref[...] = jnp.zeros(ref.shape, ref.dtype)


def _run_all(jobs):
    for job in jobs:
        job()


def _with_job_in_middle(jobs, job):
    jobs = list(jobs)
    jobs.insert(len(jobs) // 2, job)
    return jobs


def _run_stages(both_have_a_tile, both, alone):
    pl.when(both_have_a_tile)(both)
    pl.when(jnp.logical_not(both_have_a_tile))(alone)


def _emit_to(*refs):
    def emit(c0, c1, value):
        for ref in refs:
            ref[:, c0:c1] = value
    return emit


def _layer0_head_kernel(x_ref, g1_ref, win_ref, wout_ref, gm_ref, wg_ref, scale_ref, wo_ref, gn_ref,
                        src_ref, nxt_ref, new_ref, o_ref, on_ref, state_ref, dst_ref,
                        h_ref, xn_ref, xs_ref, ext_ref, z_ref, *, tl, roll_per_seq):
    i = pl.program_id(0)
    _zero_on_first_step(i, xs_ref, ext_ref)
    trailing = _pool_jobs(xs_ref.at[(i + 1) % 2], tl.trail(i) % tl.per_seq, gm_ref, wg_ref, scale_ref, wo_ref,
                          ext_ref, z_ref, o_ref, state_ref, gn_ref, on_ref)
    roll = functools.partial(
        _roll_tile, tl.lead(i) % roll_per_seq == roll_per_seq - 1, src_ref, nxt_ref, new_ref, dst_ref)

    def both():
        _normalise_into(xn_ref, x_ref, g1_ref)
        _swiglu_half(x_ref, xn_ref, win_ref, wout_ref, h_ref, _emit_to(xs_ref.at[i % 2]),
                     _with_job_in_middle(trailing, roll))

    _run_stages(i < tl.n_tiles, both, functools.partial(_run_all, trailing))


def _layer0_head(x, seq, ffn, g_mix, w_group, scale, w_o, g_next, cache, new):
    n, d = x.shape
    tl = _Tiling(n, seq, BIG_TILE)
    g1, w_in, w_out, _ = ffn
    roll_per_seq, roll_specs, roll_args, roll_out = _roll_operands(cache, new, tl)
    return pl.pallas_call(
        functools.partial(_layer0_head_kernel, tl=tl, roll_per_seq=roll_per_seq), grid=(tl.steps,),
        in_specs=[tl.row(d, tl.lead), _const_spec((1, d)), *ffn.weight_specs(),
                  _const_spec((1, d)), _const_spec(w_group.shape), _const_spec((1, d)), _const_spec(w_o.shape),
                  _const_spec((1, d))] + roll_specs,
        out_specs=[tl.row(d, tl.trail), tl.row(d, tl.trail),
                   pl.BlockSpec((None, HALO, d), lambda i: (tl.trail(i) // tl.per_seq, 0, 0)), roll_out],
        out_shape=[jax.ShapeDtypeStruct((n, d), F32), jax.ShapeDtypeStruct((n, d), BF16),
                   jax.ShapeDtypeStruct((n // seq, HALO, d), F32),
                   jax.ShapeDtypeStruct(cache.shape, cache.dtype)],
        scratch_shapes=[pltpu.VMEM((tl.tm, ffn.d_ff), BF16), pltpu.VMEM((tl.tm, d), BF16),
                        pltpu.VMEM((2, tl.tm, d), F32), pltpu.VMEM((tl.tm + HALO, d), F32),
                        pltpu.VMEM((tl.tm, d), BF16)],
        compiler_params=_params(1), name="layer0_head")(
            x, g1, w_in, w_out, g_mix, w_group, scale, w_o, g_next, *roll_args)


def _layer0_tail_kernel(x_ref, xn_ref, win_ref, wout_ref, gkv_ref, wkv_ref, gn_ref, c_ref, a_ref, b_ref,
                        src_ref, nxt_ref, new_ref,
                        o_ref, on_ref, k_ref, v_ref, ks_ref, vs_ref, kf_ref, vf_ref, dst_ref,
                        h_ref, xs_ref, xkv_ref, yk_ref, yv_ref, *, tl, roll_per_seq, kept_tiles):
    i = pl.program_id(0)
    _zero_on_first_step(i, xs_ref)
    d = x_ref.shape[1]

    def normalise():
        x = xs_ref[(i + 1) % 2]
        y = x * lax.rsqrt(jnp.mean(x * x, axis=-1, keepdims=True) + RMS_EPS)
        xkv_ref[...] = (y * gkv_ref[...]).astype(BF16)
        on_ref[...] = (y * gn_ref[...]).astype(BF16)

    def store_both(y_ref, nat_ref, str_ref, hh):
        _store_head_natural(y_ref, hh, nat_ref)
        _store_head_strided(y_ref, hh, str_ref)

    trailing = [normalise]
    trailing += _projection_jobs(xkv_ref, wkv_ref, 0, (c_ref, a_ref, b_ref), yk_ref,
                                 functools.partial(store_both, yk_ref, k_ref, ks_ref))
    trailing += _projection_jobs(xkv_ref, wkv_ref, d, None, yv_ref,
                                 functools.partial(store_both, yv_ref, v_ref, vs_ref))
    roll = functools.partial(
        _roll_tile, tl.lead(i) % roll_per_seq == roll_per_seq - 1, src_ref, nxt_ref, new_ref, dst_ref)

    def both():
        _swiglu_half(x_ref, xn_ref, win_ref, wout_ref, h_ref, _emit_to(o_ref, xs_ref.at[i % 2]),
                     _with_job_in_middle(trailing, roll))

    _run_stages(i < tl.n_tiles, both, functools.partial(_run_all, trailing))

    @pl.when(tl.trail(i) % tl.per_seq >= tl.per_seq - kept_tiles)
    def _():
        for hh in range(yk_ref.shape[0]):
            kf_ref[:, hh, :] = yk_ref[hh]
            vf_ref[:, hh, :] = yv_ref[hh]


def _layer0_tail(x, xn, seq, keep, ffn, g_kv, w_kv, g_next, tables, cache, new):
    n, d = x.shape
    b = n // seq
    n_heads = d // HEAD_DIM
    tl = _Tiling(n, seq, ROW_TILE)
    _, w_in, w_out, _ = ffn
    kept = keep // tl.tm
    roll_per_seq, roll_specs, roll_args, roll_out = _roll_operands(cache, new, tl)

    def kept_block(i):
        t = tl.trail(i)
        return ((t // tl.per_seq) * kept + jnp.maximum(t % tl.per_seq - (tl.per_seq - kept), 0), 0, 0)

    kept_spec = pl.BlockSpec((tl.tm, n_heads, HEAD_DIM), kept_block)
    nat_shape = jax.ShapeDtypeStruct((n, d), BF16)
    str_shape = jax.ShapeDtypeStruct((b, STRIDE, seq // STRIDE, d), BF16)
    kept_shape = jax.ShapeDtypeStruct((b * keep, n_heads, HEAD_DIM), F32)
    head_scratch = pltpu.VMEM((n_heads, tl.tm, HEAD_DIM), F32)
    return pl.pallas_call(
        functools.partial(_layer0_tail_kernel, tl=tl, roll_per_seq=roll_per_seq, kept_tiles=kept),
        grid=(tl.steps,),
        in_specs=[tl.row(d, tl.lead), tl.row(d, tl.lead), *ffn.weight_specs(),
                  _const_spec((1, d)), _const_spec(w_kv.shape), _const_spec((1, d))]
        + [tl.table(tl.trail)] * 3 + roll_specs,
        out_specs=[tl.row(d, tl.lead), tl.row(d, tl.trail), tl.row(d, tl.trail), tl.row(d, tl.trail),
                   tl.strided(d, tl.trail), tl.strided(d, tl.trail), kept_spec, kept_spec, roll_out],
        out_shape=[jax.ShapeDtypeStruct((n, d), F32), nat_shape, nat_shape, nat_shape, str_shape, str_shape,
                   kept_shape, kept_shape, jax.ShapeDtypeStruct(cache.shape, cache.dtype)],
        scratch_shapes=[pltpu.VMEM((tl.tm, ffn.d_ff), BF16), pltpu.VMEM((2, tl.tm, d), F32),
                        pltpu.VMEM((tl.tm, d), BF16), head_scratch, head_scratch],
        compiler_params=_params(1), name="layer0_tail")(
            x, xn, w_in, w_out, g_kv, w_kv, g_next, *tables, *roll_args)


def _layer1_head_kernel(x_ref, xn_ref, win_ref, wout_ref, gm_ref, wq_ref, c_ref, a_ref, b_ref, o_ref, *refs, tl):
    q_refs, (h_ref, xs_ref, xq_ref, y_ref) = refs[:N_DIL], refs[N_DIL:]
    i = pl.program_id(0)
    _zero_on_first_step(i, xs_ref)
    d = x_ref.shape[1]
    n_heads = d // HEAD_DIM

    def normalise():
        xq_ref[...] = _rms(xs_ref[(i + 1) % 2], gm_ref[...]).astype(BF16)

    trailing = [normalise]
    for g, ((_, dil), q_ref) in enumerate(zip(DILATED_GROUPS, q_refs)):
        yg_ref = y_ref.at[g * n_heads:(g + 1) * n_heads]
        store = _store_head_natural if dil == 1 else _store_head_strided
        trailing += _projection_jobs(xq_ref, wq_ref, g * d, (c_ref, a_ref, b_ref), yg_ref,
                                     functools.partial(lambda hh, st, src, dst: st(src, hh, dst),
                                                       st=store, src=yg_ref, dst=q_ref))

    def both():
        _swiglu_half(x_ref, xn_ref, win_ref, wout_ref, h_ref, _emit_to(o_ref, xs_ref.at[i % 2]), trailing)

    _run_stages(i < tl.n_tiles, both, functools.partial(_run_all, trailing))


def _layer1_head(x, xn, seq, ffn, g_mix, w_q, tables):
    n, d = x.shape
    b = n // seq
    tl = _Tiling(n, seq, BIG_TILE)
    _, w_in, w_out, _ = ffn
    q_specs, q_shapes = [], []
    for _, dil in DILATED_GROUPS:
        q_specs.append(tl.row(d, tl.trail) if dil == 1 else tl.strided(d, tl.trail))
        q_shapes.append(jax.ShapeDtypeStruct((n, d) if dil == 1 else (b, STRIDE, seq // STRIDE, d), BF16))
    return pl.pallas_call(
        functools.partial(_layer1_head_kernel, tl=tl), grid=(tl.steps,),
        in_specs=[tl.row(d, tl.lead), tl.row(d, tl.lead), *ffn.weight_specs(),
                  _const_spec((1, d)), _const_spec(w_q.shape)] + [tl.table(tl.trail)] * 3,
        out_specs=[tl.row(d, tl.lead)] + q_specs,
        out_shape=[jax.ShapeDtypeStruct((n, d), F32)] + q_shapes,
        scratch_shapes=[pltpu.VMEM((tl.tm, ffn.d_ff), BF16), pltpu.VMEM((2, tl.tm, d), F32),
                        pltpu.VMEM((tl.tm, d), BF16),
                        pltpu.VMEM((w_q.shape[1] // HEAD_DIM, tl.tm, HEAD_DIM), F32)],
        compiler_params=_params(1), name="layer1_head")(x, xn, w_in, w_out, g_mix, w_q, *tables)


def _layer1_tail_kernel(*refs):
    o_refs, lse_refs = refs[:N_DIL], refs[N_DIL:2 * N_DIL]
    (x_ref, wo_ref, g2_ref, win_ref, wout_ref, gf_ref, out_ref,
     h_ref, xs_ref, xns_ref, o_nat, lse_nat, w_ref, att_ref) = refs[2 * N_DIL:]
    i = pl.program_id(0)
    leading = _combine_jobs(o_refs, lse_refs, x_ref, wo_ref, o_nat, lse_nat, w_ref, att_ref,
                            xs_ref.at[i % 2], g2_ref, xns_ref.at[i % 2])

    def both():
        _swiglu_half(xs_ref.at[(i + 1) % 2], xns_ref.at[(i + 1) % 2], win_ref, wout_ref, h_ref,
                     _emit_to(out_ref), leading)
        out_ref[...] = _rms(out_ref[...], gf_ref[...])

    _run_stages(i > 0, both, functools.partial(_run_all, leading))


def _layer1_tail(outs, lses, x, seq, w_o, ffn, g_final):
    n, d = x.shape
    tl = _Tiling(n, seq, BIG_TILE)
    g2, w_in, w_out, _ = ffn

    def group_specs(width):
        return [tl.row(width, tl.lead) if dil == 1 else tl.strided(width, tl.lead) for _, dil in DILATED_GROUPS]

    n_strided = sum(dil > 1 for _, dil in DILATED_GROUPS)
    return pl.pallas_call(
        _layer1_tail_kernel, grid=(tl.steps,),
        in_specs=group_specs(d) + group_specs(LANES)
        + [tl.row(d, tl.lead), _const_spec(w_o.shape), _const_spec((1, d)), *ffn.weight_specs(),
           _const_spec((1, d))],
        out_specs=tl.row(d, tl.trail), out_shape=jax.ShapeDtypeStruct((n, d), F32),
        scratch_shapes=[pltpu.VMEM((tl.tm, ffn.d_ff), BF16), pltpu.VMEM((2, tl.tm, d), F32),
                        pltpu.VMEM((2, tl.tm, d), BF16),
                        pltpu.VMEM((n_strided * (d // HEAD_DIM), tl.tm, HEAD_DIM), F32),
                        pltpu.VMEM((n_strided, tl.tm, LANES), F32), pltpu.VMEM((N_DIL, tl.tm, LANES), F32),
                        pltpu.VMEM((tl.tm, d), BF16)],
        compiler_params=_params(1), name="layer1_tail")(*outs, *lses, x, w_o, g2, w_in, w_out, g_final)


def _attn_prompt_kernel(q_ref, kc_ref, kp_ref, vc_ref, vp_ref, o_ref, lse_ref, kk_ref, vv_ref):
    n = pl.program_id(2)
    nc, rows, d = q_ref.shape
    blk = ATTN_BLOCK
    sb = blk // nc
    n_heads = d // HEAD_DIM
    for c in range(nc):
        kk_ref[c, 0:sb, :] = kp_ref[c]
        kk_ref[c, sb:, :] = kc_ref[c]
        vv_ref[c, 0:sb, :] = vp_ref[c]
        vv_ref[c, sb:, :] = vc_ref[c]
    lane = lax.broadcasted_iota(jnp.int32, (SOFTMAX_ROWS, LANES), 1)
    scale = HEAD_DIM ** -0.5

    def keys(ref, u, sl):
        parts = [ref[c, u * sb:(u + 2) * sb, sl] for c in range(nc)]
        return parts[0] if nc == 1 else jnp.concatenate(parts, axis=0)

    def slab_ranges(r0, r1):
        spans = [(c, max(c * sb, r0), min((c + 1) * sb, r1)) for c in range(nc)]
        return [(c, lo - c * sb, lo, hi) for c, lo, hi in spans if lo < hi]

    for r0 in range(0, blk, SOFTMAX_ROWS):
        r1 = r0 + SOFTMAX_ROWS
        qrow = r0 + lax.broadcasted_iota(jnp.int32, (SOFTMAX_ROWS, 2 * blk), 0)
        kcol = lax.broadcasted_iota(jnp.int32, (SOFTMAX_ROWS, 2 * blk), 1)
        dist = nc * (qrow % sb - kcol % (2 * sb) + sb) + (qrow // sb - kcol // (2 * sb))
        band = (dist >= 0) & (dist <= blk)
        for u in range(rows // sb):
            valid = band
            if u == 0:
                valid = band & ((kcol % (2 * sb) >= sb) | (n > 0))
            lse_tile = jnp.zeros((SOFTMAX_ROWS, LANES), F32)
            for hh in range(n_heads):
                sl = slice(hh * HEAD_DIM, (hh + 1) * HEAD_DIM)
                q_parts = [q_ref[c, u * sb + off:u * sb + off + hi - lo, sl]
                           for c, off, lo, hi in slab_ranges(r0, r1)]
                qh = q_parts[0] if len(q_parts) == 1 else jnp.concatenate(q_parts, axis=0)
                s = lax.dot_general(qh, keys(kk_ref, u, sl), (((1,), (1,)), ((), ())),
                                    preferred_element_type=F32)
                s = jnp.where(valid, s, -jnp.inf)
                m = jnp.max(s, axis=1, keepdims=True)
                p = jnp.exp2((s - m) * (scale * LOG2_E))
                den = jnp.sum(p, axis=1, keepdims=True)
                o = jnp.dot(p.astype(BF16), keys(vv_ref, u, sl), preferred_element_type=F32) / den
                for c, off, lo, hi in slab_ranges(r0, r1):
                    o_ref[c, u * sb + off:u * sb + off + hi - lo, sl] = o[lo - r0:hi - r0].astype(o_ref.dtype)
                lse_tile = jnp.where(lane == hh, m * scale + jnp.log(den), lse_tile)
            for c, off, lo, hi in slab_ranges(r0, r1):
                lse_ref[c, u * sb + off:u * sb + off + hi - lo, :] = lse_tile[lo - r0:hi - r0]


def _attn_prompt(q, k, v, dil):
    b, r_all, length, d = q.shape
    nc = r_all // dil
    sb = ATTN_BLOCK // nc
    rows = min(ATTN_QUERIES // nc, length)
    view = lambda a: a.reshape(b, nc, dil, length, a.shape[-1])
    cur = pl.BlockSpec((None, nc, None, rows, d), lambda i, r, n: (i, 0, r, n, 0))
    prev = pl.BlockSpec((None, nc, None, sb, d),
                        lambda i, r, n: (i, 0, r, jnp.maximum(n * (rows // sb) - 1, 0), 0))
    o, lse = pl.pallas_call(
        _attn_prompt_kernel, grid=(b, dil, length // rows),
        in_specs=[cur, cur, prev, cur, prev],
        out_specs=[cur, pl.BlockSpec((None, nc, None, rows, LANES), lambda i, r, n: (i, 0, r, n, 0))],
        out_shape=[jax.ShapeDtypeStruct((b, nc, dil, length, d), BF16),
                   jax.ShapeDtypeStruct((b, nc, dil, length, LANES), F32)],
        scratch_shapes=[pltpu.VMEM((nc, rows + sb, d), BF16), pltpu.VMEM((nc, rows + sb, d), BF16)],
        compiler_params=_params(3), name=f"attn_prompt_d{dil}")(view(q), view(k), view(k), view(v), view(v))
    return o.reshape(b, r_all, length, d), lse.reshape(b, r_all, length, LANES)


def _ffn_kernel(x_ref, g_ref, win_ref, wout_ref, o_ref, h_ref, xn_ref):
    _normalise_into(xn_ref, x_ref, g_ref)
    _swiglu_half(x_ref, xn_ref, win_ref, wout_ref, h_ref, _emit_to(o_ref))


def _ffn_final_kernel(x_ref, g_ref, win_ref, wout_ref, gf_ref, o_ref, h_ref, xn_ref):
    _ffn_kernel(x_ref, g_ref, win_ref, wout_ref, o_ref, h_ref, xn_ref)
    o_ref[...] = _rms(o_ref[...], gf_ref[...])


def _ffn_sample(x, ffn, final_g=None):
    n, d = x.shape
    g, w_in, w_out, _ = ffn
    in_specs = [_full_spec((n, d)), _full_spec((1, d)), *ffn.weight_specs()]
    args = [x, g, w_in, w_out]
    body = _ffn_kernel
    if final_g is not None:
        in_specs.append(_full_spec((1, d)))
        args.append(final_g)
        body = _ffn_final_kernel
    return pl.pallas_call(
        body, grid=(1,), in_specs=in_specs, out_specs=_full_spec((n, d)),
        out_shape=jax.ShapeDtypeStruct((n, d), F32),
        scratch_shapes=[pltpu.VMEM((n, ffn.d_ff), BF16), pltpu.VMEM((n, d), BF16)],
        compiler_params=_params(1), name="ffn_sample")(*args)


def _pool_sample_kernel(x_ref, st_ref, g_ref, wg_ref, scale_ref, wo_ref, o_ref, new_ref, z_ref):
    n, d = x_ref.shape
    gd = d // len(POOL_WINDOWS)
    x = x_ref[...]
    h = _rms(x, g_ref[...])
    for gi, w in enumerate(POOL_WINDOWS):
        c0, c1 = gi * gd, (gi + 1) * gd
        cur = h[:, c0:c1]
        tot = cur
        for s in range(1, w):
            tot = tot + st_ref[POOL_BUF - s, :, c0:c1]
        p = (tot * (1.0 / w) - cur).astype(BF16)
        z_ref[:, c0:c1] = (jnp.dot(p, wg_ref[gi], preferred_element_type=F32) * scale_ref[:, c0:c1]).astype(BF16)
    o_ref[...] = x + jnp.dot(z_ref[...], wo_ref[...], preferred_element_type=F32)
    for r in range(POOL_BUF - 1):
        new_ref[r] = st_ref[r + 1]
    new_ref[POOL_BUF - 1] = h


def _pool_sample(x, st, g, w_group, scale, w_o):
    n, d = x.shape
    return pl.pallas_call(
        _pool_sample_kernel, grid=(1,),
        in_specs=[_full_spec((n, d)), _full_spec(st.shape), _full_spec((1, d)), _full_spec(w_group.shape),
                  _full_spec((1, d)), _full_spec(w_o.shape)],
        out_specs=[_full_spec((n, d)), _full_spec(st.shape)],
        out_shape=[jax.ShapeDtypeStruct((n, d), F32), jax.ShapeDtypeStruct(st.shape, F32)],
        scratch_shapes=[pltpu.VMEM((n, d), BF16)],
        compiler_params=_params(1), name="pool_sample")(x, st, g, w_group, scale, w_o)


def _proj_sample_kernel(x_ref, g_ref, w_ref, c_ref, a_ref, b_ref, y_ref, xn_ref, *, n_rope_heads):
    xn_ref[...] = _rms(x_ref[...], g_ref[...]).astype(BF16)
    n_slots = y_ref.shape[0]
    stored_in_place = lambda hh: None
    jobs = _projection_jobs(xn_ref, w_ref, 0, (c_ref, a_ref, b_ref), y_ref.at[0:n_rope_heads], stored_in_place)
    if n_rope_heads < n_slots:
        jobs += _projection_jobs(xn_ref, w_ref, n_rope_heads * HEAD_DIM, None, y_ref.at[n_rope_heads:n_slots],
                                 stored_in_place)
    for job in jobs:
        job()


def _proj_sample(x, g, w, tables, n_rope_heads):
    n, d = x.shape
    n_slots = w.shape[1] // HEAD_DIM
    return pl.pallas_call(
        functools.partial(_proj_sample_kernel, n_rope_heads=n_rope_heads), grid=(1,),
        in_specs=[_full_spec((n, d)), _full_spec((1, d)), _full_spec(w.shape)]
        + [_full_spec((n, HEAD_DIM))] * 3,
        out_specs=_full_spec((n_slots, n, HEAD_DIM)),
        out_shape=jax.ShapeDtypeStruct((n_slots, n, HEAD_DIM), F32),
        scratch_shapes=[pltpu.VMEM((n, d), BF16)],
        compiler_params=_params(1), name="proj_sample")(x, g, w, *tables)


def _attn_sample_kernel(*refs):
    q_ref, kn_ref, vn_ref = refs[:3]
    kc_refs = refs[3:3 + N_DIL]
    vc_refs = refs[3 + N_DIL:3 + 2 * N_DIL]
    att_ref = refs[3 + 2 * N_DIL]
    scale = HEAD_DIM ** -0.5
    kn, vn = kn_ref[...], vn_ref[...]
    outs, lses = [], []
    for g in range(N_DIL):
        qg = q_ref[g]
        s_past = jnp.sum(kc_refs[g][...] * qg[None], axis=2, keepdims=True) * scale
        s_new = jnp.sum(kn * qg, axis=1, keepdims=True) * scale
        m = jnp.maximum(jnp.max(s_past, axis=0), s_new)
        p_past = jnp.exp(s_past - m[None])
        p_new = jnp.exp(s_new - m)
        den = jnp.sum(p_past, axis=0) + p_new
        o = jnp.sum(p_past * vc_refs[g][...], axis=0) + p_new * vn
        outs.append(o / den)
        lses.append(m + jnp.log(den))
    top = functools.reduce(jnp.maximum, lses)
    es = [jnp.exp(l - top) for l in lses]
    inv = 1.0 / functools.reduce(jnp.add, es)
    att_ref[...] = functools.reduce(jnp.add, [e * inv * o for e, o in zip(es, outs)])


def _attn_sample(q, k_new, v_new, cache_k, cache_v):
    b, w, h, e = cache_k.shape
    cache_args, cache_specs = [], []
    for cache in (cache_k, cache_v):
        for win, dil in DILATED_GROUPS:
            n_keys = win // dil
            last = w // dil // n_keys - 1
            cache_args.append(cache.reshape(b, w // dil, dil, h, e))
            cache_specs.append(pl.BlockSpec((None, n_keys, None, h, e),
                                            lambda i, last=last: (i, last, 0, 0, 0)))
    head = pl.BlockSpec((None, h, e), lambda i: (i, 0, 0))
    return pl.pallas_call(
        _attn_sample_kernel, grid=(b,),
        in_specs=[pl.BlockSpec((None, N_DIL, h, e), lambda i: (i, 0, 0, 0)), head, head] + cache_specs,
        out_specs=head, out_shape=jax.ShapeDtypeStruct((b, h, e), F32),
        compiler_params=_params(1), name="attn_sample")(q, k_new, v_new, *cache_args)


def _dense_residual_kernel(a_ref, x_ref, w_ref, o_ref):
    o_ref[...] = x_ref[...] + jnp.dot(a_ref[...].astype(BF16), w_ref[...], preferred_element_type=F32)


def _dense_residual(a, x, w):
    n, d = x.shape
    return pl.pallas_call(
        _dense_residual_kernel, grid=(1,),
        in_specs=[_full_spec(a.shape), _full_spec((n, d)), _full_spec(w.shape)], out_specs=_full_spec((n, d)),
        out_shape=jax.ShapeDtypeStruct((n, d), F32),
        compiler_params=_params(1), name="dense_residual")(a, x, w)


def kernel(x_prompt, x_sample, state_pool, cache_k, cache_v, ffn1_norm, ffn1_w_in, ffn1_w_out, mix_norm,
           ffn2_norm, ffn2_w_in, ffn2_w_out, pool_w_group, pool_scale, pool_w_o, kv_norm, w_kv, attn_w_q,
           attn_w_o, final_norm):
    b, s, d = x_prompt.shape
    bs = x_sample.shape[0]
    assert x_sample.shape[1] == 1
    n_heads = d // HEAD_DIM
    w_cache = cache_k.shape[1]
    assert w_cache == min(WINDOW_MAX, PAST_LEN)
    keep = min(WINDOW_MAX, s)

    bf = lambda a: a.astype(BF16)
    vec = lambda a: a.reshape(1, d)
    n_layers = ffn1_norm.shape[0]
    f1_in, f1_out, f2_in, f2_out = bf(ffn1_w_in), bf(ffn1_w_out), bf(ffn2_w_in), bf(ffn2_w_out)
    ffn1 = [_Ffn(vec(ffn1_norm[l]), f1_in, f1_out, l) for l in range(n_layers)]
    ffn2 = [_Ffn(vec(ffn2_norm[l]), f2_in, f2_out, l) for l in range(n_layers)]
    wg, wpo, wkv, wq, wo = bf(pool_w_group[0]), bf(pool_w_o[0]), bf(w_kv), bf(attn_w_q[0]), bf(attn_w_o[0])

    tab_p = _rope_tables(jnp.arange(s, dtype=jnp.int32))
    tab_s = _rope_tables(jnp.full((bs,), PAST_LEN, dtype=jnp.int32))

    xs = _ffn_sample(x_sample.reshape(bs, d), ffn1[0])
    hist = jnp.swapaxes(state_pool[0], 0, 1)
    xs, hist_new = _pool_sample(xs, hist, vec(mix_norm[0]), wg, vec(pool_scale[0]), wpo)
    pool_s = jnp.swapaxes(hist_new, 0, 1)[None]
    xs = _ffn_sample(xs, ffn2[0])
    kv_new = jnp.swapaxes(_proj_sample(xs, vec(kv_norm), wkv, tab_s, n_heads), 0, 1)
    k_new, v_new = kv_new[:, :n_heads], kv_new[:, n_heads:]
    xs = _ffn_sample(xs, ffn1[1])
    q_new = jnp.swapaxes(_proj_sample(xs, vec(mix_norm[1]), wq, tab_s, N_DIL * n_heads), 0, 1)
    att = _attn_sample(q_new.reshape(bs, N_DIL, n_heads, HEAD_DIM), k_new, v_new, cache_k, cache_v)
    xs = _dense_residual(att.reshape(bs, d), xs, wo)
    y_sample = _ffn_sample(xs, ffn2[1], vec(final_norm)).reshape(bs, 1, d)

    x = x_prompt.reshape(b * s, d)
    x, xn, pool_hist, k_s = _layer0_head(x, s, ffn1[0], vec(mix_norm[0]), wg, vec(pool_scale[0]), wpo,
                                         ffn2[0][0], cache_k, k_new[:, None])
    pool_p = pool_hist[:, HALO - POOL_BUF:][None]
    x, xn, k_nat, v_nat, k_str, v_str, k_keep, v_keep, v_s = _layer0_tail(
        x, xn, s, keep, ffn2[0], vec(kv_norm), wkv, ffn1[1][0], tab_p, cache_v, v_new[:, None])
    x, *qs_by_group = _layer1_head(x, xn, s, ffn1[1], vec(mix_norm[1]), wq, tab_p)
    outs, lses = [], []
    for (win, dil), q in zip(DILATED_GROUPS, qs_by_group):
        assert win // dil == ATTN_BLOCK
        if dil == 1:
            nat = lambda a: a.reshape(b, 1, s, d)
            o, lse = _attn_prompt(nat(q), nat(k_nat), nat(v_nat), 1)
            outs.append(o.reshape(b * s, d))
            lses.append(lse.reshape(b * s, LANES))
        else:
            o, lse = _attn_prompt(q, k_str, v_str, dil)
            outs.append(o)
            lses.append(lse)
    y_prompt = _layer1_tail(outs, lses, x, s, wo, ffn2[1], vec(final_norm)).reshape(b, s, d)
    k_p = k_keep.reshape(b, keep, n_heads, HEAD_DIM)
    v_p = v_keep.reshape(b, keep, n_heads, HEAD_DIM)

    return (y_prompt, y_sample, pool_p, pool_s, k_p, v_p, k_s, v_s)
```

```python
import functools
from typing import NamedTuple

import jax
import jax.numpy as jnp
from jax import lax
from jax.experimental import pallas as pl
from jax.experimental.pallas import tpu as pltpu

F32 = jnp.float32
BF16 = jnp.bfloat16

POOL_WINDOWS = (2, 4, 8, 16)
POOL_BUF = max(POOL_WINDOWS) - 1
HEAD_DIM = 128
DILATED_GROUPS = ((128, 1), (512, 4), (2048, 16))
N_DIL = len(DILATED_GROUPS)
WINDOW_MAX = max(w for w, _ in DILATED_GROUPS)
STRIDE = max(d for _, d in DILATED_GROUPS)
ATTN_BLOCK = 128
ROT_DIM = HEAD_DIM // 4
ROPE_THETA = 500000.0
RMS_EPS = 1e-6
PAST_LEN = 16384
LOG2_E = 1.4426950408889634

V7X_VMEM_BYTES = 64 * 1024 * 1024
VMEM_LIMIT = (V7X_VMEM_BYTES * 29) // 32
LANES = 128
MXU_WIDTH = 256
HALO = 16

ROW_TILE = 256
BIG_TILE = 512
ATTN_QUERIES = 2048
SOFTMAX_ROWS = 128


def _const_spec(shape):
    n = len(shape)
    return pl.BlockSpec(shape, lambda *_: (0,) * n, pipeline_mode=pl.Buffered(1))


def _full_spec(shape):
    return pl.BlockSpec(shape, lambda i: (0,) * len(shape))


class _Ffn(NamedTuple):
    gain: jax.Array
    w_in: jax.Array
    w_out: jax.Array
    layer: int

    @property
    def d_ff(self):
        return self.w_out.shape[1]

    def weight_specs(self):
        def spec(w):
            return pl.BlockSpec((None,) + w.shape[1:], lambda *_: (self.layer, 0, 0),
                                pipeline_mode=pl.Buffered(1))
        return [spec(self.w_in), spec(self.w_out)]


def _params(n_axes):
    return pltpu.CompilerParams(dimension_semantics=("arbitrary",) * n_axes,
                                vmem_limit_bytes=VMEM_LIMIT)


def _rms(x, g):
    return x * lax.rsqrt(jnp.mean(x * x, axis=-1, keepdims=True) + RMS_EPS) * g


def _col_chunks(width):
    return [(c, c + MXU_WIDTH) for c in range(0, width, MXU_WIDTH)]


def _normalise_into(dst_ref, src_ref, g_ref):
    dst_ref[...] = _rms(src_ref[...], g_ref[...]).astype(dst_ref.dtype)


def _swiglu_half(src_ref, xn_ref, win_ref, wout_ref, h_ref, emit, side_jobs=()):
    d_ff, d = wout_ref.shape
    ff_chunks, out_chunks = _col_chunks(d_ff), _col_chunks(d)
    n_seg = len(ff_chunks) + len(out_chunks)
    jobs = list(side_jobs)
    done = [0, 0]

    def end_segment():
        done[0] += 1
        while done[1] < len(jobs) and done[1] * n_seg < done[0] * len(jobs):
            jobs[done[1]]()
            done[1] += 1

    for c0, c1 in ff_chunks:
        gate = jnp.dot(xn_ref[...], win_ref[:, c0:c1], preferred_element_type=F32)
        up = jnp.dot(xn_ref[...], win_ref[:, d_ff + c0:d_ff + c1], preferred_element_type=F32)
        h_ref[:, c0:c1] = (gate * jax.nn.sigmoid(gate) * up).astype(BF16)
        end_segment()
    for c0, c1 in out_chunks:
        emit(c0, c1, src_ref[:, c0:c1] + 0.5 * jnp.dot(h_ref[...], wout_ref[:, c0:c1],
                                                        preferred_element_type=F32))
        end_segment()


def _pool_trailing(ext_ref, row0, rows, j, gi, w, gd):
    c0, c1 = gi * gd, (gi + 1) * gd
    cur = ext_ref[row0:row0 + rows, c0:c1]
    tot = cur
    for s in range(1, w):
        tot = tot + ext_ref[row0 - s:row0 - s + rows, c0:c1]
    if j is None:
        return tot * (1.0 / w) - cur
    pos1 = (j * rows + 1 + lax.broadcasted_iota(jnp.int32, (rows, 1), 0)).astype(F32)
    return tot * (1.0 / jnp.minimum(pos1, float(w))) - cur


def _pool_jobs(src_ref, j, g_ref, wg_ref, scale_ref, wo_ref, ext_ref, z_ref, o_ref, state_ref, g_next_ref,
               xn_next_ref):
    tm, d = src_ref.shape
    gd = d // len(POOL_WINDOWS)

    def normalise():
        h = _rms(src_ref[...], g_ref[...])
        ext_ref[0:HALO, :] = jnp.where(j == 0, 0.0, ext_ref[tm:tm + HALO, :])
        ext_ref[HALO:HALO + tm, :] = h
        state_ref[...] = h[tm - HALO:, :]

    def group(gi, w):
        p = _pool_trailing(ext_ref, HALO, tm, j, gi, w, gd).astype(BF16)
        z = jnp.dot(p, wg_ref[gi], preferred_element_type=F32) * scale_ref[:, gi * gd:(gi + 1) * gd]
        z_ref[:, gi * gd:(gi + 1) * gd] = z.astype(BF16)

    def project(c0, c1):
        o_ref[:, c0:c1] = src_ref[:, c0:c1] + jnp.dot(z_ref[...], wo_ref[:, c0:c1],
                                                      preferred_element_type=F32)

    return ([normalise] + [functools.partial(group, gi, w) for gi, w in enumerate(POOL_WINDOWS)]
            + [functools.partial(project, c0, c1) for c0, c1 in _col_chunks(d)]
            + [functools.partial(_normalise_into, xn_next_ref, o_ref, g_next_ref)])


def _rope_tables(pos):
    half = ROT_DIM // 2
    inv_freq = ROPE_THETA ** (-jnp.arange(0, ROT_DIM, 2, dtype=F32) / ROT_DIM)
    ang = pos.astype(F32)[:, None] * inv_freq[None, :]
    cos, sin = jnp.cos(ang), jnp.sin(ang)
    t = pos.shape[0]
    c = jnp.concatenate([cos, cos, jnp.ones((t, HEAD_DIM - ROT_DIM), F32)], axis=1)
    a = jnp.concatenate([-sin, jnp.zeros((t, HEAD_DIM - half), F32)], axis=1)
    b = jnp.concatenate([jnp.zeros((t, half), F32), sin, jnp.zeros((t, HEAD_DIM - ROT_DIM), F32)], axis=1)
    return c, a, b


def _projection_jobs(xn_ref, w_ref, col0, table_refs, y_ref, store_head):
    n_slots = y_ref.shape[0]
    half = ROT_DIM // 2
    per_dot = MXU_WIDTH // HEAD_DIM

    def project(h0):
        cols = slice(col0 + h0 * HEAD_DIM, col0 + (h0 + per_dot) * HEAD_DIM)
        y = jnp.dot(xn_ref[...], w_ref[:, cols], preferred_element_type=F32)
        for k in range(per_dot):
            y_ref[h0 + k] = y[:, k * HEAD_DIM:(k + 1) * HEAD_DIM]

    def finish(hh):
        if table_refs is not None:
            c, a, b = (t[...] for t in table_refs)
            yh = y_ref[hh]
            y_ref[hh] = yh * c + pltpu.roll(yh, HEAD_DIM - half, 1) * a + pltpu.roll(yh, half, 1) * b
        store_head(hh)

    jobs = []
    for h0 in range(0, n_slots, per_dot):
        jobs.append(functools.partial(project, h0))
        jobs += [functools.partial(finish, h0 + k) for k in range(per_dot)]
    return jobs


def _store_head_natural(src_ref, hh, dst_ref):
    dst_ref[:, hh * HEAD_DIM:(hh + 1) * HEAD_DIM] = src_ref[hh].astype(dst_ref.dtype)


def _store_head_strided(src_ref, hh, dst_ref):
    per = src_ref.shape[1] // STRIDE
    for r in range(STRIDE):
        dst_ref[r, :, hh * HEAD_DIM:(hh + 1) * HEAD_DIM] = (
            src_ref[hh, pl.ds(r, per, stride=STRIDE), :].astype(dst_ref.dtype))


def _roll_tile(seq_end, src_ref, nxt_ref, new_ref, dst_ref):
    t = src_ref.shape[0]
    dst_ref[0:t - 1] = src_ref[1:t]
    dst_ref[t - 1] = jnp.where(seq_end, new_ref[0], nxt_ref[0])


def _combine_jobs(o_refs, lse_refs, x_ref, wo_ref, o_nat, lse_nat, w_ref, att_ref, dst_ref, g_next_ref,
                  xn_next_ref):
    tm, d = x_ref.shape
    per = tm // STRIDE
    n_heads = d // HEAD_DIM
    heads, lses = [], []
    jobs = []
    slot = 0
    for (_, dil), o_ref, lse_ref in zip(DILATED_GROUPS, o_refs, lse_refs):
        if dil == 1:
            heads.append(functools.partial(
                lambda hh, ref: ref[:, hh * HEAD_DIM:(hh + 1) * HEAD_DIM].astype(F32), ref=o_ref))
            lses.append(functools.partial(lambda ref: ref[...], lse_ref))
            continue

        def to_natural(r, o_ref=o_ref, lse_ref=lse_ref, slot=slot):
            lse_nat[slot, pl.ds(r, per, stride=STRIDE), :] = lse_ref[r]
            for hh in range(n_heads):
                o_nat[slot * n_heads + hh, pl.ds(r, per, stride=STRIDE), :] = (
                    o_ref[r, :, hh * HEAD_DIM:(hh + 1) * HEAD_DIM].astype(F32))

        jobs += [functools.partial(to_natural, r) for r in range(STRIDE)]
        heads.append(functools.partial(lambda hh, s0: o_nat[s0 + hh], s0=slot * n_heads))
        lses.append(functools.partial(lambda s: lse_nat[s], slot))
        slot += 1

    def weights():
        ls = [l() for l in lses]
        top = functools.reduce(jnp.maximum, ls)
        es = [jnp.exp(l - top) for l in ls]
        inv = 1.0 / functools.reduce(jnp.add, es)
        for g, e in enumerate(es):
            w_ref[g] = e * inv

    def mix_head(hh):
        att = functools.reduce(jnp.add, [w_ref[g][:, hh:hh + 1] * heads[g](hh) for g in range(N_DIL)])
        att_ref[:, hh * HEAD_DIM:(hh + 1) * HEAD_DIM] = att.astype(BF16)

    def project(c0, c1):
        dst_ref[:, c0:c1] = x_ref[:, c0:c1] + jnp.dot(att_ref[...], wo_ref[:, c0:c1],
                                                      preferred_element_type=F32)

    return (jobs + [weights] + [functools.partial(mix_head, hh) for hh in range(n_heads)]
            + [functools.partial(project, c0, c1) for c0, c1 in _col_chunks(d)]
            + [functools.partial(_normalise_into, xn_next_ref, dst_ref, g_next_ref)])


class _Tiling:
    def __init__(self, n_rows, seq, tm):
        self.tm = tm
        self.n_tiles = n_rows // tm
        self.per_seq = seq // tm
        self.steps = self.n_tiles + 1

    def lead(self, i):
        return jnp.minimum(i, self.n_tiles - 1)

    def trail(self, i):
        return jnp.maximum(i - 1, 0)

    def row(self, width, tile_of):
        return pl.BlockSpec((self.tm, width), lambda i: (tile_of(i), 0))

    def strided(self, width, tile_of):
        return pl.BlockSpec((None, STRIDE, self.tm // STRIDE, width),
                            lambda i: (tile_of(i) // self.per_seq, 0, tile_of(i) % self.per_seq, 0))

    def table(self, tile_of):
        return pl.BlockSpec((self.tm, HEAD_DIM), lambda i: (tile_of(i) % self.per_seq, 0))


def _roll_operands(cache, new, tl):
    b, w, h, e = cache.shape
    per = tl.n_tiles // b
    t = w // per
    blk = pl.BlockSpec((None, t, h, e), lambda i: (tl.lead(i) // per, tl.lead(i) % per, 0, 0))
    nxt = pl.BlockSpec((None, 1, h, e),
                       lambda i: (tl.lead(i) // per, jnp.minimum((tl.lead(i) % per + 1) * t, w - 1), 0, 0))
    cur_new = pl.BlockSpec((None, 1, h, e), lambda i: (tl.lead(i) // per, 0, 0, 0))
    return per, [blk, nxt, cur_new], [cache, cache, new], blk


def _zero_on_first_step(i, *refs):
    @pl.when(i == 0)
    def _():
        for ref in refs:
            ref[...] = jnp.zeros(ref.shape, ref.dtype)


def _run_all(jobs):
    for job in jobs:
        job()


def _with_job_in_middle(jobs, job):
    jobs = list(jobs)
    jobs.insert(len(jobs) // 2, job)
    return jobs


def _run_stages(both_have_a_tile, both, alone):
    pl.when(both_have_a_tile)(both)
    pl.when(jnp.logical_not(both_have_a_tile))(alone)


def _emit_to(*refs):
    def emit(c0, c1, value):
        for ref in refs:
            ref[:, c0:c1] = value
    return emit


def _layer0_head_kernel(x_ref, g1_ref, win_ref, wout_ref, gm_ref, wg_ref, scale_ref, wo_ref, gn_ref,
                        src_ref, nxt_ref, new_ref, o_ref, on_ref, state_ref, dst_ref,
                        h_ref, xn_ref, xs_ref, ext_ref, z_ref, *, tl, roll_per_seq):
    i = pl.program_id(0)
    _zero_on_first_step(i, xs_ref, ext_ref)
    trailing = _pool_jobs(xs_ref.at[(i + 1) % 2], tl.trail(i) % tl.per_seq, gm_ref, wg_ref, scale_ref, wo_ref,
                          ext_ref, z_ref, o_ref, state_ref, gn_ref, on_ref)
    roll = functools.partial(
        _roll_tile, tl.lead(i) % roll_per_seq == roll_per_seq - 1, src_ref, nxt_ref, new_ref, dst_ref)

    def both():
        _normalise_into(xn_ref, x_ref, g1_ref)
        _swiglu_half(x_ref, xn_ref, win_ref, wout_ref, h_ref, _emit_to(xs_ref.at[i % 2]),
                     _with_job_in_middle(trailing, roll))

    _run_stages(i < tl.n_tiles, both, functools.partial(_run_all, trailing))


def _layer0_head(x, seq, ffn, g_mix, w_group, scale, w_o, g_next, cache, new):
    n, d = x.shape
    tl = _Tiling(n, seq, BIG_TILE)
    g1, w_in, w_out, _ = ffn
    roll_per_seq, roll_specs, roll_args, roll_out = _roll_operands(cache, new, tl)
    return pl.pallas_call(
        functools.partial(_layer0_head_kernel, tl=tl, roll_per_seq=roll_per_seq), grid=(tl.steps,),
        in_specs=[tl.row(d, tl.lead), _const_spec((1, d)), *ffn.weight_specs(),
                  _const_spec((1, d)), _const_spec(w_group.shape), _const_spec((1, d)), _const_spec(w_o.shape),
                  _const_spec((1, d))] + roll_specs,
        out_specs=[tl.row(d, tl.trail), tl.row(d, tl.trail),
                   pl.BlockSpec((None, HALO, d), lambda i: (tl.trail(i) // tl.per_seq, 0, 0)), roll_out],
        out_shape=[jax.ShapeDtypeStruct((n, d), F32), jax.ShapeDtypeStruct((n, d), BF16),
                   jax.ShapeDtypeStruct((n // seq, HALO, d), F32),
                   jax.ShapeDtypeStruct(cache.shape, cache.dtype)],
        scratch_shapes=[pltpu.VMEM((tl.tm, ffn.d_ff), BF16), pltpu.VMEM((tl.tm, d), BF16),
                        pltpu.VMEM((2, tl.tm, d), F32), pltpu.VMEM((tl.tm + HALO, d), F32),
                        pltpu.VMEM((tl.tm, d), BF16)],
        compiler_params=_params(1), name="layer0_head")(
            x, g1, w_in, w_out, g_mix, w_group, scale, w_o, g_next, *roll_args)


def _layer0_tail_kernel(x_ref, xn_ref, win_ref, wout_ref, gkv_ref, wkv_ref, gn_ref, c_ref, a_ref, b_ref,
                        src_ref, nxt_ref, new_ref,
                        o_ref, on_ref, k_ref, v_ref, ks_ref, vs_ref, kf_ref, vf_ref, dst_ref,
                        h_ref, xs_ref, xkv_ref, yk_ref, yv_ref, *, tl, roll_per_seq, kept_tiles):
    i = pl.program_id(0)
    _zero_on_first_step(i, xs_ref)
    d = x_ref.shape[1]

    def normalise():
        x = xs_ref[(i + 1) % 2]
        y = x * lax.rsqrt(jnp.mean(x * x, axis=-1, keepdims=True) + RMS_EPS)
        xkv_ref[...] = (y * gkv_ref[...]).astype(BF16)
        on_ref[...] = (y * gn_ref[...]).astype(BF16)

    def store_both(y_ref, nat_ref, str_ref, hh):
        _store_head_natural(y_ref, hh, nat_ref)
        _store_head_strided(y_ref, hh, str_ref)

    trailing = [normalise]
    trailing += _projection_jobs(xkv_ref, wkv_ref, 0, (c_ref, a_ref, b_ref), yk_ref,
                                 functools.partial(store_both, yk_ref, k_ref, ks_ref))
    trailing += _projection_jobs(xkv_ref, wkv_ref, d, None, yv_ref,
                                 functools.partial(store_both, yv_ref, v_ref, vs_ref))
    roll = functools.partial(
        _roll_tile, tl.lead(i) % roll_per_seq == roll_per_seq - 1, src_ref, nxt_ref, new_ref, dst_ref)

    def both():
        _swiglu_half(x_ref, xn_ref, win_ref, wout_ref, h_ref, _emit_to(o_ref, xs_ref.at[i % 2]),
                     _with_job_in_middle(trailing, roll))

    _run_stages(i < tl.n_tiles, both, functools.partial(_run_all, trailing))

    @pl.when(tl.trail(i) % tl.per_seq >= tl.per_seq - kept_tiles)
    def _():
        for hh in range(yk_ref.shape[0]):
            kf_ref[:, hh, :] = yk_ref[hh]
            vf_ref[:, hh, :] = yv_ref[hh]


def _layer0_tail(x, xn, seq, keep, ffn, g_kv, w_kv, g_next, tables, cache, new):
    n, d = x.shape
    b = n // seq
    n_heads = d // HEAD_DIM
    tl = _Tiling(n, seq, ROW_TILE)
    _, w_in, w_out, _ = ffn
    kept = keep // tl.tm
    roll_per_seq, roll_specs, roll_args, roll_out = _roll_operands(cache, new, tl)

    def kept_block(i):
        t = tl.trail(i)
        return ((t // tl.per_seq) * kept + jnp.maximum(t % tl.per_seq - (tl.per_seq - kept), 0), 0, 0)

    kept_spec = pl.BlockSpec((tl.tm, n_heads, HEAD_DIM), kept_block)
    nat_shape = jax.ShapeDtypeStruct((n, d), BF16)
    str_shape = jax.ShapeDtypeStruct((b, STRIDE, seq // STRIDE, d), BF16)
    kept_shape = jax.ShapeDtypeStruct((b * keep, n_heads, HEAD_DIM), F32)
    head_scratch = pltpu.VMEM((n_heads, tl.tm, HEAD_DIM), F32)
    return pl.pallas_call(
        functools.partial(_layer0_tail_kernel, tl=tl, roll_per_seq=roll_per_seq, kept_tiles=kept),
        grid=(tl.steps,),
        in_specs=[tl.row(d, tl.lead), tl.row(d, tl.lead), *ffn.weight_specs(),
                  _const_spec((1, d)), _const_spec(w_kv.shape), _const_spec((1, d))]
        + [tl.table(tl.trail)] * 3 + roll_specs,
        out_specs=[tl.row(d, tl.lead), tl.row(d, tl.trail), tl.row(d, tl.trail), tl.row(d, tl.trail),
                   tl.strided(d, tl.trail), tl.strided(d, tl.trail), kept_spec, kept_spec, roll_out],
        out_shape=[jax.ShapeDtypeStruct((n, d), F32), nat_shape, nat_shape, nat_shape, str_shape, str_shape,
                   kept_shape, kept_shape, jax.ShapeDtypeStruct(cache.shape, cache.dtype)],
        scratch_shapes=[pltpu.VMEM((tl.tm, ffn.d_ff), BF16), pltpu.VMEM((2, tl.tm, d), F32),
                        pltpu.VMEM((tl.tm, d), BF16), head_scratch, head_scratch],
        compiler_params=_params(1), name="layer0_tail")(
            x, xn, w_in, w_out, g_kv, w_kv, g_next, *tables, *roll_args)


def _layer1_head_kernel(x_ref, xn_ref, win_ref, wout_ref, gm_ref, wq_ref, c_ref, a_ref, b_ref, o_ref, *refs, tl):
    q_refs, (h_ref, xs_ref, xq_ref, y_ref) = refs[:N_DIL], refs[N_DIL:]
    i = pl.program_id(0)
    _zero_on_first_step(i, xs_ref)
    d = x_ref.shape[1]
    n_heads = d // HEAD_DIM

    def normalise():
        xq_ref[...] = _rms(xs_ref[(i + 1) % 2], gm_ref[...]).astype(BF16)

    trailing = [normalise]
    for g, ((_, dil), q_ref) in enumerate(zip(DILATED_GROUPS, q_refs)):
        yg_ref = y_ref.at[g * n_heads:(g + 1) * n_heads]
        store = _store_head_natural if dil == 1 else _store_head_strided
        trailing += _projection_jobs(xq_ref, wq_ref, g * d, (c_ref, a_ref, b_ref), yg_ref,
                                     functools.partial(lambda hh, st, src, dst: st(src, hh, dst),
                                                       st=store, src=yg_ref, dst=q_ref))

    def both():
        _swiglu_half(x_ref, xn_ref, win_ref, wout_ref, h_ref, _emit_to(o_ref, xs_ref.at[i % 2]), trailing)

    _run_stages(i < tl.n_tiles, both, functools.partial(_run_all, trailing))


def _layer1_head(x, xn, seq, ffn, g_mix, w_q, tables):
    n, d = x.shape
    b = n // seq
    tl = _Tiling(n, seq, BIG_TILE)
    _, w_in, w_out, _ = ffn
    q_specs, q_shapes = [], []
    for _, dil in DILATED_GROUPS:
        q_specs.append(tl.row(d, tl.trail) if dil == 1 else tl.strided(d, tl.trail))
        q_shapes.append(jax.ShapeDtypeStruct((n, d) if dil == 1 else (b, STRIDE, seq // STRIDE, d), BF16))
    return pl.pallas_call(
        functools.partial(_layer1_head_kernel, tl=tl), grid=(tl.steps,),
        in_specs=[tl.row(d, tl.lead), tl.row(d, tl.lead), *ffn.weight_specs(),
                  _const_spec((1, d)), _const_spec(w_q.shape)] + [tl.table(tl.trail)] * 3,
        out_specs=[tl.row(d, tl.lead)] + q_specs,
        out_shape=[jax.ShapeDtypeStruct((n, d), F32)] + q_shapes,
        scratch_shapes=[pltpu.VMEM((tl.tm, ffn.d_ff), BF16), pltpu.VMEM((2, tl.tm, d), F32),
                        pltpu.VMEM((tl.tm, d), BF16),
                        pltpu.VMEM((w_q.shape[1] // HEAD_DIM, tl.tm, HEAD_DIM), F32)],
        compiler_params=_params(1), name="layer1_head")(x, xn, w_in, w_out, g_mix, w_q, *tables)


def _layer1_tail_kernel(*refs):
    o_refs, lse_refs = refs[:N_DIL], refs[N_DIL:2 * N_DIL]
    (x_ref, wo_ref, g2_ref, win_ref, wout_ref, gf_ref, out_ref,
     h_ref, xs_ref, xns_ref, o_nat, lse_nat, w_ref, att_ref) = refs[2 * N_DIL:]
    i = pl.program_id(0)
    leading = _combine_jobs(o_refs, lse_refs, x_ref, wo_ref, o_nat, lse_nat, w_ref, att_ref,
                            xs_ref.at[i % 2], g2_ref, xns_ref.at[i % 2])

    def both():
        _swiglu_half(xs_ref.at[(i + 1) % 2], xns_ref.at[(i + 1) % 2], win_ref, wout_ref, h_ref,
                     _emit_to(out_ref), leading)
        out_ref[...] = _rms(out_ref[...], gf_ref[...])

    _run_stages(i > 0, both, functools.partial(_run_all, leading))


def _layer1_tail(outs, lses, x, seq, w_o, ffn, g_final):
    n, d = x.shape
    tl = _Tiling(n, seq, BIG_TILE)
    g2, w_in, w_out, _ = ffn

    def group_specs(width):
        return [tl.row(width, tl.lead) if dil == 1 else tl.strided(width, tl.lead) for _, dil in DILATED_GROUPS]

    n_strided = sum(dil > 1 for _, dil in DILATED_GROUPS)
    return pl.pallas_call(
        _layer1_tail_kernel, grid=(tl.steps,),
        in_specs=group_specs(d) + group_specs(LANES)
        + [tl.row(d, tl.lead), _const_spec(w_o.shape), _const_spec((1, d)), *ffn.weight_specs(),
           _const_spec((1, d))],
        out_specs=tl.row(d, tl.trail), out_shape=jax.ShapeDtypeStruct((n, d), F32),
        scratch_shapes=[pltpu.VMEM((tl.tm, ffn.d_ff), BF16), pltpu.VMEM((2, tl.tm, d), F32),
                        pltpu.VMEM((2, tl.tm, d), BF16),
                        pltpu.VMEM((n_strided * (d // HEAD_DIM), tl.tm, HEAD_DIM), F32),
                        pltpu.VMEM((n_strided, tl.tm, LANES), F32), pltpu.VMEM((N_DIL, tl.tm, LANES), F32),
                        pltpu.VMEM((tl.tm, d), BF16)],
        compiler_params=_params(1), name="layer1_tail")(*outs, *lses, x, w_o, g2, w_in, w_out, g_final)


def _attn_prompt_kernel(q_ref, kc_ref, kp_ref, vc_ref, vp_ref, o_ref, lse_ref, kk_ref, vv_ref):
    n = pl.program_id(2)
    nc, rows, d = q_ref.shape
    blk = ATTN_BLOCK
    sb = blk // nc
    n_heads = d // HEAD_DIM
    for c in range(nc):
        kk_ref[c, 0:sb, :] = kp_ref[c]
        kk_ref[c, sb:, :] = kc_ref[c]
        vv_ref[c, 0:sb, :] = vp_ref[c]
        vv_ref[c, sb:, :] = vc_ref[c]
    lane = lax.broadcasted_iota(jnp.int32, (SOFTMAX_ROWS, LANES), 1)
    scale = HEAD_DIM ** -0.5

    def keys(ref, u, sl):
        parts = [ref[c, u * sb:(u + 2) * sb, sl] for c in range(nc)]
        return parts[0] if nc == 1 else jnp.concatenate(parts, axis=0)

    def slab_ranges(r0, r1):
        spans = [(c, max(c * sb, r0), min((c + 1) * sb, r1)) for c in range(nc)]
        return [(c, lo - c * sb, lo, hi) for c, lo, hi in spans if lo < hi]

    for r0 in range(0, blk, SOFTMAX_ROWS):
        r1 = r0 + SOFTMAX_ROWS
        qrow = r0 + lax.broadcasted_iota(jnp.int32, (SOFTMAX_ROWS, 2 * blk), 0)
        kcol = lax.broadcasted_iota(jnp.int32, (SOFTMAX_ROWS, 2 * blk), 1)
        dist = nc * (qrow % sb - kcol % (2 * sb) + sb) + (qrow // sb - kcol // (2 * sb))
        band = (dist >= 0) & (dist <= blk)
        for u in range(rows // sb):
            valid = band
            if u == 0:
                valid = band & ((kcol % (2 * sb) >= sb) | (n > 0))
            lse_tile = jnp.zeros((SOFTMAX_ROWS, LANES), F32)
            for hh in range(n_heads):
                sl = slice(hh * HEAD_DIM, (hh + 1) * HEAD_DIM)
                q_parts = [q_ref[c, u * sb + off:u * sb + off + hi - lo, sl]
                           for c, off, lo, hi in slab_ranges(r0, r1)]
                qh = q_parts[0] if len(q_parts) == 1 else jnp.concatenate(q_parts, axis=0)
                s = lax.dot_general(qh, keys(kk_ref, u, sl), (((1,), (1,)), ((), ())),
                                    preferred_element_type=F32)
                s = jnp.where(valid, s, -jnp.inf)
                m = jnp.max(s, axis=1, keepdims=True)
                p = jnp.exp2((s - m) * (scale * LOG2_E))
                den = jnp.sum(p, axis=1, keepdims=True)
                o = jnp.dot(p.astype(BF16), keys(vv_ref, u, sl), preferred_element_type=F32) / den
                for c, off, lo, hi in slab_ranges(r0, r1):
                    o_ref[c, u * sb + off:u * sb + off + hi - lo, sl] = o[lo - r0:hi - r0].astype(o_ref.dtype)
                lse_tile = jnp.where(lane == hh, m * scale + jnp.log(den), lse_tile)
            for c, off, lo, hi in slab_ranges(r0, r1):
                lse_ref[c, u * sb + off:u * sb + off + hi - lo, :] = lse_tile[lo - r0:hi - r0]


def _attn_prompt(q, k, v, dil):
    b, r_all, length, d = q.shape
    nc = r_all // dil
    sb = ATTN_BLOCK // nc
    rows = min(ATTN_QUERIES // nc, length)
    view = lambda a: a.reshape(b, nc, dil, length, a.shape[-1])
    cur = pl.BlockSpec((None, nc, None, rows, d), lambda i, r, n: (i, 0, r, n, 0))
    prev = pl.BlockSpec((None, nc, None, sb, d),
                        lambda i, r, n: (i, 0, r, jnp.maximum(n * (rows // sb) - 1, 0), 0))
    o, lse = pl.pallas_call(
        _attn_prompt_kernel, grid=(b, dil, length // rows),
        in_specs=[cur, cur, prev, cur, prev],
        out_specs=[cur, pl.BlockSpec((None, nc, None, rows, LANES), lambda i, r, n: (i, 0, r, n, 0))],
        out_shape=[jax.ShapeDtypeStruct((b, nc, dil, length, d), BF16),
                   jax.ShapeDtypeStruct((b, nc, dil, length, LANES), F32)],
        scratch_shapes=[pltpu.VMEM((nc, rows + sb, d), BF16), pltpu.VMEM((nc, rows + sb, d), BF16)],
        compiler_params=_params(3), name=f"attn_prompt_d{dil}")(view(q), view(k), view(k), view(v), view(v))
    return o.reshape(b, r_all, length, d), lse.reshape(b, r_all, length, LANES)


def _ffn_kernel(x_ref, g_ref, win_ref, wout_ref, o_ref, h_ref, xn_ref):
    _normalise_into(xn_ref, x_ref, g_ref)
    _swiglu_half(x_ref, xn_ref, win_ref, wout_ref, h_ref, _emit_to(o_ref))


def _ffn_final_kernel(x_ref, g_ref, win_ref, wout_ref, gf_ref, o_ref, h_ref, xn_ref):
    _ffn_kernel(x_ref, g_ref, win_ref, wout_ref, o_ref, h_ref, xn_ref)
    o_ref[...] = _rms(o_ref[...], gf_ref[...])


def _ffn_sample(x, ffn, final_g=None):
    n, d = x.shape
    g, w_in, w_out, _ = ffn
    in_specs = [_full_spec((n, d)), _full_spec((1, d)), *ffn.weight_specs()]
    args = [x, g, w_in, w_out]
    body = _ffn_kernel
    if final_g is not None:
        in_specs.append(_full_spec((1, d)))
        args.append(final_g)
        body = _ffn_final_kernel
    return pl.pallas_call(
        body, grid=(1,), in_specs=in_specs, out_specs=_full_spec((n, d)),
        out_shape=jax.ShapeDtypeStruct((n, d), F32),
        scratch_shapes=[pltpu.VMEM((n, ffn.d_ff), BF16), pltpu.VMEM((n, d), BF16)],
        compiler_params=_params(1), name="ffn_sample")(*args)


def _pool_sample_kernel(x_ref, st_ref, g_ref, wg_ref, scale_ref, wo_ref, o_ref, new_ref, z_ref):
    n, d = x_ref.shape
    gd = d // len(POOL_WINDOWS)
    x = x_ref[...]
    h = _rms(x, g_ref[...])
    for gi, w in enumerate(POOL_WINDOWS):
        c0, c1 = gi * gd, (gi + 1) * gd
        cur = h[:, c0:c1]
        tot = cur
        for s in range(1, w):
            tot = tot + st_ref[POOL_BUF - s, :, c0:c1]
        p = (tot * (1.0 / w) - cur).astype(BF16)
        z_ref[:, c0:c1] = (jnp.dot(p, wg_ref[gi], preferred_element_type=F32) * scale_ref[:, c0:c1]).astype(BF16)
    o_ref[...] = x + jnp.dot(z_ref[...], wo_ref[...], preferred_element_type=F32)
    for r in range(POOL_BUF - 1):
        new_ref[r] = st_ref[r + 1]
    new_ref[POOL_BUF - 1] = h


def _pool_sample(x, st, g, w_group, scale, w_o):
    n, d = x.shape
    return pl.pallas_call(
        _pool_sample_kernel, grid=(1,),
        in_specs=[_full_spec((n, d)), _full_spec(st.shape), _full_spec((1, d)), _full_spec(w_group.shape),
                  _full_spec((1, d)), _full_spec(w_o.shape)],
        out_specs=[_full_spec((n, d)), _full_spec(st.shape)],
        out_shape=[jax.ShapeDtypeStruct((n, d), F32), jax.ShapeDtypeStruct(st.shape, F32)],
        scratch_shapes=[pltpu.VMEM((n, d), BF16)],
        compiler_params=_params(1), name="pool_sample")(x, st, g, w_group, scale, w_o)


def _proj_sample_kernel(x_ref, g_ref, w_ref, c_ref, a_ref, b_ref, y_ref, xn_ref, *, n_rope_heads):
    xn_ref[...] = _rms(x_ref[...], g_ref[...]).astype(BF16)
    n_slots = y_ref.shape[0]
    stored_in_place = lambda hh: None
    jobs = _projection_jobs(xn_ref, w_ref, 0, (c_ref, a_ref, b_ref), y_ref.at[0:n_rope_heads], stored_in_place)
    if n_rope_heads < n_slots:
        jobs += _projection_jobs(xn_ref, w_ref, n_rope_heads * HEAD_DIM, None, y_ref.at[n_rope_heads:n_slots],
                                 stored_in_place)
    for job in jobs:
        job()


def _proj_sample(x, g, w, tables, n_rope_heads):
    n, d = x.shape
    n_slots = w.shape[1] // HEAD_DIM
    return pl.pallas_call(
        functools.partial(_proj_sample_kernel, n_rope_heads=n_rope_heads), grid=(1,),
        in_specs=[_full_spec((n, d)), _full_spec((1, d)), _full_spec(w.shape)]
        + [_full_spec((n, HEAD_DIM))] * 3,
        out_specs=_full_spec((n_slots, n, HEAD_DIM)),
        out_shape=jax.ShapeDtypeStruct((n_slots, n, HEAD_DIM), F32),
        scratch_shapes=[pltpu.VMEM((n, d), BF16)],
        compiler_params=_params(1), name="proj_sample")(x, g, w, *tables)


def _attn_sample_kernel(*refs):
    q_ref, kn_ref, vn_ref = refs[:3]
    kc_refs = refs[3:3 + N_DIL]
    vc_refs = refs[3 + N_DIL:3 + 2 * N_DIL]
    att_ref = refs[3 + 2 * N_DIL]
    scale = HEAD_DIM ** -0.5
    kn, vn = kn_ref[...], vn_ref[...]
    outs, lses = [], []
    for g in range(N_DIL):
        qg = q_ref[g]
        s_past = jnp.sum(kc_refs[g][...] * qg[None], axis=2, keepdims=True) * scale
        s_new = jnp.sum(kn * qg, axis=1, keepdims=True) * scale
        m = jnp.maximum(jnp.max(s_past, axis=0), s_new)
        p_past = jnp.exp(s_past - m[None])
        p_new = jnp.exp(s_new - m)
        den = jnp.sum(p_past, axis=0) + p_new
        o = jnp.sum(p_past * vc_refs[g][...], axis=0) + p_new * vn
        outs.append(o / den)
        lses.append(m + jnp.log(den))
    top = functools.reduce(jnp.maximum, lses)
    es = [jnp.exp(l - top) for l in lses]
    inv = 1.0 / functools.reduce(jnp.add, es)
    att_ref[...] = functools.reduce(jnp.add, [e * inv * o for e, o in zip(es, outs)])


def _attn_sample(q, k_new, v_new, cache_k, cache_v):
    b, w, h, e = cache_k.shape
    cache_args, cache_specs = [], []
    for cache in (cache_k, cache_v):
        for win, dil in DILATED_GROUPS:
            n_keys = win // dil
            last = w // dil // n_keys - 1
            cache_args.append(cache.reshape(b, w // dil, dil, h, e))
            cache_specs.append(pl.BlockSpec((None, n_keys, None, h, e),
                                            lambda i, last=last: (i, last, 0, 0, 0)))
    head = pl.BlockSpec((None, h, e), lambda i: (i, 0, 0))
    return pl.pallas_call(
        _attn_sample_kernel, grid=(b,),
        in_specs=[pl.BlockSpec((None, N_DIL, h, e), lambda i: (i, 0, 0, 0)), head, head] + cache_specs,
        out_specs=head, out_shape=jax.ShapeDtypeStruct((b, h, e), F32),
        compiler_params=_params(1), name="attn_sample")(q, k_new, v_new, *cache_args)


def _dense_residual_kernel(a_ref, x_ref, w_ref, o_ref):
    o_ref[...] = x_ref[...] + jnp.dot(a_ref[...].astype(BF16), w_ref[...], preferred_element_type=F32)


def _dense_residual(a, x, w):
    n, d = x.shape
    return pl.pallas_call(
        _dense_residual_kernel, grid=(1,),
        in_specs=[_full_spec(a.shape), _full_spec((n, d)), _full_spec(w.shape)], out_specs=_full_spec((n, d)),
        out_shape=jax.ShapeDtypeStruct((n, d), F32),
        compiler_params=_params(1), name="dense_residual")(a, x, w)


def kernel(x_prompt, x_sample, state_pool, cache_k, cache_v, ffn1_norm, ffn1_w_in, ffn1_w_out, mix_norm,
           ffn2_norm, ffn2_w_in, ffn2_w_out, pool_w_group, pool_scale, pool_w_o, kv_norm, w_kv, attn_w_q,
           attn_w_o, final_norm):
    b, s, d = x_prompt.shape
    bs = x_sample.shape[0]
    assert x_sample.shape[1] == 1
    n_heads = d // HEAD_DIM
    w_cache = cache_k.shape[1]
    assert w_cache == min(WINDOW_MAX, PAST_LEN)
    keep = min(WINDOW_MAX, s)

    bf = lambda a: a.astype(BF16)
    vec = lambda a: a.reshape(1, d)
    n_layers = ffn1_norm.shape[0]
    f1_in, f1_out, f2_in, f2_out = bf(ffn1_w_in), bf(ffn1_w_out), bf(ffn2_w_in), bf(ffn2_w_out)
    ffn1 = [_Ffn(vec(ffn1_norm[l]), f1_in, f1_out, l) for l in range(n_layers)]
    ffn2 = [_Ffn(vec(ffn2_norm[l]), f2_in, f2_out, l) for l in range(n_layers)]
    wg, wpo, wkv, wq, wo = bf(pool_w_group[0]), bf(pool_w_o[0]), bf(w_kv), bf(attn_w_q[0]), bf(attn_w_o[0])

    tab_p = _rope_tables(jnp.arange(s, dtype=jnp.int32))
    tab_s = _rope_tables(jnp.full((bs,), PAST_LEN, dtype=jnp.int32))

    xs = _ffn_sample(x_sample.reshape(bs, d), ffn1[0])
    hist = jnp.swapaxes(state_pool[0], 0, 1)
    xs, hist_new = _pool_sample(xs, hist, vec(mix_norm[0]), wg, vec(pool_scale[0]), wpo)
    pool_s = jnp.swapaxes(hist_new, 0, 1)[None]
    xs = _ffn_sample(xs, ffn2[0])
    kv_new = jnp.swapaxes(_proj_sample(xs, vec(kv_norm), wkv, tab_s, n_heads), 0, 1)
    k_new, v_new = kv_new[:, :n_heads], kv_new[:, n_heads:]
    xs = _ffn_sample(xs, ffn1[1])
    q_new = jnp.swapaxes(_proj_sample(xs, vec(mix_norm[1]), wq, tab_s, N_DIL * n_heads), 0, 1)
    att = _attn_sample(q_new.reshape(bs, N_DIL, n_heads, HEAD_DIM), k_new, v_new, cache_k, cache_v)
    xs = _dense_residual(att.reshape(bs, d), xs, wo)
    y_sample = _ffn_sample(xs, ffn2[1], vec(final_norm)).reshape(bs, 1, d)

    x = x_prompt.reshape(b * s, d)
    x, xn, pool_hist, k_s = _layer0_head(x, s, ffn1[0], vec(mix_norm[0]), wg, vec(pool_scale[0]), wpo,
                                         ffn2[0][0], cache_k, k_new[:, None])
    pool_p = pool_hist[:, HALO - POOL_BUF:][None]
    x, xn, k_nat, v_nat, k_str, v_str, k_keep, v_keep, v_s = _layer0_tail(
        x, xn, s, keep, ffn2[0], vec(kv_norm), wkv, ffn1[1][0], tab_p, cache_v, v_new[:, None])
    x, *qs_by_group = _layer1_head(x, xn, s, ffn1[1], vec(mix_norm[1]), wq, tab_p)
    outs, lses = [], []
    for (win, dil), q in zip(DILATED_GROUPS, qs_by_group):
        assert win // dil == ATTN_BLOCK
        if dil == 1:
            nat = lambda a: a.reshape(b, 1, s, d)
            o, lse = _attn_prompt(nat(q), nat(k_nat), nat(v_nat), 1)
            outs.append(o.reshape(b * s, d))
            lses.append(lse.reshape(b * s, LANES))
        else:
            o, lse = _attn_prompt(q, k_str, v_str, dil)
            outs.append(o)
            lses.append(lse)
    y_prompt = _layer1_tail(outs, lses, x, s, wo, ffn2[1], vec(final_norm)).reshape(b, s, d)
    k_p = k_keep.reshape(b, keep, n_heads, HEAD_DIM)
    v_p = v_keep.reshape(b, keep, n_heads, HEAD_DIM)

    return (y_prompt, y_sample, pool_p, pool_s, k_p, v_p, k_s, v_s)
```
